```python
import math
import jax
import jax.numpy as jnp
from jax import lax
import numpy as np

D_MODEL = 1024
BATCH = 2
SEQ = 8192
DEPTH = 4

GRID_W = 64
CTX_LEN = 256
N_MIXERS = 3
N_SSD_LAYERS = (DEPTH + 2) // 3
N_DIFF_LAYERS = (DEPTH + 1) // 3
N_MLSTM_LAYERS = DEPTH // 3
RMS_EPS = 1e-6
N_MOD = 6

SSD_INNER = 2 * D_MODEL
SSD_HEAD_DIM = 64
SSD_HEADS = SSD_INNER // SSD_HEAD_DIM
SSD_GROUPS = 4
SSD_STATE = 128
SSD_CONV = 3
SSD_CHUNK = 128
SSD_BC = SSD_GROUPS * SSD_STATE
SSD_CONV_DIM = SSD_INNER + 2 * SSD_BC
SSD_IN = SSD_INNER + SSD_CONV_DIM + 2 * SSD_HEADS

DIFF_HEADS = 8
DIFF_HEAD_DIM = D_MODEL // DIFF_HEADS // 2
DIFF_V_DIM = 2 * DIFF_HEAD_DIM
ATTN_BLOCK = 128
ROPE_BASE = 10000.0
ROPE_Q = DIFF_HEAD_DIM // 4

MLSTM_HEADS = 4
MLSTM_QK_DIM = D_MODEL // 2 // MLSTM_HEADS
MLSTM_V_DIM = D_MODEL // MLSTM_HEADS
MLSTM_CHUNK = 128
MLSTM_QK = MLSTM_HEADS * MLSTM_QK_DIM
MLSTM_V = MLSTM_HEADS * MLSTM_V_DIM
MLSTM_IN = 2 * MLSTM_QK + 2 * MLSTM_V + 4 * MLSTM_HEADS

N_EXPERTS = 32
TOP_K = 4
D_EXPERT = D_MODEL
SWIGLU_LIMIT = 7.0
SWIGLU_ALPHA = 1.702
MOE_BLOCK = 256

kernel_name = 'hybrid_ssd_diffattn_mlstm_moe_dit'

F32 = jnp.float32


def rms_norm(x, g):
    x32 = x.astype(F32)
    y = x32 * lax.rsqrt(jnp.mean(x32 * x32, axis=-1, keepdims=True) + RMS_EPS)
    return (y * g.astype(F32)).astype(x.dtype)


def modulate(h, shift, scale):
    return h * (1.0 + scale) + shift


def flip(t):
    return jnp.flip(t, axis=1)


def to_chunks(t, size):
    return t.reshape(t.shape[0], t.shape[1] // size, size, *t.shape[2:]).swapaxes(0, 1)


def from_chunks(t):
    t = t.swapaxes(0, 1)
    return t.reshape(t.shape[0], t.shape[1] * t.shape[2], *t.shape[3:])


def centred_dwconv(u, w, b):
    ch = u.shape[-1]
    y = lax.conv_general_dilated(u, w[:, None, :].astype(u.dtype), window_strides=(1,),
                                 padding=[(SSD_CONV // 2, SSD_CONV // 2)],
                                 dimension_numbers=('NWC', 'WIO', 'NWC'), feature_group_count=ch)
    return y + b


def ssd_scan(x, dt, a, bm, cm, h0):
    bsz, seq, nh, hp = x.shape
    ng = bm.shape[2]
    hg = nh // ng
    xs = to_chunks(x.reshape(bsz, seq, ng, hg, hp), SSD_CHUNK)
    dts = to_chunks(dt.reshape(bsz, seq, ng, hg), SSD_CHUNK)
    bs = to_chunks(bm, SSD_CHUNK)
    cs = to_chunks(cm, SSD_CHUNK)
    a_g = a.reshape(ng, hg)
    causal = jnp.tril(jnp.ones((SSD_CHUNK, SSD_CHUNK), bool))[:, :, None, None]

    def step(h, inp):
        xc, dtc, bc, cc = inp
        la = jnp.cumsum(dtc * a_g, axis=1)
        decay = jnp.exp(jnp.where(causal, la[:, :, None] - la[:, None], -jnp.inf))
        cb = jnp.einsum('bign,bjgn->bijg', cc, bc)
        w = decay * cb[..., None] * dtc[:, None]
        y = (jnp.einsum('bijgh,bjghp->bighp', w, xc)
             + jnp.exp(la)[..., None] * jnp.einsum('bign,bghpn->bighp', cc, h))
        last = la[:, -1]
        ws = jnp.exp(last[:, None] - la) * dtc
        h = jnp.exp(last)[..., None, None] * h + jnp.einsum('bjgh,bjgn,bjghp->bghpn', ws, bc, xc)
        return h, y

    h_final, ys = lax.scan(step, h0, (xs, dts, bs, cs))
    return from_chunks(ys).reshape(bsz, seq, nh, hp), h_final


def ssd_mixer(hx, hc, w_in, conv_w, conv_b, dt_bias, a_log, d_skip, norm_w, w_out, need_ctx):
    def project(h):
        bsz, seq, _ = h.shape
        t = h @ w_in
        z = t[..., :SSD_INNER]
        xbc = jax.nn.silu(centred_dwconv(t[..., SSD_INNER:SSD_INNER + SSD_CONV_DIM], conv_w, conv_b)).astype(F32)
        dt = t[..., SSD_INNER + SSD_CONV_DIM:].astype(F32).reshape(bsz, seq, 2, SSD_HEADS)
        dt = jax.nn.softplus(dt + dt_bias.astype(F32))
        xs = xbc[..., :SSD_INNER].reshape(bsz, seq, SSD_HEADS, SSD_HEAD_DIM)
        bm = xbc[..., SSD_INNER:SSD_INNER + SSD_BC].reshape(bsz, seq, SSD_GROUPS, SSD_STATE)
        cm = xbc[..., SSD_INNER + SSD_BC:].reshape(bsz, seq, SSD_GROUPS, SSD_STATE)
        return z, xs, bm, cm, dt

    zc, xc, bc, cc, dtc = project(hc)
    zx, xx, bx, cx, dtx = project(hx)
    a = -jnp.exp(a_log.astype(F32))
    h0 = jnp.zeros((hx.shape[0], SSD_GROUPS, SSD_HEADS // SSD_GROUPS, SSD_HEAD_DIM, SSD_STATE), F32)
    yc_f, st_f = ssd_scan(xc, dtc[:, :, 0], a[0], bc, cc, h0)
    yx_f, _ = ssd_scan(xx, dtx[:, :, 0], a[0], bx, cx, st_f)
    yc_b, st_b = ssd_scan(flip(xc), flip(dtc[:, :, 1]), a[1], flip(bc), flip(cc), h0)
    yx_b, _ = ssd_scan(flip(xx), flip(dtx[:, :, 1]), a[1], flip(bx), flip(cx), st_b)

    def finish(y_f, y_b_rev, xs, z):
        y = y_f + flip(y_b_rev) + d_skip.astype(F32)[:, None] * xs
        y = y.reshape(y.shape[0], y.shape[1], SSD_INNER) * jax.nn.silu(z.astype(F32))
        return rms_norm(y, norm_w).astype(hx.dtype) @ w_out

    ux = finish(yx_f, yx_b, xx, zx)
    uc = finish(yc_f, yc_b, xc, zc) if need_ctx else None
    return ux, uc


def axial_rope_tables(rows):
    row = jnp.repeat(jnp.arange(rows, dtype=F32), GRID_W)
    col = jnp.tile(jnp.arange(GRID_W, dtype=F32), rows)
    inv = ROPE_BASE ** (-jnp.arange(ROPE_Q, dtype=F32) / ROPE_Q)
    ang = jnp.stack([row[:, None] * inv, col[:, None] * inv], axis=1)
    return jnp.cos(ang), jnp.sin(ang)


def axial_rope(x, cos, sin):
    xs = x.reshape(*x.shape[:-1], 2, 2, ROPE_Q)
    x1, x2 = xs[..., 0, :], xs[..., 1, :]
    c = cos[:, None, None]
    s = sin[:, None, None]
    out = jnp.stack([x1 * c - x2 * s, x2 * c + x1 * s], axis=-2)
    return out.reshape(x.shape).astype(x.dtype)


def diff_attention(hx, hc, w_qkv, lam, subln_w, w_out, lambda_init, cos, sin, need_ctx):
    bsz, seq, d = hx.shape

    def project(h):
        q, k, v = jnp.split(h @ w_qkv, 3, axis=-1)
        shp = h.shape[:2] + (DIFF_HEADS, 2, DIFF_HEAD_DIM)
        return q.reshape(shp), k.reshape(shp), v.reshape(h.shape[:2] + (DIFF_HEADS, DIFF_V_DIM))

    qx, kx, vx = project(hx)
    qc, kc, vc = project(hc)
    qx = axial_rope(qx, cos, sin)
    kx = axial_rope(kx, cos, sin)
    lam32 = lam.astype(F32)
    lam_full = jnp.exp(jnp.sum(lam32[0] * lam32[1])) - jnp.exp(jnp.sum(lam32[2] * lam32[3])) + lambda_init
    scale = DIFF_HEAD_DIM ** -0.5

    def attend(q, k, v):
        s = jnp.einsum('bqhcd,bkhcd->bhcqk', q, k).astype(F32) * scale
        p = jax.nn.softmax(s, axis=-1)
        amap = p[:, :, 0] - lam_full * p[:, :, 1]
        return jnp.einsum('bhqk,bkhv->bqhv', amap, v.astype(F32))

    def finish(o):
        o = rms_norm(o, subln_w) * (1.0 - lambda_init)
        return o.reshape(o.shape[0], o.shape[1], d).astype(hx.dtype) @ w_out

    keys = jnp.concatenate([kc, kx], axis=1)
    vals = jnp.concatenate([vc, vx], axis=1)
    nb = seq // ATTN_BLOCK
    q_blocks = qx.reshape(bsz, nb, ATTN_BLOCK, DIFF_HEADS, 2, DIFF_HEAD_DIM).swapaxes(0, 1)
    o_blocks = lax.map(lambda qb: attend(qb, keys, vals), q_blocks)
    ox = o_blocks.swapaxes(0, 1).reshape(bsz, seq, DIFF_HEADS, DIFF_V_DIM)
    ux = finish(ox)
    uc = finish(attend(qc, kc, vc)) if need_ctx else None
    return ux, uc


def mlstm_scan(q, k, v, ig, lf, state):
    causal = jnp.tril(jnp.ones((MLSTM_CHUNK, MLSTM_CHUNK), bool))[:, :, None]

    def step(carry, inp):
        cmat, nvec, m = carry
        qc, kc, vc, ic, fc = inp
        b = jnp.cumsum(fc, axis=1)
        g = b + m[:, None]
        dmat = jnp.where(causal, b[:, :, None] - b[:, None] + ic[:, None], -jnp.inf)
        mt = jnp.maximum(g, dmat.max(axis=2))
        s = jnp.einsum('bihd,bjhd->bijh', qc, kc) * jnp.exp(dmat - mt[:, :, None])
        inter = jnp.exp(g - mt)
        num = (jnp.einsum('bijh,bjhv->bihv', s, vc)
               + inter[..., None] * jnp.einsum('bihd,bhdv->bihv', qc, cmat))
        den = s.sum(axis=2) + inter * jnp.einsum('bihd,bhd->bih', qc, nvec)
        h = num / jnp.maximum(jnp.abs(den), jnp.exp(-mt))[..., None]
        m_new = mt[:, -1]
        wk = jnp.exp(b[:, -1:] - b + ic - m_new[:, None])
        carry_scale = jnp.exp(b[:, -1] + m - m_new)
        cmat = carry_scale[..., None, None] * cmat + jnp.einsum('bjh,bjhd,bjhv->bhdv', wk, kc, vc)
        nvec = carry_scale[..., None] * nvec + jnp.einsum('bjh,bjhd->bhd', wk, kc)
        return (cmat, nvec, m_new), h

    inputs = tuple(to_chunks(t, MLSTM_CHUNK) for t in (q, k, v, ig, lf))
    final, hs = lax.scan(step, state, inputs)
    return from_chunks(hs), final


def mlstm_mixer(hx, hc, w_in, b_gates, norm_w, w_out, need_ctx):
    def project(h):
        bsz, seq, _ = h.shape
        t = (h @ w_in).astype(F32)
        q = t[..., :MLSTM_QK].reshape(bsz, seq, MLSTM_HEADS, MLSTM_QK_DIM) * MLSTM_QK_DIM ** -0.5
        k = t[..., MLSTM_QK:2 * MLSTM_QK].reshape(bsz, seq, MLSTM_HEADS, MLSTM_QK_DIM)
        v = t[..., 2 * MLSTM_QK:2 * MLSTM_QK + MLSTM_V].reshape(bsz, seq, MLSTM_HEADS, MLSTM_V_DIM)
        o = jax.nn.sigmoid(t[..., 2 * MLSTM_QK + MLSTM_V:2 * MLSTM_QK + 2 * MLSTM_V]).reshape(bsz, seq, MLSTM_HEADS, MLSTM_V_DIM)
        gates = t[..., 2 * MLSTM_QK + 2 * MLSTM_V:].reshape(bsz, seq, 4, MLSTM_HEADS) + b_gates.astype(F32)
        return q, k, v, o, gates

    qx, kx, vx, ox, gx = project(hx)
    qc, kc, vc, oc, gc = project(hc)
    bsz = hx.shape[0]
    state0 = (jnp.zeros((bsz, MLSTM_HEADS, MLSTM_QK_DIM, MLSTM_V_DIM), F32),
              jnp.zeros((bsz, MLSTM_HEADS, MLSTM_QK_DIM), F32),
              jnp.zeros((bsz, MLSTM_HEADS), F32))

    def run(q, k, v, g, direction, state):
        ig = g[:, :, 2 * direction]
        lf = jax.nn.log_sigmoid(g[:, :, 2 * direction + 1])
        return mlstm_scan(q, k, v, ig, lf, state)

    hc_f, st_f = run(qc, kc, vc, gc, 0, state0)
    hx_f, _ = run(qx, kx, vx, gx, 0, st_f)
    hc_b, st_b = run(flip(qc), flip(kc), flip(vc), flip(gc), 1, state0)
    hx_b, _ = run(flip(qx), flip(kx), flip(vx), flip(gx), 1, st_b)

    def finish(h_f, h_b_rev, o):
        h = o * (h_f + flip(h_b_rev))
        h = rms_norm(h, norm_w)
        return h.reshape(h.shape[0], h.shape[1], MLSTM_V).astype(hx.dtype) @ w_out

    ux = finish(hx_f, hx_b, ox)
    uc = finish(hc_f, hc_b, oc) if need_ctx else None
    return ux, uc


def moe_ffn(h, w_router, b_router, w_gu, b_gu, w_dn, b_dn):
    n_tok, d = h.shape
    n_assign = n_tok * TOP_K
    logits = (h @ w_router + b_router).astype(F32)
    top_val, top_idx = lax.top_k(logits, TOP_K)
    gate = jax.nn.softmax(top_val, axis=-1).reshape(n_assign)
    expert = top_idx.reshape(n_assign)
    token = jnp.repeat(jnp.arange(n_tok, dtype=jnp.int32), TOP_K)
    order = jnp.argsort(expert, stable=True)
    e_s, t_s, g_s = expert[order], token[order], gate[order]
    counts = jnp.bincount(expert, length=N_EXPERTS)
    starts = jnp.cumsum(counts) - counts
    padded = (counts + MOE_BLOCK - 1) // MOE_BLOCK * MOE_BLOCK
    pad_ends = jnp.cumsum(padded)
    pad_starts = pad_ends - padded
    dest = pad_starts[e_s] + jnp.arange(n_assign) - starts[e_s]
    n_blocks = -(-(n_assign + N_EXPERTS * (MOE_BLOCK - 1)) // MOE_BLOCK)
    row_tok = jnp.full((n_blocks * MOE_BLOCK,), n_tok, jnp.int32).at[dest].set(t_s)
    block_expert = jnp.minimum(jnp.searchsorted(pad_ends, jnp.arange(n_blocks) * MOE_BLOCK, side='right'),
                               N_EXPERTS - 1)
    h_pad = jnp.concatenate([h, jnp.zeros((1, d), h.dtype)], axis=0)
    xb = h_pad[row_tok].reshape(n_blocks, MOE_BLOCK, d)

    def expert_block(args):
        xblk, e = args
        gu = xblk @ w_gu[e] + b_gu[e]
        g = jnp.minimum(gu[:, :D_EXPERT], SWIGLU_LIMIT)
        u = jnp.clip(gu[:, D_EXPERT:], -SWIGLU_LIMIT, SWIGLU_LIMIT)
        return ((u + 1.0) * g * jax.nn.sigmoid(SWIGLU_ALPHA * g)) @ w_dn[e] + b_dn[e]

    yb = lax.map(expert_block, (xb, block_expert)).reshape(n_blocks * MOE_BLOCK, d)
    y = yb[dest].astype(F32) * g_s[:, None]
    return jax.ops.segment_sum(y, t_s, num_segments=n_tok).astype(h.dtype)


def setup_inputs(seed: int = 0) -> dict:
    key = jax.random.key(seed)
    k = jax.random.split(key, 32)

    def nrm(i, shape, scale):
        return jax.random.normal(k[i], shape, F32) * scale

    D = D_MODEL
    nS, nD, nM = N_SSD_LAYERS, N_DIFF_LAYERS, N_MLSTM_LAYERS
    dt0 = jnp.exp(jax.random.uniform(k[10], (nS, 2, SSD_HEADS), F32, math.log(1e-3), math.log(1e-1)))
    return {
        'x': nrm(0, (BATCH, SEQ, D), 1.0),
        'c': nrm(1, (BATCH, D), 1.0),
        'ctx': nrm(2, (BATCH, CTX_LEN, D), 1.0),
        'c_ctx': nrm(3, (D,), 1.0),
        'ada_w': nrm(4, (DEPTH, D, N_MOD * D), 0.5 * D ** -0.5),
        'ada_b': nrm(5, (DEPTH, N_MOD * D), 0.02),
        'norm1_g': 1.0 + nrm(6, (DEPTH, D), 0.02),
        'norm2_g': 1.0 + nrm(7, (DEPTH, D), 0.02),
        'ssd_w_in': nrm(8, (nS, D, SSD_IN), D ** -0.5),
        'ssd_conv_w': nrm(9, (nS, SSD_CONV, SSD_CONV_DIM), SSD_CONV ** -0.5),
        'ssd_conv_b': nrm(11, (nS, SSD_CONV_DIM), 0.02),
        'ssd_dt_bias': dt0 + jnp.log(-jnp.expm1(-dt0)),
        'ssd_a_log': jnp.log(jax.random.uniform(k[12], (nS, 2, SSD_HEADS), F32, 1.0, 16.0)),
        'ssd_d': 1.0 + nrm(13, (nS, SSD_HEADS), 0.1),
        'ssd_norm_w': 1.0 + nrm(14, (nS, SSD_INNER), 0.02),
        'ssd_w_out': nrm(15, (nS, SSD_INNER, D), SSD_INNER ** -0.5),
        'diff_w_qkv': nrm(16, (nD, D, 3 * D), D ** -0.5),
        'diff_lam': nrm(17, (nD, 4, DIFF_HEAD_DIM), 0.1),
        'diff_subln_w': 1.0 + nrm(18, (nD, DIFF_V_DIM), 0.02),
        'diff_w_out': nrm(19, (nD, DIFF_HEADS * DIFF_V_DIM, D), (DIFF_HEADS * DIFF_V_DIM) ** -0.5),
        'ml_w_in': nrm(20, (nM, D, MLSTM_IN), D ** -0.5),
        'ml_b_gates': nrm(21, (nM, 4, MLSTM_HEADS), 0.1) + jnp.array([0.0, 3.0, 0.0, 3.0], F32)[:, None],
        'ml_norm_w': 1.0 + nrm(22, (nM, MLSTM_V_DIM), 0.02),
        'ml_w_out': nrm(23, (nM, MLSTM_V, D), MLSTM_V ** -0.5),
        'moe_w_router': nrm(24, (DEPTH, D, N_EXPERTS), D ** -0.5),
        'moe_b_router': nrm(25, (DEPTH, N_EXPERTS), 0.01),
        'moe_w_gu': nrm(26, (DEPTH, N_EXPERTS, D, 2 * D_EXPERT), D ** -0.5),
        'moe_b_gu': nrm(27, (DEPTH, N_EXPERTS, 2 * D_EXPERT), 0.02),
        'moe_w_dn': nrm(28, (DEPTH, N_EXPERTS, D_EXPERT, D), D_EXPERT ** -0.5),
        'moe_b_dn': nrm(29, (DEPTH, N_EXPERTS, D), 0.02),
        'final_g': 1.0 + nrm(30, (D,), 0.02),
    }


def reference(x, c, ctx, c_ctx, ada_w, ada_b, norm1_g, norm2_g, ssd_w_in, ssd_conv_w, ssd_conv_b,
              ssd_dt_bias, ssd_a_log, ssd_d, ssd_norm_w, ssd_w_out, diff_w_qkv, diff_lam, diff_subln_w,
              diff_w_out, ml_w_in, ml_b_gates, ml_norm_w, ml_w_out, moe_w_router, moe_b_router, moe_w_gu,
              moe_b_gu, moe_w_dn, moe_b_dn, final_g):
    bsz, seq, d = x.shape
    rows = seq // GRID_W
    cos, sin = axial_rope_tables(rows)
    cond_lat = jax.nn.silu(c)
    cond_ctx = jax.nn.silu(c_ctx)
    n_lat = bsz * seq
    for i in range(DEPTH):
        last = i == DEPTH - 1
        mod_x = (cond_lat @ ada_w[i] + ada_b[i]).reshape(bsz, 1, N_MOD, d)
        mod_c = (cond_ctx @ ada_w[i] + ada_b[i]).reshape(1, 1, N_MOD, d)
        hx = modulate(rms_norm(x, norm1_g[i]), mod_x[:, :, 0], mod_x[:, :, 1])
        hc = modulate(rms_norm(ctx, norm1_g[i]), mod_c[:, :, 0], mod_c[:, :, 1])
        kind = i % N_MIXERS
        j = i // N_MIXERS
        if kind == 0:
            ux, uc = ssd_mixer(hx, hc, ssd_w_in[j], ssd_conv_w[j], ssd_conv_b[j], ssd_dt_bias[j],
                               ssd_a_log[j], ssd_d[j], ssd_norm_w[j], ssd_w_out[j], not last)
        elif kind == 1:
            lambda_init = 0.8 - 0.6 * math.exp(-0.3 * i)
            ux, uc = diff_attention(hx, hc, diff_w_qkv[j], diff_lam[j], diff_subln_w[j], diff_w_out[j],
                                    lambda_init, cos, sin, not last)
        else:
            ux, uc = mlstm_mixer(hx, hc, ml_w_in[j], ml_b_gates[j], ml_norm_w[j], ml_w_out[j], not last)
        x = x + mod_x[:, :, 2] * ux
        hx = modulate(rms_norm(x, norm2_g[i]), mod_x[:, :, 3], mod_x[:, :, 4]).reshape(n_lat, d)
        if last:
            fx = moe_ffn(hx, moe_w_router[i], moe_b_router[i], moe_w_gu[i], moe_b_gu[i],
                         moe_w_dn[i], moe_b_dn[i])
        else:
            ctx = ctx + mod_c[:, :, 2] * uc
            hc = modulate(rms_norm(ctx, norm2_g[i]), mod_c[:, :, 3], mod_c[:, :, 4]).reshape(-1, d)
            f = moe_ffn(jnp.concatenate([hx, hc], axis=0), moe_w_router[i], moe_b_router[i],
                        moe_w_gu[i], moe_b_gu[i], moe_w_dn[i], moe_b_dn[i])
            fx = f[:n_lat]
            ctx = ctx + mod_c[:, :, 5] * f[n_lat:].reshape(ctx.shape)
        x = x + mod_x[:, :, 5] * fx.reshape(x.shape)
    return rms_norm(x, final_g)
```

```python
import functools
import math

import numpy as np
import jax
import jax.numpy as jnp
from jax import lax
from jax.experimental import pallas as pl
from jax.experimental.pallas import tpu as pltpu

F32 = jnp.float32
BF16 = jnp.bfloat16
HIGHEST = lax.Precision.HIGHEST

GRID_W = 64
RMS_EPS = 1e-6
N_MOD = 6
SSD_HEAD_DIM = 64
SSD_HEADS = 32
SSD_GROUPS = 4
SSD_STATE = 128
SSD_INNER = SSD_HEADS * SSD_HEAD_DIM
SSD_BC = SSD_GROUPS * SSD_STATE
SSD_CONV_DIM = SSD_INNER + 2 * SSD_BC
DIFF_HEADS = 8
DIFF_HEAD_DIM = 64
DIFF_V_DIM = 128
ROPE_BASE = 10000.0
ROPE_Q = DIFF_HEAD_DIM // 4
ML_HEADS = 4
ML_QK_DIM = 128
ML_V_DIM = 256
ML_QK = ML_HEADS * ML_QK_DIM
ML_V = ML_HEADS * ML_V_DIM
N_EXPERTS = 32
TOP_K = 4
SWIGLU_LIMIT = 7.0
SWIGLU_ALPHA = 1.702

LANES = 128
CHUNK = 128
ROW_TILE = 256
MOE_BLOCK = 256
VMEM_LIMIT = 56 * 1024 * 1024


def _params(sem):
    return pltpu.CompilerParams(dimension_semantics=sem, vmem_limit_bytes=VMEM_LIMIT)


def _dot(a, b):
    return jnp.dot(a, b, preferred_element_type=F32)


def _dot_nt(a, b):
    return lax.dot_general(a, b, (((1,), (1,)), ((), ())), preferred_element_type=F32)


def _dot_tn(a, b):
    return lax.dot_general(a, b, (((0,), (0,)), ((), ())), preferred_element_type=F32)


def _rms(x):
    return x * lax.rsqrt(jnp.mean(x * x, axis=-1, keepdims=True) + RMS_EPS)


def _sigmoid(x):
    return 1.0 / (1.0 + jnp.exp(-x))


def _silu(x):
    return x * _sigmoid(x)


def _softplus(x):
    return jnp.maximum(x, 0.0) + jnp.log(1.0 + jnp.exp(-jnp.abs(x)))


def _log_sigmoid(x):
    return -_softplus(-x)


def _mod_spec(which, lat_tiles_per_batch, bsz, d):
    return pl.BlockSpec((None, None, 1, d),
                        lambda i: (which, jnp.minimum(i // lat_tiles_per_batch, bsz), 0, 0))


def _mods_kernel(c_ref, w_ref, b_ref, o_ref):
    c = c_ref[...]
    cond = _silu(c)
    o_ref[...] = jnp.dot(cond, w_ref[...], precision=HIGHEST, preferred_element_type=F32) + b_ref[...]


def _mods(cond_rows, ada_w, ada_b):
    depth, d, _ = ada_w.shape
    nr = cond_rows.shape[0]
    return pl.pallas_call(
        _mods_kernel,
        grid=(depth, N_MOD),
        in_specs=[pl.BlockSpec((nr, d), lambda l, j: (0, 0)),
                  pl.BlockSpec((None, d, d), lambda l, j: (l, 0, j)),
                  pl.BlockSpec((None, 1, d), lambda l, j: (l, 0, j))],
        out_specs=pl.BlockSpec((None, None, nr, d), lambda l, j: (l, j, 0, 0)),
        out_shape=jax.ShapeDtypeStruct((depth, N_MOD, nr, d), F32),
        compiler_params=_params(("arbitrary", "arbitrary")),
        name="adaln_mods",
    )(cond_rows, ada_w, ada_b.reshape(depth, 1, N_MOD * d))


def _norm_mod(x, g, sh, sc):
    return (_rms(x) * g) * (1.0 + sc) + sh


def _inproj_kernel(x_ref, g_ref, sh_ref, sc_ref, w_ref, o_ref, *, n_chunk):
    hb = _norm_mod(x_ref[...], g_ref[...], sh_ref[...], sc_ref[...]).astype(BF16)
    n = o_ref.shape[1]
    for n0 in range(0, n, n_chunk):
        o_ref[:, n0:n0 + n_chunk] = _dot(hb, w_ref[:, n0:n0 + n_chunk])


def _col_chunk(n):
    for c in (512, 384, 256, 128):
        if n % c == 0:
            return c
    raise ValueError(n)


def _inproj(xs, g, mods_l, w_bf16, dims):
    bsz, seq, ctx = dims
    r, d = xs.shape
    n = w_bf16.shape[1]
    tm = ROW_TILE
    lt = seq // tm
    return pl.pallas_call(
        functools.partial(_inproj_kernel, n_chunk=_col_chunk(n)),
        grid=(r // tm,),
        in_specs=[pl.BlockSpec((tm, d), lambda i: (i, 0)),
                  pl.BlockSpec((1, d), lambda i: (0, 0)),
                  _mod_spec(0, lt, bsz, d), _mod_spec(1, lt, bsz, d),
                  pl.BlockSpec((d, n), lambda i: (0, 0))],
        out_specs=pl.BlockSpec((tm, n), lambda i: (i, 0)),
        out_shape=jax.ShapeDtypeStruct((r, n), F32),
        compiler_params=_params(("arbitrary",)),
        name="norm_mod_inproj",
    )(xs, g.reshape(1, d), mods_l, mods_l, w_bf16)


def _conv_kernel(first_ref, last_ref, cur_ref, prev_ref, nxt_ref, w_ref, b_ref, o_ref):
    i = pl.program_id(0)
    cur = cur_ref[...]
    tm = cur.shape[0]
    row = lax.broadcasted_iota(jnp.int32, cur.shape, 0)
    prev_row = jnp.where(first_ref[i] == 1, 0.0, prev_ref[7:8, :])
    next_row = jnp.where(last_ref[i] == 1, 0.0, nxt_ref[0:1, :])
    down = jnp.where(row == 0, prev_row, pltpu.roll(cur, 1, 0))
    up = jnp.where(row == tm - 1, next_row, pltpu.roll(cur, tm - 1, 0))
    y = w_ref[0:1, :] * down + w_ref[1:2, :] * cur + w_ref[2:3, :] * up + b_ref[...]
    o_ref[...] = _silu(y)


def _ssd_conv(t_main, conv_w, conv_b, dims):
    bsz, seq, ctx = dims
    r = t_main.shape[0]
    tm = ROW_TILE
    cw = 1024
    ncol = SSD_CONV_DIM // cw
    off = SSD_INNER // cw
    starts = [b * seq for b in range(bsz)] + [bsz * seq + b * ctx for b in range(bsz)]
    ends = [s + (seq if k < bsz else ctx) for k, s in enumerate(starts)]
    first = np.array([1 if (i * tm) in starts else 0 for i in range(r // tm)], np.int32)
    last = np.array([1 if ((i + 1) * tm) in ends else 0 for i in range(r // tm)], np.int32)
    sub = tm // 8
    nblk8 = r // 8
    grid_spec = pltpu.PrefetchScalarGridSpec(
        num_scalar_prefetch=2,
        grid=(r // tm, ncol),
        in_specs=[pl.BlockSpec((tm, cw), lambda i, j, f, l: (i, off + j)),
                  pl.BlockSpec((8, cw), lambda i, j, f, l: (jnp.maximum(i * sub - 1, 0), off + j)),
                  pl.BlockSpec((8, cw), lambda i, j, f, l: (jnp.minimum((i + 1) * sub, nblk8 - 1), off + j)),
                  pl.BlockSpec((3, cw), lambda i, j, f, l: (0, j)),
                  pl.BlockSpec((1, cw), lambda i, j, f, l: (0, j))],
        out_specs=pl.BlockSpec((tm, cw), lambda i, j, f, l: (i, j)),
    )
    return pl.pallas_call(
        _conv_kernel,
        grid_spec=grid_spec,
        out_shape=jax.ShapeDtypeStruct((r, SSD_CONV_DIM), F32),
        compiler_params=_params(("arbitrary", "arbitrary")),
        name="ssd_conv_silu",
    )(jnp.asarray(first), jnp.asarray(last), t_main, t_main, t_main, conv_w, conv_b.reshape(1, -1))


def _chunk_block(b, d, s, bsz, seq, ctx):
    nc_ctx = ctx // CHUNK
    nc_lat = seq // CHUNK
    in_ctx = s < nc_ctx
    pc = jnp.where(d == 0, s, nc_ctx - 1 - s)
    pls = s - nc_ctx
    plat = jnp.where(d == 0, pls, nc_lat - 1 - pls)
    ctx_blk = bsz * nc_lat + b * nc_ctx + pc
    lat_blk = b * nc_lat + plat
    return jnp.where(in_ctx, ctx_blk, lat_blk)


def _dir_masks(d):
    ii = lax.broadcasted_iota(jnp.int32, (CHUNK, CHUNK), 0)
    jj = lax.broadcasted_iota(jnp.int32, (CHUNK, CHUNK), 1)
    sign = 1 - 2 * d
    return (ii - jj) * sign >= 0


def _ssd_scan_kernel(x_ref, b_ref, c_ref, dt_ref, bias_ref, a_ref, y_ref, st_ref):
    d = pl.program_id(1)
    s = pl.program_id(2)

    @pl.when(s == 0)
    def _():
        st_ref[...] = jnp.zeros_like(st_ref)

    mask = _dir_masks(d)
    tri = jnp.where(mask, 1.0, 0.0).astype(F32)
    dt = _softplus(dt_ref[...] + bias_ref[...])
    dta = dt * a_ref[...]
    la = jnp.dot(tri, dta, precision=HIGHEST, preferred_element_type=F32)
    total = jnp.sum(dta, axis=0, keepdims=True)
    la_t = la.T
    dt_t = dt.T
    e_la = jnp.exp(la)
    ws = jnp.exp(total - la) * dt
    e_tot = jnp.exp(total)
    lane = lax.broadcasted_iota(jnp.int32, (CHUNK, LANES), 1)
    lo_half = lane < SSD_HEAD_DIM
    hg = SSD_HEADS // SSD_GROUPS
    for g in range(SSD_GROUPS):
        cg = c_ref[:, g * SSD_STATE:(g + 1) * SSD_STATE].astype(BF16)
        bg = b_ref[:, g * SSD_STATE:(g + 1) * SSD_STATE].astype(BF16)
        cb = _dot_nt(cg, bg)
        for pr in range(hg // 2):
            h0 = g * hg + 2 * pr
            col0 = slice((h0 * SSD_HEAD_DIM), (h0 + 2) * SSD_HEAD_DIM)
            xp = x_ref[:, col0]
            xpb = xp.astype(BF16)
            ys = []
            for h in (h0, h0 + 1):
                diff = la[:, h:h + 1] - la_t[h:h + 1, :]
                w = jnp.exp(jnp.where(mask, diff, -jnp.inf)) * cb * dt_t[h:h + 1, :]
                ys.append(_dot(w.astype(BF16), xpb))
            y_intra = jnp.where(lo_half, ys[0], ys[1])
            st = st_ref[g, :, pr * LANES:(pr + 1) * LANES]
            e_pair = jnp.where(lo_half, e_la[:, h0:h0 + 1], e_la[:, h0 + 1:h0 + 2])
            y_ref[:, col0] = y_intra + e_pair * _dot(cg, st.astype(BF16))
            ws_pair = jnp.where(lo_half, ws[:, h0:h0 + 1], ws[:, h0 + 1:h0 + 2])
            et_pair = jnp.where(lo_half[0:1], e_tot[:, h0:h0 + 1], e_tot[:, h0 + 1:h0 + 2])
            st_ref[g, :, pr * LANES:(pr + 1) * LANES] = et_pair * st + _dot_tn(bg, (ws_pair * xp).astype(BF16))


def _ssd_scan(xbc, dt_raw, dt_bias_pad, a_pad, dims):
    bsz, seq, ctx = dims
    r = xbc.shape[0]
    nsteps = (seq + ctx) // CHUNK
    blk = functools.partial(_chunk_block, bsz=bsz, seq=seq, ctx=ctx)
    xw = SSD_INNER
    return pl.pallas_call(
        _ssd_scan_kernel,
        grid=(bsz, 2, nsteps),
        in_specs=[pl.BlockSpec((CHUNK, xw), lambda b, d, s: (blk(b, d, s), 0)),
                  pl.BlockSpec((CHUNK, SSD_BC), lambda b, d, s: (blk(b, d, s), xw // SSD_BC)),
                  pl.BlockSpec((CHUNK, SSD_BC), lambda b, d, s: (blk(b, d, s), xw // SSD_BC + 1)),
                  pl.BlockSpec((CHUNK, LANES), lambda b, d, s: (blk(b, d, s), d)),
                  pl.BlockSpec((None, 1, LANES), lambda b, d, s: (d, 0, 0)),
                  pl.BlockSpec((None, 1, LANES), lambda b, d, s: (d, 0, 0))],
        out_specs=pl.BlockSpec((None, CHUNK, xw), lambda b, d, s: (d, blk(b, d, s), 0)),
        out_shape=jax.ShapeDtypeStruct((2, r, xw), F32),
        scratch_shapes=[pltpu.VMEM((SSD_GROUPS, SSD_STATE, xw // SSD_GROUPS), F32)],
        compiler_params=_params(("arbitrary", "arbitrary", "arbitrary")),
        name="ssd_scan",
    )(xbc, xbc, xbc, dt_raw, dt_bias_pad, a_pad)


def _ssd_out_kernel(yf_ref, yb_ref, xc_ref, z_ref, dexp_ref, nw_ref, w_ref, xs_ref, gate_ref, o_ref):
    y = yf_ref[...] + yb_ref[...] + dexp_ref[...] * xc_ref[...]
    y = y * _silu(z_ref[...])
    a = (_rms(y) * nw_ref[...]).astype(BF16)
    o_ref[...] = xs_ref[...] + gate_ref[...] * _dot(a, w_ref[...])


def _ssd_out(y2, xbc, t_main, d_exp, norm_w, w_out_bf16, xs, mods_l, dims):
    bsz, seq, ctx = dims
    r, d = xs.shape
    tm = ROW_TILE
    lt = seq // tm
    xw = SSD_INNER
    return pl.pallas_call(
        _ssd_out_kernel,
        grid=(r // tm,),
        in_specs=[pl.BlockSpec((None, tm, xw), lambda i: (0, i, 0)),
                  pl.BlockSpec((None, tm, xw), lambda i: (1, i, 0)),
                  pl.BlockSpec((tm, xw), lambda i: (i, 0)),
                  pl.BlockSpec((tm, xw), lambda i: (i, 0)),
                  pl.BlockSpec((1, xw), lambda i: (0, 0)),
                  pl.BlockSpec((1, xw), lambda i: (0, 0)),
                  pl.BlockSpec((xw, d), lambda i: (0, 0)),
                  pl.BlockSpec((tm, d), lambda i: (i, 0)),
                  _mod_spec(2, lt, bsz, d)],
        out_specs=pl.BlockSpec((tm, d), lambda i: (i, 0)),
        out_shape=jax.ShapeDtypeStruct((r, d), F32),
        compiler_params=_params(("arbitrary",)),
        name="ssd_gated_norm_outproj",
    )(y2, y2, xbc, t_main, d_exp, norm_w.reshape(1, xw), w_out_bf16, xs, mods_l)


def _ssd_layer(xs, mods_l, g1, w_in, conv_w, conv_b, dt_bias, a_log, d_skip, norm_w, w_out, dims):
    d = xs.shape[1]
    main = SSD_INNER + SSD_CONV_DIM
    w_main = w_in[:, :main].astype(BF16)
    w_dt = jnp.zeros((d, 2 * LANES), F32)
    w_dt = w_dt.at[:, :SSD_HEADS].set(w_in[:, main:main + SSD_HEADS])
    w_dt = w_dt.at[:, LANES:LANES + SSD_HEADS].set(w_in[:, main + SSD_HEADS:]).astype(BF16)
    t_main = _inproj(xs, g1, mods_l, w_main, dims)
    dt_raw = _inproj(xs, g1, mods_l, w_dt, dims)
    xbc = _ssd_conv(t_main, conv_w, conv_b, dims)
    pad = jnp.zeros((2, 1, LANES - SSD_HEADS), F32)
    bias_pad = jnp.concatenate([dt_bias.astype(F32).reshape(2, 1, SSD_HEADS), pad], axis=-1)
    a_pad = jnp.concatenate([-jnp.exp(a_log.astype(F32)).reshape(2, 1, SSD_HEADS), pad], axis=-1)
    y2 = _ssd_scan(xbc, dt_raw, bias_pad, a_pad, dims)
    d_exp = jnp.repeat(d_skip.astype(F32), SSD_HEAD_DIM).reshape(1, SSD_INNER)
    return _ssd_out(y2, xbc, t_main, d_exp, norm_w, w_out.astype(BF16), xs, mods_l, dims)


def _qkv_rope_kernel(x_ref, g_ref, sh_ref, sc_ref, w_ref, cos_ref, sin_ref, q_ref, k_ref, v_ref, *, n_lat_tiles):
    i = pl.program_id(0)
    hb = _norm_mod(x_ref[...], g_ref[...], sh_ref[...], sc_ref[...]).astype(BF16)
    d = x_ref.shape[1]
    is_ctx = i >= n_lat_tiles
    cos = jnp.where(is_ctx, 1.0, cos_ref[...])
    sin = jnp.where(is_ctx, 0.0, sin_ref[...])
    nrep = d // LANES
    cos = jnp.concatenate([cos] * nrep, axis=1)
    sin = jnp.concatenate([sin] * nrep, axis=1)

    def mm(c):
        return _dot(hb, w_ref[:, c * d:(c + 1) * d])

    q = mm(0) * cos + mm(3) * sin
    q_ref[...] = (q * (DIFF_HEAD_DIM ** -0.5)).astype(BF16)
    k_ref[...] = (mm(1) * cos + mm(4) * sin).astype(BF16)
    v_ref[...] = mm(2).astype(BF16)


def _rope_tables(seq):
    rows = seq // GRID_W
    row = jnp.repeat(jnp.arange(rows, dtype=F32), GRID_W)
    col = jnp.tile(jnp.arange(GRID_W, dtype=F32), rows)
    inv = ROPE_BASE ** (-jnp.arange(ROPE_Q, dtype=F32) / ROPE_Q)
    ang_r = row[:, None] * inv
    ang_c = col[:, None] * inv
    cos = jnp.concatenate([jnp.cos(ang_r), jnp.cos(ang_r), jnp.cos(ang_c), jnp.cos(ang_c)], axis=1)
    sin = jnp.concatenate([-jnp.sin(ang_r), jnp.sin(ang_r), -jnp.sin(ang_c), jnp.sin(ang_c)], axis=1)
    return jnp.tile(cos, (1, 2)), jnp.tile(sin, (1, 2))


def _rope_partner_cols(d):
    col = np.arange(d)
    within = col % (2 * ROPE_Q)
    return np.where(within < ROPE_Q, col + ROPE_Q, col - ROPE_Q)


def _qkv_rope(xs, g, mods_l, w_qkv, dims):
    bsz, seq, ctx = dims
    r, d = xs.shape
    tm = ROW_TILE
    lt = seq // tm
    perm = _rope_partner_cols(d)
    wq, wk, wv = w_qkv[:, :d], w_qkv[:, d:2 * d], w_qkv[:, 2 * d:]
    w_all = jnp.concatenate([wq, wk, wv, wq[:, perm], wk[:, perm]], axis=1).astype(BF16)
    cos, sin = _rope_tables(seq)
    out = jax.ShapeDtypeStruct((r, d), BF16)
    row_spec = pl.BlockSpec((tm, d), lambda i: (i, 0))
    tab_spec = pl.BlockSpec((tm, LANES), lambda i: (i % lt, 0))
    return pl.pallas_call(
        functools.partial(_qkv_rope_kernel, n_lat_tiles=bsz * lt),
        grid=(r // tm,),
        in_specs=[row_spec, pl.BlockSpec((1, d), lambda i: (0, 0)),
                  _mod_spec(0, lt, bsz, d), _mod_spec(1, lt, bsz, d),
                  pl.BlockSpec((d, 5 * d), lambda i: (0, 0)), tab_spec, tab_spec],
        out_specs=[row_spec, row_spec, row_spec],
        out_shape=[out, out, out],
        compiler_params=_params(("arbitrary",)),
        name="norm_mod_qkv_rope",
    )(xs, g.reshape(1, d), mods_l, mods_l, w_all, cos, sin)


def _attn_kernel(lam_ref, q_ref, kc_ref, vc_ref, kx_ref, vx_ref, nw_ref, o_ref,
                 sc_ref, sx_ref, *, n_lat_q, tk, lambda_init):
    i = pl.program_id(2)
    q = q_ref[...]
    tq = q.shape[0]
    lane = lax.broadcasted_iota(jnp.int32, q.shape, 1)
    zero = jnp.zeros_like(q)
    qs = (jnp.where(lane < DIFF_HEAD_DIM, q, zero), jnp.where(lane >= DIFF_HEAD_DIM, q, zero))
    n_x = jnp.where(i < n_lat_q, kx_ref.shape[0] // tk, 0)

    kc = kc_ref[...]
    m = []
    for c in range(2):
        s = _dot_nt(qs[c], kc)
        sc_ref[c] = s
        m.append(jnp.max(s, axis=1, keepdims=True))

    def p1(j, carry):
        kj = kx_ref[pl.ds(pl.multiple_of(j * tk, tk), tk), :]
        out = []
        for c in range(2):
            s = _dot_nt(qs[c], kj)
            sx_ref[c, j] = s
            out.append(jnp.maximum(carry[c], jnp.max(s, axis=1, keepdims=True)))
        return tuple(out)

    m = lax.fori_loop(0, n_x, p1, tuple(m))

    vc = vc_ref[...]
    l, acc = [], []
    for c in range(2):
        p = jnp.exp(sc_ref[c] - m[c])
        l.append(jnp.sum(p, axis=1, keepdims=True))
        acc.append(_dot(p.astype(BF16), vc))

    def p2(j, carry):
        vj = vx_ref[pl.ds(pl.multiple_of(j * tk, tk), tk), :]
        l_, a_ = carry
        lo, ao = [], []
        for c in range(2):
            p = jnp.exp(sx_ref[c, j] - m[c])
            lo.append(l_[c] + jnp.sum(p, axis=1, keepdims=True))
            ao.append(a_[c] + _dot(p.astype(BF16), vj))
        return tuple(lo), tuple(ao)

    l, acc = lax.fori_loop(0, n_x, p2, (tuple(l), tuple(acc)))
    o = acc[0] / l[0] - lam_ref[0] * (acc[1] / l[1])
    o = _rms(o) * nw_ref[...] * (1.0 - lambda_init)
    o_ref[...] = o.astype(o_ref.dtype)


def _diff_attention(q, k, v, lam_full, subln_w, lambda_init, dims):
    bsz, seq, ctx = dims
    r, d = q.shape
    tq = min(256, ctx)
    tk = min(512, seq)
    n_lat_q = seq // tq
    n_ctx_q = ctx // tq
    nq = n_lat_q + n_ctx_q

    def q_idx(b, h, i):
        return (jnp.where(i < n_lat_q, b * n_lat_q + i, bsz * n_lat_q + b * n_ctx_q + (i - n_lat_q)), h)

    ctx_spec = pl.BlockSpec((ctx, LANES), lambda b, h, i: (bsz * seq // ctx + b, h))
    lat_spec = pl.BlockSpec((seq, LANES), lambda b, h, i: (b, h))
    return pl.pallas_call(
        functools.partial(_attn_kernel, n_lat_q=n_lat_q, tk=tk, lambda_init=lambda_init),
        grid=(bsz, DIFF_HEADS, nq),
        in_specs=[pl.BlockSpec(memory_space=pltpu.SMEM),
                  pl.BlockSpec((tq, LANES), q_idx),
                  ctx_spec, ctx_spec, lat_spec, lat_spec,
                  pl.BlockSpec((1, LANES), lambda b, h, i: (0, 0))],
        out_specs=pl.BlockSpec((tq, LANES), q_idx),
        out_shape=jax.ShapeDtypeStruct((r, d), BF16),
        scratch_shapes=[pltpu.VMEM((2, tq, ctx), F32),
                        pltpu.VMEM((2, seq // tk, tq, tk), F32)],
        compiler_params=_params(("arbitrary", "arbitrary", "arbitrary")),
        name="diff_attention",
    )(lam_full, q, k, v, k, v, subln_w.reshape(1, LANES))


def _proj_res_kernel(a_ref, w_ref, xs_ref, gate_ref, o_ref):
    o_ref[...] = xs_ref[...] + gate_ref[...] * _dot(a_ref[...], w_ref[...])


def _proj_res(a_bf16, w_bf16, xs, mods_l, dims):
    bsz, seq, ctx = dims
    r, d = xs.shape
    kdim = a_bf16.shape[1]
    tm = ROW_TILE
    lt = seq // tm
    return pl.pallas_call(
        _proj_res_kernel,
        grid=(r // tm,),
        in_specs=[pl.BlockSpec((tm, kdim), lambda i: (i, 0)),
                  pl.BlockSpec((kdim, d), lambda i: (0, 0)),
                  pl.BlockSpec((tm, d), lambda i: (i, 0)),
                  _mod_spec(2, lt, bsz, d)],
        out_specs=pl.BlockSpec((tm, d), lambda i: (i, 0)),
        out_shape=jax.ShapeDtypeStruct((r, d), F32),
        compiler_params=_params(("arbitrary",)),
        name="outproj_gate_residual",
    )(a_bf16, w_bf16, xs, mods_l)


def _diff_layer(xs, mods_l, g1, w_qkv, lam, subln_w, w_out, lambda_init, dims):
    q, k, v = _qkv_rope(xs, g1, mods_l, w_qkv, dims)
    lam32 = lam.astype(F32)
    lam_full = (jnp.exp(jnp.sum(lam32[0] * lam32[1])) - jnp.exp(jnp.sum(lam32[2] * lam32[3]))
                + lambda_init).reshape(1)
    o = _diff_attention(q, k, v, lam_full, subln_w, lambda_init, dims)
    return _proj_res(o, w_out.astype(BF16), xs, mods_l, dims)


def _mlstm_scan_kernel(q_ref, k_ref, v_ref, g_ref, bias_ref, h_ref, c_ref, n_ref, m_ref):
    d = pl.program_id(1)
    s = pl.program_id(2)

    @pl.when(s == 0)
    def _():
        c_ref[...] = jnp.zeros_like(c_ref)
        n_ref[...] = jnp.zeros_like(n_ref)
        m_ref[...] = jnp.zeros_like(m_ref)

    mask = _dir_masks(d)
    tri = jnp.where(mask, 1.0, 0.0).astype(F32)
    gates = g_ref[...] + bias_ref[...]
    ig = gates[:, 0:ML_HEADS]
    lf = _log_sigmoid(gates)
    bcum = jnp.dot(tri, lf, precision=HIGHEST, preferred_element_type=F32)
    btot = jnp.sum(lf, axis=0, keepdims=True)
    bcum_t = bcum.T
    gates_t = gates.T
    row_last = jnp.where(d == 0, CHUNK - 1, 0)
    rsel = lax.broadcasted_iota(jnp.int32, (CHUNK, 1), 0) == row_last
    for h in range(ML_HEADS):
        fcol = ML_HEADS + h
        m_prev = m_ref[h:h + 1, 0:1]
        bcol = bcum[:, fcol:fcol + 1]
        brow = bcum_t[fcol:fcol + 1, :]
        irow = gates_t[h:h + 1, :]
        icol = ig[:, h:h + 1]
        gcol = bcol + m_prev
        dmat = jnp.where(mask, bcol - brow + irow, -jnp.inf)
        mt = jnp.maximum(gcol, jnp.max(dmat, axis=1, keepdims=True))
        q32 = q_ref[:, h * ML_QK_DIM:(h + 1) * ML_QK_DIM] * (ML_QK_DIM ** -0.5)
        qh = q32.astype(BF16)
        kh32 = k_ref[:, h * ML_QK_DIM:(h + 1) * ML_QK_DIM]
        kh = kh32.astype(BF16)
        vh = v_ref[:, h * ML_V_DIM:(h + 1) * ML_V_DIM].astype(BF16)
        sm = _dot_nt(qh, kh) * jnp.exp(dmat - mt)
        inter = jnp.exp(gcol - mt)
        cst = c_ref[h]
        nst = n_ref[h:h + 1, :]
        num = _dot(sm.astype(BF16), vh) + inter * _dot(qh, cst.astype(BF16))
        qn = jnp.sum(q32 * nst, axis=1, keepdims=True)
        den = jnp.sum(sm, axis=1, keepdims=True) + inter * qn
        h_ref[:, h * ML_V_DIM:(h + 1) * ML_V_DIM] = num / jnp.maximum(jnp.abs(den), jnp.exp(-mt))
        m_new = jnp.sum(jnp.where(rsel, mt, 0.0), axis=0, keepdims=True)
        btot_h = btot[:, fcol:fcol + 1]
        wk = jnp.exp(btot_h - bcol + icol - m_new)
        cscale = jnp.exp(btot_h + m_prev - m_new)
        kw = kh32 * wk
        c_ref[h] = cscale * cst + _dot_tn(kw.astype(BF16), vh)
        n_ref[h:h + 1, :] = cscale * nst + jnp.sum(kw, axis=0, keepdims=True)
        m_ref[h:h + 1, :] = jnp.broadcast_to(m_new, (1, LANES))


def _mlstm_scan(t_main, gates_raw, bias_pad, dims):
    bsz, seq, ctx = dims
    r = t_main.shape[0]
    nsteps = (seq + ctx) // CHUNK
    blk = functools.partial(_chunk_block, bsz=bsz, seq=seq, ctx=ctx)
    return pl.pallas_call(
        _mlstm_scan_kernel,
        grid=(bsz, 2, nsteps),
        in_specs=[pl.BlockSpec((CHUNK, ML_QK), lambda b, d, s: (blk(b, d, s), 0)),
                  pl.BlockSpec((CHUNK, ML_QK), lambda b, d, s: (blk(b, d, s), 1)),
                  pl.BlockSpec((CHUNK, ML_V), lambda b, d, s: (blk(b, d, s), 1)),
                  pl.BlockSpec((CHUNK, LANES), lambda b, d, s: (blk(b, d, s), d)),
                  pl.BlockSpec((None, 1, LANES), lambda b, d, s: (d, 0, 0))],
        out_specs=pl.BlockSpec((None, CHUNK, ML_V), lambda b, d, s: (d, blk(b, d, s), 0)),
        out_shape=jax.ShapeDtypeStruct((2, r, ML_V), F32),
        scratch_shapes=[pltpu.VMEM((ML_HEADS, ML_QK_DIM, ML_V_DIM), F32),
                        pltpu.VMEM((8, ML_QK_DIM), F32),
                        pltpu.VMEM((8, LANES), F32)],
        compiler_params=_params(("arbitrary", "arbitrary", "arbitrary")),
        name="mlstm_scan",
    )(t_main, t_main, t_main, gates_raw, bias_pad)


def _mlstm_out_kernel(hf_ref, hb_ref, o_ref_in, nw_ref, w_ref, xs_ref, gate_ref, out_ref):
    u = None
    for h in range(ML_HEADS):
        cs = slice(h * ML_V_DIM, (h + 1) * ML_V_DIM)
        a = _sigmoid(o_ref_in[:, cs]) * (hf_ref[:, cs] + hb_ref[:, cs])
        a = (_rms(a) * nw_ref[...]).astype(BF16)
        part = _dot(a, w_ref[cs, :])
        u = part if u is None else u + part
    out_ref[...] = xs_ref[...] + gate_ref[...] * u


def _mlstm_out(h2, t_main, norm_w, w_out_bf16, xs, mods_l, dims):
    bsz, seq, ctx = dims
    r, d = xs.shape
    tm = ROW_TILE
    lt = seq // tm
    o_blk = (2 * ML_QK + ML_V) // ML_V
    return pl.pallas_call(
        _mlstm_out_kernel,
        grid=(r // tm,),
        in_specs=[pl.BlockSpec((None, tm, ML_V), lambda i: (0, i, 0)),
                  pl.BlockSpec((None, tm, ML_V), lambda i: (1, i, 0)),
                  pl.BlockSpec((tm, ML_V), lambda i: (i, o_blk)),
                  pl.BlockSpec((1, ML_V_DIM), lambda i: (0, 0)),
                  pl.BlockSpec((ML_V, d), lambda i: (0, 0)),
                  pl.BlockSpec((tm, d), lambda i: (i, 0)),
                  _mod_spec(2, lt, bsz, d)],
        out_specs=pl.BlockSpec((tm, d), lambda i: (i, 0)),
        out_shape=jax.ShapeDtypeStruct((r, d), F32),
        compiler_params=_params(("arbitrary",)),
        name="mlstm_norm_outproj",
    )(h2, h2, t_main, norm_w.reshape(1, ML_V_DIM), w_out_bf16, xs, mods_l)


def _mlstm_layer(xs, mods_l, g1, w_in, b_gates, norm_w, w_out, dims):
    d = xs.shape[1]
    main = 2 * ML_QK + 2 * ML_V
    w_main = w_in[:, :main].astype(BF16)
    wg = w_in[:, main:].reshape(d, 4, ML_HEADS)
    w_g = jnp.zeros((d, 2 * LANES), F32)
    bias = jnp.zeros((2, 1, LANES), F32)
    for dr in range(2):
        w_g = w_g.at[:, dr * LANES:dr * LANES + 2 * ML_HEADS].set(
            wg[:, 2 * dr:2 * dr + 2].reshape(d, 2 * ML_HEADS))
        bias = bias.at[dr, 0, :2 * ML_HEADS].set(b_gates.astype(F32)[2 * dr:2 * dr + 2].reshape(2 * ML_HEADS))
    t_main = _inproj(xs, g1, mods_l, w_main, dims)
    gates_raw = _inproj(xs, g1, mods_l, w_g.astype(BF16), dims)
    h2 = _mlstm_scan(t_main, gates_raw, bias, dims)
    return _mlstm_out(h2, t_main, norm_w, w_out.astype(BF16), xs, mods_l, dims)


def _router_kernel(x_ref, g_ref, sh_ref, sc_ref, wr_ref, br_ref, h_ref, idx_ref, gate_ref, rank_ref, cnt_ref,
                   carry_ref):
    i = pl.program_id(0)

    @pl.when(i == 0)
    def _():
        carry_ref[...] = jnp.zeros_like(carry_ref)

    h = _norm_mod(x_ref[...], g_ref[...], sh_ref[...], sc_ref[...])
    h_ref[...] = h.astype(h_ref.dtype)
    tm = h.shape[0]
    logits = lax.dot_general(wr_ref[...], h, (((1,), (1,)), ((), ())), precision=HIGHEST,
                             preferred_element_type=F32) + br_ref[...]
    eidx = lax.broadcasted_iota(jnp.int32, logits.shape, 0)
    work = logits
    vals, idxs = [], []
    picked = jnp.zeros(logits.shape, F32)
    for _ in range(TOP_K):
        mx = jnp.max(work, axis=0, keepdims=True)
        ix = jnp.min(jnp.where(work == mx, eidx, N_EXPERTS), axis=0, keepdims=True)
        sel = eidx == ix
        vals.append(mx)
        idxs.append(ix)
        picked = jnp.where(sel, 1.0, picked)
        work = jnp.where(sel, -jnp.inf, work)
    es = [jnp.exp(v - vals[0]) for v in vals]
    tot = es[0] + es[1] + es[2] + es[3]
    jj = lax.broadcasted_iota(jnp.int32, (tm, tm), 0)
    ii = lax.broadcasted_iota(jnp.int32, (tm, tm), 1)
    upper = jnp.where(jj <= ii, 1.0, 0.0).astype(BF16)
    incl = _dot(picked.astype(BF16), upper)
    carry = carry_ref[:, 0:1]
    excl = incl - picked + carry
    for k in range(TOP_K):
        idx_ref[k:k + 1, :] = idxs[k]
        gate_ref[k:k + 1, :] = es[k] / tot
        rk = jnp.sum(jnp.where(eidx == idxs[k], excl, 0.0), axis=0, keepdims=True)
        rank_ref[k:k + 1, :] = rk.astype(jnp.int32)
    new_carry = carry + jnp.sum(picked, axis=1, keepdims=True)
    carry_ref[...] = jnp.broadcast_to(new_carry, carry_ref.shape)
    cnt_ref[...] = jnp.broadcast_to(new_carry, cnt_ref.shape).astype(jnp.int32)


def _router(xs, g2, mods_l, w_router, b_router, dims):
    bsz, seq, ctx = dims
    r, d = xs.shape
    tm = ROW_TILE
    lt = seq // tm
    tok_spec = pl.BlockSpec((TOP_K, tm), lambda i: (0, i))
    return pl.pallas_call(
        _router_kernel,
        grid=(r // tm,),
        in_specs=[pl.BlockSpec((tm, d), lambda i: (i, 0)),
                  pl.BlockSpec((1, d), lambda i: (0, 0)),
                  _mod_spec(3, lt, bsz, d), _mod_spec(4, lt, bsz, d),
                  pl.BlockSpec((N_EXPERTS, d), lambda i: (0, 0)),
                  pl.BlockSpec((N_EXPERTS, 1), lambda i: (0, 0))],
        out_specs=[pl.BlockSpec((tm, d), lambda i: (i, 0)), tok_spec, tok_spec, tok_spec,
                   pl.BlockSpec((N_EXPERTS, LANES), lambda i: (0, 0))],
        out_shape=[jax.ShapeDtypeStruct((r, d), BF16),
                   jax.ShapeDtypeStruct((TOP_K, r), jnp.int32),
                   jax.ShapeDtypeStruct((TOP_K, r), F32),
                   jax.ShapeDtypeStruct((TOP_K, r), jnp.int32),
                   jax.ShapeDtypeStruct((N_EXPERTS, LANES), jnp.int32)],
        scratch_shapes=[pltpu.VMEM((N_EXPERTS, LANES), F32)],
        compiler_params=_params(("arbitrary",)),
        name="norm_mod_router_top4",
    )(xs, g2.reshape(1, d), mods_l, mods_l, w_router.T, b_router.reshape(N_EXPERTS, 1))


def _expert_kernel(be_ref, nb_ref, x_ref, wgu_ref, bgu_ref, wdn_ref, bdn_ref, y_ref, wgu_bf, wdn_bf):
    i = pl.program_id(0)
    active = i < nb_ref[0]
    new_expert = jnp.logical_or(i == 0, be_ref[i] != be_ref[jnp.maximum(i - 1, 0)])

    @pl.when(jnp.logical_and(active, new_expert))
    def _():
        wgu_bf[...] = wgu_ref[...].astype(BF16)
        wdn_bf[...] = wdn_ref[...].astype(BF16)

    @pl.when(active)
    def _():
        de = wdn_ref.shape[0]
        gu = _dot(x_ref[...], wgu_bf[...]) + bgu_ref[...]
        g = jnp.minimum(gu[:, :de], SWIGLU_LIMIT)
        u = jnp.clip(gu[:, de:], -SWIGLU_LIMIT, SWIGLU_LIMIT)
        a = (u + 1.0) * g * _sigmoid(SWIGLU_ALPHA * g)
        y_ref[...] = _dot(a.astype(BF16), wdn_bf[...]) + bdn_ref[...]


def _experts(xb, block_expert, nb_used, w_gu, b_gu, w_dn, b_dn):
    nrows, d = xb.shape
    bm = MOE_BLOCK
    ne, _, two_de = w_gu.shape
    de = two_de // 2
    grid_spec = pltpu.PrefetchScalarGridSpec(
        num_scalar_prefetch=2,
        grid=(nrows // bm,),
        in_specs=[pl.BlockSpec((bm, d), lambda i, be, nb: (i, 0)),
                  pl.BlockSpec((None, d, two_de), lambda i, be, nb: (be[i], 0, 0)),
                  pl.BlockSpec((None, 1, two_de), lambda i, be, nb: (be[i], 0, 0)),
                  pl.BlockSpec((None, de, d), lambda i, be, nb: (be[i], 0, 0)),
                  pl.BlockSpec((None, 1, d), lambda i, be, nb: (be[i], 0, 0))],
        out_specs=pl.BlockSpec((bm, d), lambda i, be, nb: (i, 0)),
        scratch_shapes=[pltpu.VMEM((d, two_de), BF16), pltpu.VMEM((de, d), BF16)],
    )
    return pl.pallas_call(
        _expert_kernel,
        grid_spec=grid_spec,
        out_shape=jax.ShapeDtypeStruct((nrows, d), F32),
        compiler_params=_params(("arbitrary",)),
        name="moe_expert_ffn",
    )(block_expert, nb_used, xb, w_gu, b_gu.reshape(ne, 1, two_de), w_dn, b_dn.reshape(ne, 1, d))


def _combine_kernel(xs_ref, yg_ref, gt_ref, gate_ref, fg_ref, o_ref, *, final):
    f = None
    for k in range(TOP_K):
        part = yg_ref[k] * gt_ref[:, k:k + 1]
        f = part if f is None else f + part
    out = xs_ref[...] + gate_ref[...] * f
    if final:
        out = _rms(out) * fg_ref[...]
    o_ref[...] = out


def _combine(xs, yg, gate_t, mods_l, final_g, final, dims):
    bsz, seq, ctx = dims
    r, d = xs.shape
    tm = ROW_TILE
    lt = seq // tm
    return pl.pallas_call(
        functools.partial(_combine_kernel, final=final),
        grid=(r // tm,),
        in_specs=[pl.BlockSpec((tm, d), lambda i: (i, 0)),
                  pl.BlockSpec((TOP_K, tm, d), lambda i: (0, i, 0)),
                  pl.BlockSpec((tm, TOP_K), lambda i: (i, 0)),
                  _mod_spec(5, lt, bsz, d),
                  pl.BlockSpec((1, d), lambda i: (0, 0))],
        out_specs=pl.BlockSpec((tm, d), lambda i: (i, 0)),
        out_shape=jax.ShapeDtypeStruct((r, d), F32),
        compiler_params=_params(("arbitrary",)),
        name="moe_combine_residual",
    )(xs, yg, gate_t, mods_l, final_g.reshape(1, d))


def _moe_layer(xs, mods_l, g2, w_router, b_router, w_gu, b_gu, w_dn, b_dn, final_g, final, dims):
    r, d = xs.shape
    bm = MOE_BLOCK
    h, top_idx, gate, rank, cnt = _router(xs, g2, mods_l, w_router, b_router, dims)
    counts = cnt[:, 0]
    padded = (counts + bm - 1) // bm * bm
    pad_ends = jnp.cumsum(padded)
    pad_starts = pad_ends - padded
    dest = pad_starts[top_idx] + rank
    n_blocks = -(-(r * TOP_K + N_EXPERTS * (bm - 1)) // bm)
    block_expert = jnp.minimum(jnp.searchsorted(pad_ends, jnp.arange(n_blocks, dtype=jnp.int32) * bm,
                                                side='right'), N_EXPERTS - 1).astype(jnp.int32)
    nb_used = (pad_ends[-1] // bm).astype(jnp.int32).reshape(1)
    tok = jnp.broadcast_to(jnp.arange(r, dtype=jnp.int32), (TOP_K, r))
    row_tok = jnp.zeros((n_blocks * bm,), jnp.int32).at[dest.reshape(-1)].set(tok.reshape(-1))
    xb = jnp.take(h, row_tok, axis=0)
    yb = _experts(xb, block_expert, nb_used, w_gu, b_gu, w_dn, b_dn)
    yg = jnp.take(yb, dest.reshape(-1), axis=0).reshape(TOP_K, r, d)
    return _combine(xs, yg, gate.T, mods_l, final_g, final, dims)


def kernel(x, c, ctx, c_ctx, ada_w, ada_b, norm1_g, norm2_g, ssd_w_in, ssd_conv_w, ssd_conv_b, ssd_dt_bias,
           ssd_a_log, ssd_d, ssd_norm_w, ssd_w_out, diff_w_qkv, diff_lam, diff_subln_w, diff_w_out, ml_w_in,
           ml_b_gates, ml_norm_w, ml_w_out, moe_w_router, moe_b_router, moe_w_gu, moe_b_gu, moe_w_dn, moe_b_dn,
           final_g):
    bsz, seq, d = x.shape
    n_ctx = ctx.shape[1]
    depth = ada_w.shape[0]
    dims = (bsz, seq, n_ctx)
    n_lat = bsz * seq
    xs = jnp.concatenate([x.reshape(n_lat, d), ctx.reshape(bsz * n_ctx, d)], axis=0)
    cond_rows = jnp.zeros((8, d), F32).at[:bsz].set(c).at[bsz].set(c_ctx)
    mods = _mods(cond_rows, ada_w, ada_b)
    mods = mods[:, :, :bsz + 1].reshape(depth, N_MOD, bsz + 1, 1, d)
    for i in range(depth):
        mods_l = mods[i]
        kind, j = i % 3, i // 3
        if kind == 0:
            xs = _ssd_layer(xs, mods_l, norm1_g[i], ssd_w_in[j], ssd_conv_w[j], ssd_conv_b[j], ssd_dt_bias[j],
                            ssd_a_log[j], ssd_d[j], ssd_norm_w[j], ssd_w_out[j], dims)
        elif kind == 1:
            lambda_init = 0.8 - 0.6 * math.exp(-0.3 * i)
            xs = _diff_layer(xs, mods_l, norm1_g[i], diff_w_qkv[j], diff_lam[j], diff_subln_w[j], diff_w_out[j],
                             lambda_init, dims)
        else:
            xs = _mlstm_layer(xs, mods_l, norm1_g[i], ml_w_in[j], ml_b_gates[j], ml_norm_w[j], ml_w_out[j], dims)
        xs = _moe_layer(xs, mods_l, norm2_g[i], moe_w_router[i], moe_b_router[i], moe_w_gu[i], moe_b_gu[i],
                        moe_w_dn[i], moe_b_dn[i], final_g, i == depth - 1, dims)
    return xs[:n_lat].reshape(bsz, seq, d)
```

```python
import functools
import math

import numpy as np
import jax
import jax.numpy as jnp
from jax import lax
from jax.experimental import pallas as pl
from jax.experimental.pallas import tpu as pltpu
from jax.experimental.pallas import tpu_sc as plsc

F32 = jnp.float32
BF16 = jnp.bfloat16
HIGHEST = lax.Precision.HIGHEST

GRID_W = 64
RMS_EPS = 1e-6
N_MOD = 6
SSD_HEAD_DIM = 64
SSD_HEADS = 32
SSD_GROUPS = 4
SSD_STATE = 128
SSD_INNER = SSD_HEADS * SSD_HEAD_DIM
SSD_BC = SSD_GROUPS * SSD_STATE
SSD_CONV_DIM = SSD_INNER + 2 * SSD_BC
DIFF_HEADS = 8
DIFF_HEAD_DIM = 64
DIFF_V_DIM = 128
ROPE_BASE = 10000.0
ROPE_Q = DIFF_HEAD_DIM // 4
ML_HEADS = 4
ML_QK_DIM = 128
ML_V_DIM = 256
ML_QK = ML_HEADS * ML_QK_DIM
ML_V = ML_HEADS * ML_V_DIM
N_EXPERTS = 32
TOP_K = 4
SWIGLU_LIMIT = 7.0
SWIGLU_ALPHA = 1.702

LANES = 128
CHUNK = 128
ROW_TILE = 256
MOE_BLOCK = 256
ATTN_Q_TILE = 256
ATTN_UNROLL = 4
SC_CORES = 2
SC_SUBCORES = 16
SC_GATHER_ROWS = 64
VMEM_LIMIT = 56 * 1024 * 1024


def _params(sem):
    return pltpu.CompilerParams(dimension_semantics=sem, vmem_limit_bytes=VMEM_LIMIT)


def _dot(a, b):
    return jnp.dot(a, b, preferred_element_type=F32)


def _dot_nt(a, b):
    return lax.dot_general(a, b, (((1,), (1,)), ((), ())), preferred_element_type=F32)


def _dot_tn(a, b):
    return lax.dot_general(a, b, (((0,), (0,)), ((), ())), preferred_element_type=F32)


def _rms(x):
    return x * lax.rsqrt(jnp.mean(x * x, axis=-1, keepdims=True) + RMS_EPS)


def _sigmoid(x):
    return 1.0 / (1.0 + jnp.exp(-x))


def _silu(x):
    return x * _sigmoid(x)


def _softplus(x):
    return jnp.maximum(x, 0.0) + jnp.log(1.0 + jnp.exp(-jnp.abs(x)))


def _log_sigmoid(x):
    return -_softplus(-x)


def _mod_spec(which, lat_tiles_per_batch, bsz, d):
    return pl.BlockSpec((None, None, 1, d),
                        lambda i: (which, jnp.minimum(i // lat_tiles_per_batch, bsz), 0, 0))


def _mods_kernel(c_ref, w_ref, b_ref, o_ref):
    c = c_ref[...]
    cond = _silu(c)
    o_ref[...] = jnp.dot(cond, w_ref[...], precision=HIGHEST, preferred_element_type=F32) + b_ref[...]


def _mods(cond_rows, ada_w, ada_b):
    depth, d, _ = ada_w.shape
    nr = cond_rows.shape[0]
    return pl.pallas_call(
        _mods_kernel,
        grid=(depth, N_MOD),
        in_specs=[pl.BlockSpec((nr, d), lambda l, j: (0, 0)),
                  pl.BlockSpec((None, d, d), lambda l, j: (l, 0, j)),
                  pl.BlockSpec((None, 1, d), lambda l, j: (l, 0, j))],
        out_specs=pl.BlockSpec((None, None, nr, d), lambda l, j: (l, j, 0, 0)),
        out_shape=jax.ShapeDtypeStruct((depth, N_MOD, nr, d), F32),
        compiler_params=_params(("arbitrary", "arbitrary")),
        name="adaln_mods",
    )(cond_rows, ada_w, ada_b.reshape(depth, 1, N_MOD * d))


def _norm_mod(x, g, sh, sc):
    return (_rms(x) * g) * (1.0 + sc) + sh


def _inproj_kernel(x_ref, g_ref, sh_ref, sc_ref, w_ref, o_ref, *, n_chunk):
    hb = _norm_mod(x_ref[...], g_ref[...], sh_ref[...], sc_ref[...]).astype(BF16)
    n = o_ref.shape[1]
    for n0 in range(0, n, n_chunk):
        o_ref[:, n0:n0 + n_chunk] = _dot(hb, w_ref[:, n0:n0 + n_chunk])


def _col_chunk(n):
    for c in (512, 384, 256, 128):
        if n % c == 0:
            return c
    raise ValueError(n)


def _inproj(xs, g, mods_l, w_bf16, dims):
    bsz, seq, ctx = dims
    r, d = xs.shape
    n = w_bf16.shape[1]
    tm = ROW_TILE
    lt = seq // tm
    return pl.pallas_call(
        functools.partial(_inproj_kernel, n_chunk=_col_chunk(n)),
        grid=(r // tm,),
        in_specs=[pl.BlockSpec((tm, d), lambda i: (i, 0)),
                  pl.BlockSpec((1, d), lambda i: (0, 0)),
                  _mod_spec(0, lt, bsz, d), _mod_spec(1, lt, bsz, d),
                  pl.BlockSpec((d, n), lambda i: (0, 0))],
        out_specs=pl.BlockSpec((tm, n), lambda i: (i, 0)),
        out_shape=jax.ShapeDtypeStruct((r, n), F32),
        compiler_params=_params(("arbitrary",)),
        name="norm_mod_inproj",
    )(xs, g.reshape(1, d), mods_l, mods_l, w_bf16)


def _conv_kernel(first_ref, last_ref, cur_ref, prev_ref, nxt_ref, w_ref, b_ref, o_ref):
    i = pl.program_id(0)
    cur = cur_ref[...]
    tm = cur.shape[0]
    row = lax.broadcasted_iota(jnp.int32, cur.shape, 0)
    prev_row = jnp.where(first_ref[i] == 1, 0.0, prev_ref[7:8, :])
    next_row = jnp.where(last_ref[i] == 1, 0.0, nxt_ref[0:1, :])
    down = jnp.where(row == 0, prev_row, pltpu.roll(cur, 1, 0))
    up = jnp.where(row == tm - 1, next_row, pltpu.roll(cur, tm - 1, 0))
    y = w_ref[0:1, :] * down + w_ref[1:2, :] * cur + w_ref[2:3, :] * up + b_ref[...]
    o_ref[...] = _silu(y)


def _ssd_conv(t_main, conv_w, conv_b, dims):
    bsz, seq, ctx = dims
    r = t_main.shape[0]
    tm = ROW_TILE
    cw = 1024
    ncol = SSD_CONV_DIM // cw
    off = SSD_INNER // cw
    starts = [b * seq for b in range(bsz)] + [bsz * seq + b * ctx for b in range(bsz)]
    ends = [s + (seq if k < bsz else ctx) for k, s in enumerate(starts)]
    first = np.array([1 if (i * tm) in starts else 0 for i in range(r // tm)], np.int32)
    last = np.array([1 if ((i + 1) * tm) in ends else 0 for i in range(r // tm)], np.int32)
    sub = tm // 8
    nblk8 = r // 8
    grid_spec = pltpu.PrefetchScalarGridSpec(
        num_scalar_prefetch=2,
        grid=(r // tm, ncol),
        in_specs=[pl.BlockSpec((tm, cw), lambda i, j, f, l: (i, off + j)),
                  pl.BlockSpec((8, cw), lambda i, j, f, l: (jnp.maximum(i * sub - 1, 0), off + j)),
                  pl.BlockSpec((8, cw), lambda i, j, f, l: (jnp.minimum((i + 1) * sub, nblk8 - 1), off + j)),
                  pl.BlockSpec((3, cw), lambda i, j, f, l: (0, j)),
                  pl.BlockSpec((1, cw), lambda i, j, f, l: (0, j))],
        out_specs=pl.BlockSpec((tm, cw), lambda i, j, f, l: (i, j)),
    )
    return pl.pallas_call(
        _conv_kernel,
        grid_spec=grid_spec,
        out_shape=jax.ShapeDtypeStruct((r, SSD_CONV_DIM), F32),
        compiler_params=_params(("arbitrary", "arbitrary")),
        name="ssd_conv_silu",
    )(jnp.asarray(first), jnp.asarray(last), t_main, t_main, t_main, conv_w, conv_b.reshape(1, -1))


def _chunk_block(b, d, s, bsz, seq, ctx):
    nc_ctx = ctx // CHUNK
    nc_lat = seq // CHUNK
    in_ctx = s < nc_ctx
    pc = jnp.where(d == 0, s, nc_ctx - 1 - s)
    pls = s - nc_ctx
    plat = jnp.where(d == 0, pls, nc_lat - 1 - pls)
    ctx_blk = bsz * nc_lat + b * nc_ctx + pc
    lat_blk = b * nc_lat + plat
    return jnp.where(in_ctx, ctx_blk, lat_blk)


def _dir_masks(d):
    ii = lax.broadcasted_iota(jnp.int32, (CHUNK, CHUNK), 0)
    jj = lax.broadcasted_iota(jnp.int32, (CHUNK, CHUNK), 1)
    sign = 1 - 2 * d
    return (ii - jj) * sign >= 0


def _ssd_scan_kernel(x_ref, b_ref, c_ref, dt_ref, bias_ref, a_ref, y_ref, st_ref):
    d = pl.program_id(1)
    s = pl.program_id(2)

    @pl.when(s == 0)
    def _():
        st_ref[...] = jnp.zeros_like(st_ref)

    mask = _dir_masks(d)
    tri = jnp.where(mask, 1.0, 0.0).astype(F32)
    dt = _softplus(dt_ref[...] + bias_ref[...])
    dta = dt * a_ref[...]
    la = jnp.dot(tri, dta, precision=HIGHEST, preferred_element_type=F32)
    total = jnp.sum(dta, axis=0, keepdims=True)
    la_t = la.T
    dt_t = dt.T
    e_la = jnp.exp(la)
    ws = jnp.exp(total - la) * dt
    e_tot = jnp.exp(total)
    lane = lax.broadcasted_iota(jnp.int32, (CHUNK, LANES), 1)
    lo_half = lane < SSD_HEAD_DIM
    hg = SSD_HEADS // SSD_GROUPS
    for g in range(SSD_GROUPS):
        cg = c_ref[:, g * SSD_STATE:(g + 1) * SSD_STATE].astype(BF16)
        bg = b_ref[:, g * SSD_STATE:(g + 1) * SSD_STATE].astype(BF16)
        cb = _dot_nt(cg, bg)
        for pr in range(hg // 2):
            h0 = g * hg + 2 * pr
            col0 = slice((h0 * SSD_HEAD_DIM), (h0 + 2) * SSD_HEAD_DIM)
            xp = x_ref[:, col0]
            xpb = xp.astype(BF16)
            ys = []
            for h in (h0, h0 + 1):
                diff = la[:, h:h + 1] - la_t[h:h + 1, :]
                w = jnp.exp(jnp.where(mask, diff, -jnp.inf)) * cb * dt_t[h:h + 1, :]
                ys.append(_dot(w.astype(BF16), xpb))
            y_intra = jnp.where(lo_half, ys[0], ys[1])
            st = st_ref[g, :, pr * LANES:(pr + 1) * LANES]
            e_pair = jnp.where(lo_half, e_la[:, h0:h0 + 1], e_la[:, h0 + 1:h0 + 2])
            y_ref[:, col0] = y_intra + e_pair * _dot(cg, st.astype(BF16))
            ws_pair = jnp.where(lo_half, ws[:, h0:h0 + 1], ws[:, h0 + 1:h0 + 2])
            et_pair = jnp.where(lo_half[0:1], e_tot[:, h0:h0 + 1], e_tot[:, h0 + 1:h0 + 2])
            st_ref[g, :, pr * LANES:(pr + 1) * LANES] = et_pair * st + _dot_tn(bg, (ws_pair * xp).astype(BF16))


def _ssd_scan(xbc, dt_raw, dt_bias_pad, a_pad, dims):
    bsz, seq, ctx = dims
    r = xbc.shape[0]
    nsteps = (seq + ctx) // CHUNK
    blk = functools.partial(_chunk_block, bsz=bsz, seq=seq, ctx=ctx)
    xw = SSD_INNER
    return pl.pallas_call(
        _ssd_scan_kernel,
        grid=(bsz, 2, nsteps),
        in_specs=[pl.BlockSpec((CHUNK, xw), lambda b, d, s: (blk(b, d, s), 0)),
                  pl.BlockSpec((CHUNK, SSD_BC), lambda b, d, s: (blk(b, d, s), xw // SSD_BC)),
                  pl.BlockSpec((CHUNK, SSD_BC), lambda b, d, s: (blk(b, d, s), xw // SSD_BC + 1)),
                  pl.BlockSpec((CHUNK, LANES), lambda b, d, s: (blk(b, d, s), d)),
                  pl.BlockSpec((None, 1, LANES), lambda b, d, s: (d, 0, 0)),
                  pl.BlockSpec((None, 1, LANES), lambda b, d, s: (d, 0, 0))],
        out_specs=pl.BlockSpec((None, CHUNK, xw), lambda b, d, s: (d, blk(b, d, s), 0)),
        out_shape=jax.ShapeDtypeStruct((2, r, xw), F32),
        scratch_shapes=[pltpu.VMEM((SSD_GROUPS, SSD_STATE, xw // SSD_GROUPS), F32)],
        compiler_params=_params(("arbitrary", "arbitrary", "arbitrary")),
        name="ssd_scan",
    )(xbc, xbc, xbc, dt_raw, dt_bias_pad, a_pad)


def _ssd_out_kernel(yf_ref, yb_ref, xc_ref, z_ref, dexp_ref, nw_ref, w_ref, xs_ref, gate_ref, o_ref):
    y = yf_ref[...] + yb_ref[...] + dexp_ref[...] * xc_ref[...]
    y = y * _silu(z_ref[...])
    a = (_rms(y) * nw_ref[...]).astype(BF16)
    o_ref[...] = xs_ref[...] + gate_ref[...] * _dot(a, w_ref[...])


def _ssd_out(y2, xbc, t_main, d_exp, norm_w, w_out_bf16, xs, mods_l, dims):
    bsz, seq, ctx = dims
    r, d = xs.shape
    tm = ROW_TILE
    lt = seq // tm
    xw = SSD_INNER
    return pl.pallas_call(
        _ssd_out_kernel,
        grid=(r // tm,),
        in_specs=[pl.BlockSpec((None, tm, xw), lambda i: (0, i, 0)),
                  pl.BlockSpec((None, tm, xw), lambda i: (1, i, 0)),
                  pl.BlockSpec((tm, xw), lambda i: (i, 0)),
                  pl.BlockSpec((tm, xw), lambda i: (i, 0)),
                  pl.BlockSpec((1, xw), lambda i: (0, 0)),
                  pl.BlockSpec((1, xw), lambda i: (0, 0)),
                  pl.BlockSpec((xw, d), lambda i: (0, 0)),
                  pl.BlockSpec((tm, d), lambda i: (i, 0)),
                  _mod_spec(2, lt, bsz, d)],
        out_specs=pl.BlockSpec((tm, d), lambda i: (i, 0)),
        out_shape=jax.ShapeDtypeStruct((r, d), F32),
        compiler_params=_params(("arbitrary",)),
        name="ssd_gated_norm_outproj",
    )(y2, y2, xbc, t_main, d_exp, norm_w.reshape(1, xw), w_out_bf16, xs, mods_l)


def _ssd_layer(xs, mods_l, g1, w_in, conv_w, conv_b, dt_bias, a_log, d_skip, norm_w, w_out, dims):
    d = xs.shape[1]
    main = SSD_INNER + SSD_CONV_DIM
    w_main = w_in[:, :main].astype(BF16)
    w_dt = jnp.zeros((d, 2 * LANES), F32)
    w_dt = w_dt.at[:, :SSD_HEADS].set(w_in[:, main:main + SSD_HEADS])
    w_dt = w_dt.at[:, LANES:LANES + SSD_HEADS].set(w_in[:, main + SSD_HEADS:]).astype(BF16)
    t_main = _inproj(xs, g1, mods_l, w_main, dims)
    dt_raw = _inproj(xs, g1, mods_l, w_dt, dims)
    xbc = _ssd_conv(t_main, conv_w, conv_b, dims)
    pad = jnp.zeros((2, 1, LANES - SSD_HEADS), F32)
    bias_pad = jnp.concatenate([dt_bias.astype(F32).reshape(2, 1, SSD_HEADS), pad], axis=-1)
    a_pad = jnp.concatenate([-jnp.exp(a_log.astype(F32)).reshape(2, 1, SSD_HEADS), pad], axis=-1)
    y2 = _ssd_scan(xbc, dt_raw, bias_pad, a_pad, dims)
    d_exp = jnp.repeat(d_skip.astype(F32), SSD_HEAD_DIM).reshape(1, SSD_INNER)
    return _ssd_out(y2, xbc, t_main, d_exp, norm_w, w_out.astype(BF16), xs, mods_l, dims)


def _qkv_rope_kernel(x_ref, g_ref, sh_ref, sc_ref, w_ref, cos_ref, sin_ref, q_ref, k_ref, v_ref, *, n_lat_tiles):
    i = pl.program_id(0)
    hb = _norm_mod(x_ref[...], g_ref[...], sh_ref[...], sc_ref[...]).astype(BF16)
    d = x_ref.shape[1]
    is_ctx = i >= n_lat_tiles
    cos = jnp.where(is_ctx, 1.0, cos_ref[...])
    sin = jnp.where(is_ctx, 0.0, sin_ref[...])
    nrep = d // LANES
    cos = jnp.concatenate([cos] * nrep, axis=1)
    sin = jnp.concatenate([sin] * nrep, axis=1)

    def mm(c):
        return _dot(hb, w_ref[:, c * d:(c + 1) * d])

    q = mm(0) * cos + mm(3) * sin
    q_ref[...] = (q * (DIFF_HEAD_DIM ** -0.5)).astype(BF16)
    k_ref[...] = (mm(1) * cos + mm(4) * sin).astype(BF16)
    v_ref[...] = mm(2).astype(BF16)


def _rope_tables(seq):
    rows = seq // GRID_W
    row = jnp.repeat(jnp.arange(rows, dtype=F32), GRID_W)
    col = jnp.tile(jnp.arange(GRID_W, dtype=F32), rows)
    inv = ROPE_BASE ** (-jnp.arange(ROPE_Q, dtype=F32) / ROPE_Q)
    ang_r = row[:, None] * inv
    ang_c = col[:, None] * inv
    cos = jnp.concatenate([jnp.cos(ang_r), jnp.cos(ang_r), jnp.cos(ang_c), jnp.cos(ang_c)], axis=1)
    sin = jnp.concatenate([-jnp.sin(ang_r), jnp.sin(ang_r), -jnp.sin(ang_c), jnp.sin(ang_c)], axis=1)
    return jnp.tile(cos, (1, 2)), jnp.tile(sin, (1, 2))


def _rope_partner_cols(d):
    col = np.arange(d)
    within = col % (2 * ROPE_Q)
    return np.where(within < ROPE_Q, col + ROPE_Q, col - ROPE_Q)


def _qkv_rope(xs, g, mods_l, w_qkv, dims):
    bsz, seq, ctx = dims
    r, d = xs.shape
    tm = ROW_TILE
    lt = seq // tm
    perm = _rope_partner_cols(d)
    wq, wk, wv = w_qkv[:, :d], w_qkv[:, d:2 * d], w_qkv[:, 2 * d:]
    w_all = jnp.concatenate([wq, wk, wv, wq[:, perm], wk[:, perm]], axis=1).astype(BF16)
    cos, sin = _rope_tables(seq)
    out = jax.ShapeDtypeStruct((r, d), BF16)
    row_spec = pl.BlockSpec((tm, d), lambda i: (i, 0))
    tab_spec = pl.BlockSpec((tm, LANES), lambda i: (i % lt, 0))
    return pl.pallas_call(
        functools.partial(_qkv_rope_kernel, n_lat_tiles=bsz * lt),
        grid=(r // tm,),
        in_specs=[row_spec, pl.BlockSpec((1, d), lambda i: (0, 0)),
                  _mod_spec(0, lt, bsz, d), _mod_spec(1, lt, bsz, d),
                  pl.BlockSpec((d, 5 * d), lambda i: (0, 0)), tab_spec, tab_spec],
        out_specs=[row_spec, row_spec, row_spec],
        out_shape=[out, out, out],
        compiler_params=_params(("arbitrary",)),
        name="norm_mod_qkv_rope",
    )(xs, g.reshape(1, d), mods_l, mods_l, w_all, cos, sin)


def _lane_fold(x, op):
    parts = [x[:, t * LANES:(t + 1) * LANES] for t in range(x.shape[1] // LANES)]
    return functools.reduce(op, parts)


def _attn_kernel(lam_ref, q_ref, kc_ref, vc_ref, kx_ref, vx_ref, nw_ref, o_ref,
                 sc_ref, sx_ref, acc_ref, m_ref, l_ref, *, n_lat_q, tk, unroll, lambda_init):
    i = pl.program_id(2)
    q = q_ref[...]
    tq = q.shape[0]
    lane = lax.broadcasted_iota(jnp.int32, q.shape, 1)
    zero = jnp.zeros_like(q)
    qq = jnp.concatenate([jnp.where(lane < DIFF_HEAD_DIM, q, zero),
                          jnp.where(lane >= DIFF_HEAD_DIM, q, zero)], axis=0)
    is_latent = i < n_lat_q
    n_x = kx_ref.shape[0] // tk

    s = _dot_nt(qq, kc_ref[...])
    sc_ref[...] = s
    m_ref[...] = _lane_fold(s, jnp.maximum)

    @pl.when(is_latent)
    def _():
        def p1(j, m_):
            kj = kx_ref[pl.ds(pl.multiple_of(j * tk, tk), tk), :]
            sj = _dot_nt(qq, kj)
            sx_ref[j] = sj
            return jnp.maximum(m_, _lane_fold(sj, jnp.maximum))

        m_ref[...] = lax.fori_loop(0, n_x, p1, m_ref[...], unroll=unroll)

    mrow = jnp.max(m_ref[...], axis=1, keepdims=True)

    p = jnp.exp(sc_ref[...] - mrow)
    acc_ref[...] = _dot(p.astype(BF16), vc_ref[...])
    l_ref[...] = _lane_fold(p, jnp.add)

    @pl.when(is_latent)
    def _():
        def p2(j, l_):
            vj = vx_ref[pl.ds(pl.multiple_of(j * tk, tk), tk), :]
            pj = jnp.exp(sx_ref[j] - mrow)
            acc_ref[...] += _dot(pj.astype(BF16), vj)
            return l_ + _lane_fold(pj, jnp.add)

        l_ref[...] = lax.fori_loop(0, n_x, p2, l_ref[...], unroll=unroll)

    on = acc_ref[...] / jnp.sum(l_ref[...], axis=1, keepdims=True)
    o = on[:tq] - lam_ref[0] * on[tq:]
    o = _rms(o) * nw_ref[...] * (1.0 - lambda_init)
    o_ref[...] = o.astype(o_ref.dtype)


def _diff_attention(q, k, v, lam_full, subln_w, lambda_init, dims):
    bsz, seq, ctx = dims
    r, d = q.shape
    tq = ATTN_Q_TILE
    tk = min(512, seq)
    n_lat_q = seq // tq
    n_ctx_q = ctx // tq
    nq = n_lat_q + n_ctx_q

    def q_idx(b, h, i):
        return (jnp.where(i < n_lat_q, b * n_lat_q + i, bsz * n_lat_q + b * n_ctx_q + (i - n_lat_q)), h)

    ctx_spec = pl.BlockSpec((ctx, LANES), lambda b, h, i: (bsz * seq // ctx + b, h))
    lat_spec = pl.BlockSpec((seq, LANES), lambda b, h, i: (b, h))
    return pl.pallas_call(
        functools.partial(_attn_kernel, n_lat_q=n_lat_q, tk=tk, unroll=min(ATTN_UNROLL, seq // tk),
                          lambda_init=lambda_init),
        grid=(bsz, DIFF_HEADS, nq),
        in_specs=[pl.BlockSpec(memory_space=pltpu.SMEM),
                  pl.BlockSpec((tq, LANES), q_idx),
                  ctx_spec, ctx_spec, lat_spec, lat_spec,
                  pl.BlockSpec((1, LANES), lambda b, h, i: (0, 0))],
        out_specs=pl.BlockSpec((tq, LANES), q_idx),
        out_shape=jax.ShapeDtypeStruct((r, d), BF16),
        scratch_shapes=[pltpu.VMEM((2 * tq, ctx), F32),
                        pltpu.VMEM((seq // tk, 2 * tq, tk), F32),
                        pltpu.VMEM((2 * tq, LANES), F32),
                        pltpu.VMEM((2 * tq, LANES), F32),
                        pltpu.VMEM((2 * tq, LANES), F32)],
        compiler_params=_params(("arbitrary", "arbitrary", "arbitrary")),
        name="diff_attention",
    )(lam_full, q, k, v, k, v, subln_w.reshape(1, LANES))


def _proj_res_kernel(a_ref, w_ref, xs_ref, gate_ref, o_ref):
    o_ref[...] = xs_ref[...] + gate_ref[...] * _dot(a_ref[...], w_ref[...])


def _proj_res(a_bf16, w_bf16, xs, mods_l, dims):
    bsz, seq, ctx = dims
    r, d = xs.shape
    kdim = a_bf16.shape[1]
    tm = ROW_TILE
    lt = seq // tm
    return pl.pallas_call(
        _proj_res_kernel,
        grid=(r // tm,),
        in_specs=[pl.BlockSpec((tm, kdim), lambda i: (i, 0)),
                  pl.BlockSpec((kdim, d), lambda i: (0, 0)),
                  pl.BlockSpec((tm, d), lambda i: (i, 0)),
                  _mod_spec(2, lt, bsz, d)],
        out_specs=pl.BlockSpec((tm, d), lambda i: (i, 0)),
        out_shape=jax.ShapeDtypeStruct((r, d), F32),
        compiler_params=_params(("arbitrary",)),
        name="outproj_gate_residual",
    )(a_bf16, w_bf16, xs, mods_l)


def _diff_layer(xs, mods_l, g1, w_qkv, lam, subln_w, w_out, lambda_init, dims):
    q, k, v = _qkv_rope(xs, g1, mods_l, w_qkv, dims)
    lam32 = lam.astype(F32)
    lam_full = (jnp.exp(jnp.sum(lam32[0] * lam32[1])) - jnp.exp(jnp.sum(lam32[2] * lam32[3]))
                + lambda_init).reshape(1)
    o = _diff_attention(q, k, v, lam_full, subln_w, lambda_init, dims)
    return _proj_res(o, w_out.astype(BF16), xs, mods_l, dims)


def _mlstm_scan_kernel(q_ref, k_ref, v_ref, g_ref, bias_ref, h_ref, c_ref, n_ref, m_ref):
    d = pl.program_id(1)
    s = pl.program_id(2)

    @pl.when(s == 0)
    def _():
        c_ref[...] = jnp.zeros_like(c_ref)
        n_ref[...] = jnp.zeros_like(n_ref)
        m_ref[...] = jnp.zeros_like(m_ref)

    mask = _dir_masks(d)
    tri = jnp.where(mask, 1.0, 0.0).astype(F32)
    gates = g_ref[...] + bias_ref[...]
    ig = gates[:, 0:ML_HEADS]
    lf = _log_sigmoid(gates)
    bcum = jnp.dot(tri, lf, precision=HIGHEST, preferred_element_type=F32)
    btot = jnp.sum(lf, axis=0, keepdims=True)
    bcum_t = bcum.T
    gates_t = gates.T
    row_last = jnp.where(d == 0, CHUNK - 1, 0)
    rsel = lax.broadcasted_iota(jnp.int32, (CHUNK, 1), 0) == row_last
    for h in range(ML_HEADS):
        fcol = ML_HEADS + h
        m_prev = m_ref[h:h + 1, 0:1]
        bcol = bcum[:, fcol:fcol + 1]
        brow = bcum_t[fcol:fcol + 1, :]
        irow = gates_t[h:h + 1, :]
        icol = ig[:, h:h + 1]
        gcol = bcol + m_prev
        dmat = jnp.where(mask, bcol - brow + irow, -jnp.inf)
        mt = jnp.maximum(gcol, jnp.max(dmat, axis=1, keepdims=True))
        q32 = q_ref[:, h * ML_QK_DIM:(h + 1) * ML_QK_DIM] * (ML_QK_DIM ** -0.5)
        qh = q32.astype(BF16)
        kh32 = k_ref[:, h * ML_QK_DIM:(h + 1) * ML_QK_DIM]
        kh = kh32.astype(BF16)
        vh = v_ref[:, h * ML_V_DIM:(h + 1) * ML_V_DIM].astype(BF16)
        sm = _dot_nt(qh, kh) * jnp.exp(dmat - mt)
        inter = jnp.exp(gcol - mt)
        cst = c_ref[h]
        nst = n_ref[h:h + 1, :]
        num = _dot(sm.astype(BF16), vh) + inter * _dot(qh, cst.astype(BF16))
        qn = jnp.sum(q32 * nst, axis=1, keepdims=True)
        den = jnp.sum(sm, axis=1, keepdims=True) + inter * qn
        h_ref[:, h * ML_V_DIM:(h + 1) * ML_V_DIM] = num / jnp.maximum(jnp.abs(den), jnp.exp(-mt))
        m_new = jnp.sum(jnp.where(rsel, mt, 0.0), axis=0, keepdims=True)
        btot_h = btot[:, fcol:fcol + 1]
        wk = jnp.exp(btot_h - bcol + icol - m_new)
        cscale = jnp.exp(btot_h + m_prev - m_new)
        kw = kh32 * wk
        c_ref[h] = cscale * cst + _dot_tn(kw.astype(BF16), vh)
        n_ref[h:h + 1, :] = cscale * nst + jnp.sum(kw, axis=0, keepdims=True)
        m_ref[h:h + 1, :] = jnp.broadcast_to(m_new, (1, LANES))


def _mlstm_scan(t_main, gates_raw, bias_pad, dims):
    bsz, seq, ctx = dims
    r = t_main.shape[0]
    nsteps = (seq + ctx) // CHUNK
    blk = functools.partial(_chunk_block, bsz=bsz, seq=seq, ctx=ctx)
    return pl.pallas_call(
        _mlstm_scan_kernel,
        grid=(bsz, 2, nsteps),
        in_specs=[pl.BlockSpec((CHUNK, ML_QK), lambda b, d, s: (blk(b, d, s), 0)),
                  pl.BlockSpec((CHUNK, ML_QK), lambda b, d, s: (blk(b, d, s), 1)),
                  pl.BlockSpec((CHUNK, ML_V), lambda b, d, s: (blk(b, d, s), 1)),
                  pl.BlockSpec((CHUNK, LANES), lambda b, d, s: (blk(b, d, s), d)),
                  pl.BlockSpec((None, 1, LANES), lambda b, d, s: (d, 0, 0))],
        out_specs=pl.BlockSpec((None, CHUNK, ML_V), lambda b, d, s: (d, blk(b, d, s), 0)),
        out_shape=jax.ShapeDtypeStruct((2, r, ML_V), F32),
        scratch_shapes=[pltpu.VMEM((ML_HEADS, ML_QK_DIM, ML_V_DIM), F32),
                        pltpu.VMEM((8, ML_QK_DIM), F32),
                        pltpu.VMEM((8, LANES), F32)],
        compiler_params=_params(("arbitrary", "arbitrary", "arbitrary")),
        name="mlstm_scan",
    )(t_main, t_main, t_main, gates_raw, bias_pad)


def _mlstm_out_kernel(hf_ref, hb_ref, o_ref_in, nw_ref, w_ref, xs_ref, gate_ref, out_ref):
    u = None
    for h in range(ML_HEADS):
        cs = slice(h * ML_V_DIM, (h + 1) * ML_V_DIM)
        a = _sigmoid(o_ref_in[:, cs]) * (hf_ref[:, cs] + hb_ref[:, cs])
        a = (_rms(a) * nw_ref[...]).astype(BF16)
        part = _dot(a, w_ref[cs, :])
        u = part if u is None else u + part
    out_ref[...] = xs_ref[...] + gate_ref[...] * u


def _mlstm_out(h2, t_main, norm_w, w_out_bf16, xs, mods_l, dims):
    bsz, seq, ctx = dims
    r, d = xs.shape
    tm = ROW_TILE
    lt = seq // tm
    o_blk = (2 * ML_QK + ML_V) // ML_V
    return pl.pallas_call(
        _mlstm_out_kernel,
        grid=(r // tm,),
        in_specs=[pl.BlockSpec((None, tm, ML_V), lambda i: (0, i, 0)),
                  pl.BlockSpec((None, tm, ML_V), lambda i: (1, i, 0)),
                  pl.BlockSpec((tm, ML_V), lambda i: (i, o_blk)),
                  pl.BlockSpec((1, ML_V_DIM), lambda i: (0, 0)),
                  pl.BlockSpec((ML_V, d), lambda i: (0, 0)),
                  pl.BlockSpec((tm, d), lambda i: (i, 0)),
                  _mod_spec(2, lt, bsz, d)],
        out_specs=pl.BlockSpec((tm, d), lambda i: (i, 0)),
        out_shape=jax.ShapeDtypeStruct((r, d), F32),
        compiler_params=_params(("arbitrary",)),
        name="mlstm_norm_outproj",
    )(h2, h2, t_main, norm_w.reshape(1, ML_V_DIM), w_out_bf16, xs, mods_l)


def _mlstm_layer(xs, mods_l, g1, w_in, b_gates, norm_w, w_out, dims):
    d = xs.shape[1]
    main = 2 * ML_QK + 2 * ML_V
    w_main = w_in[:, :main].astype(BF16)
    wg = w_in[:, main:].reshape(d, 4, ML_HEADS)
    w_g = jnp.zeros((d, 2 * LANES), F32)
    bias = jnp.zeros((2, 1, LANES), F32)
    for dr in range(2):
        w_g = w_g.at[:, dr * LANES:dr * LANES + 2 * ML_HEADS].set(
            wg[:, 2 * dr:2 * dr + 2].reshape(d, 2 * ML_HEADS))
        bias = bias.at[dr, 0, :2 * ML_HEADS].set(b_gates.astype(F32)[2 * dr:2 * dr + 2].reshape(2 * ML_HEADS))
    t_main = _inproj(xs, g1, mods_l, w_main, dims)
    gates_raw = _inproj(xs, g1, mods_l, w_g.astype(BF16), dims)
    h2 = _mlstm_scan(t_main, gates_raw, bias, dims)
    return _mlstm_out(h2, t_main, norm_w, w_out.astype(BF16), xs, mods_l, dims)


def _router_kernel(x_ref, g_ref, sh_ref, sc_ref, wr_ref, br_ref, h_ref, idx_ref, gate_ref, rank_ref, cnt_ref,
                   carry_ref):
    i = pl.program_id(0)

    @pl.when(i == 0)
    def _():
        carry_ref[...] = jnp.zeros_like(carry_ref)

    h = _norm_mod(x_ref[...], g_ref[...], sh_ref[...], sc_ref[...])
    tm, d = h.shape
    hr = h.astype(BF16).astype(F32)
    hi = lax.bitcast_convert_type(hr[:, :d // 2], jnp.int32)
    lo = lax.bitcast_convert_type(hr[:, d // 2:], jnp.int32)
    h_ref[...] = hi | lax.shift_right_logical(lo, 16)
    logits = lax.dot_general(wr_ref[...], h, (((1,), (1,)), ((), ())), precision=HIGHEST,
                             preferred_element_type=F32) + br_ref[...]
    eidx = lax.broadcasted_iota(jnp.int32, logits.shape, 0)
    work = logits
    vals, idxs = [], []
    picked = jnp.zeros(logits.shape, F32)
    for _ in range(TOP_K):
        mx = jnp.max(work, axis=0, keepdims=True)
        ix = jnp.min(jnp.where(work == mx, eidx, N_EXPERTS), axis=0, keepdims=True)
        sel = eidx == ix
        vals.append(mx)
        idxs.append(ix)
        picked = jnp.where(sel, 1.0, picked)
        work = jnp.where(sel, -jnp.inf, work)
    es = [jnp.exp(v - vals[0]) for v in vals]
    tot = es[0] + es[1] + es[2] + es[3]
    jj = lax.broadcasted_iota(jnp.int32, (tm, tm), 0)
    ii = lax.broadcasted_iota(jnp.int32, (tm, tm), 1)
    upper = jnp.where(jj <= ii, 1.0, 0.0).astype(BF16)
    incl = _dot(picked.astype(BF16), upper)
    carry = carry_ref[:, 0:1]
    excl = incl - picked + carry
    for k in range(TOP_K):
        idx_ref[k:k + 1, :] = idxs[k]
        gate_ref[k:k + 1, :] = es[k] / tot
        rk = jnp.sum(jnp.where(eidx == idxs[k], excl, 0.0), axis=0, keepdims=True)
        rank_ref[k:k + 1, :] = rk.astype(jnp.int32)
    new_carry = carry + jnp.sum(picked, axis=1, keepdims=True)
    carry_ref[...] = jnp.broadcast_to(new_carry, carry_ref.shape)
    cnt_ref[...] = jnp.broadcast_to(new_carry, cnt_ref.shape).astype(jnp.int32)


def _router(xs, g2, mods_l, w_router, b_router, dims):
    bsz, seq, ctx = dims
    r, d = xs.shape
    tm = ROW_TILE
    lt = seq // tm
    tok_spec = pl.BlockSpec((TOP_K, tm), lambda i: (0, i))
    return pl.pallas_call(
        _router_kernel,
        grid=(r // tm,),
        in_specs=[pl.BlockSpec((tm, d), lambda i: (i, 0)),
                  pl.BlockSpec((1, d), lambda i: (0, 0)),
                  _mod_spec(3, lt, bsz, d), _mod_spec(4, lt, bsz, d),
                  pl.BlockSpec((N_EXPERTS, d), lambda i: (0, 0)),
                  pl.BlockSpec((N_EXPERTS, 1), lambda i: (0, 0))],
        out_specs=[pl.BlockSpec((tm, d // 2), lambda i: (i, 0)), tok_spec, tok_spec, tok_spec,
                   pl.BlockSpec((N_EXPERTS, LANES), lambda i: (0, 0))],
        out_shape=[jax.ShapeDtypeStruct((r, d // 2), jnp.int32),
                   jax.ShapeDtypeStruct((TOP_K, r), jnp.int32),
                   jax.ShapeDtypeStruct((TOP_K, r), F32),
                   jax.ShapeDtypeStruct((TOP_K, r), jnp.int32),
                   jax.ShapeDtypeStruct((N_EXPERTS, LANES), jnp.int32)],
        scratch_shapes=[pltpu.VMEM((N_EXPERTS, LANES), F32)],
        compiler_params=_params(("arbitrary",)),
        name="norm_mod_router_top4",
    )(xs, g2.reshape(1, d), mods_l, mods_l, w_router.T, b_router.reshape(N_EXPERTS, 1))


def _expert_kernel(be_ref, nb_ref, x_ref, wgu_ref, bgu_ref, wdn_ref, bdn_ref, y_ref, wgu_bf, wdn_bf):
    i = pl.program_id(0)
    active = i < nb_ref[0]
    new_expert = jnp.logical_or(i == 0, be_ref[i] != be_ref[jnp.maximum(i - 1, 0)])

    @pl.when(jnp.logical_and(active, new_expert))
    def _():
        wgu_bf[...] = wgu_ref[...].astype(BF16)
        wdn_bf[...] = wdn_ref[...].astype(BF16)

    @pl.when(active)
    def _():
        de = wdn_ref.shape[0]
        half = x_ref.shape[1]
        xp = x_ref[...]
        xa = lax.bitcast_convert_type(xp & jnp.int32(-65536), F32).astype(BF16)
        xb = lax.bitcast_convert_type(lax.shift_left(xp, 16), F32).astype(BF16)
        gu = _dot(xa, wgu_bf[:half, :]) + _dot(xb, wgu_bf[half:, :]) + bgu_ref[...]
        g = jnp.minimum(gu[:, :de], SWIGLU_LIMIT)
        u = jnp.clip(gu[:, de:], -SWIGLU_LIMIT, SWIGLU_LIMIT)
        a = (u + 1.0) * g * _sigmoid(SWIGLU_ALPHA * g)
        y_ref[...] = _dot(a.astype(BF16), wdn_bf[...]) + bdn_ref[...]


def _experts(xb, block_expert, nb_used, w_gu, b_gu, w_dn, b_dn, layer):
    nrows, half = xb.shape
    bm = MOE_BLOCK
    depth, ne, d, two_de = w_gu.shape
    de = two_de // 2
    grid_spec = pltpu.PrefetchScalarGridSpec(
        num_scalar_prefetch=2,
        grid=(nrows // bm,),
        in_specs=[pl.BlockSpec((bm, half), lambda i, be, nb: (i, 0)),
                  pl.BlockSpec((None, None, d, two_de), lambda i, be, nb: (layer, be[i], 0, 0)),
                  pl.BlockSpec((None, None, 1, two_de), lambda i, be, nb: (layer, be[i], 0, 0)),
                  pl.BlockSpec((None, None, de, d), lambda i, be, nb: (layer, be[i], 0, 0)),
                  pl.BlockSpec((None, None, 1, d), lambda i, be, nb: (layer, be[i], 0, 0))],
        out_specs=pl.BlockSpec((bm, d), lambda i, be, nb: (i, 0)),
        scratch_shapes=[pltpu.VMEM((d, two_de), BF16), pltpu.VMEM((de, d), BF16)],
    )
    return pl.pallas_call(
        _expert_kernel,
        grid_spec=grid_spec,
        out_shape=jax.ShapeDtypeStruct((nrows, d), F32),
        compiler_params=_params(("arbitrary",)),
        name="moe_expert_ffn",
    )(block_expert, nb_used, xb, w_gu, b_gu.reshape(depth, ne, 1, two_de), w_dn, b_dn.reshape(depth, ne, 1, d))


def _sc_gather_rows(table, idx):
    n = idx.shape[0]
    width = table.shape[1]
    nw = SC_CORES * SC_SUBCORES
    nb = SC_GATHER_ROWS
    per_w = n // nw
    assert n % nw == 0 and per_w % nb == 0, (n, nw, nb)
    steps = per_w // nb
    mesh = plsc.VectorSubcoreMesh(core_axis_name="c", subcore_axis_name="s",
                                  num_cores=SC_CORES, num_subcores=SC_SUBCORES)

    def body(table_hbm, idx_hbm, out_hbm, idx_v, rows_v, sem):
        wid = lax.axis_index("s") * SC_CORES + lax.axis_index("c")
        base = wid * per_w

        @pl.loop(0, steps)
        def _(j):
            off = pl.multiple_of(base + j * nb, 8)
            pltpu.sync_copy(idx_hbm.at[pl.ds(off, nb)], idx_v)
            pltpu.async_copy(table_hbm.at[idx_v], rows_v, sem).wait()
            pltpu.sync_copy(rows_v, out_hbm.at[pl.ds(off, nb)])

    return pl.kernel(
        body,
        out_type=jax.ShapeDtypeStruct((n, width), table.dtype),
        mesh=mesh,
        scratch_types=[pltpu.VMEM((nb,), jnp.int32),
                       pltpu.VMEM((nb, width), table.dtype),
                       pltpu.SemaphoreType.DMA],
        name="sc_gather_rows",
    )(table, idx)


def _combine_kernel(xs_ref, yg_ref, gt_ref, gate_ref, fg_ref, o_ref, *, final):
    f = None
    for k in range(TOP_K):
        part = yg_ref[k] * gt_ref[:, k:k + 1]
        f = part if f is None else f + part
    out = xs_ref[...] + gate_ref[...] * f
    if final:
        out = _rms(out) * fg_ref[...]
    o_ref[...] = out


def _combine(xs, yg, gate_t, mods_l, final_g, final, dims):
    bsz, seq, ctx = dims
    r, d = xs.shape
    tm = ROW_TILE
    lt = seq // tm
    return pl.pallas_call(
        functools.partial(_combine_kernel, final=final),
        grid=(r // tm,),
        in_specs=[pl.BlockSpec((tm, d), lambda i: (i, 0)),
                  pl.BlockSpec((TOP_K, tm, d), lambda i: (0, i, 0)),
                  pl.BlockSpec((tm, TOP_K), lambda i: (i, 0)),
                  _mod_spec(5, lt, bsz, d),
                  pl.BlockSpec((1, d), lambda i: (0, 0))],
        out_specs=pl.BlockSpec((tm, d), lambda i: (i, 0)),
        out_shape=jax.ShapeDtypeStruct((r, d), F32),
        compiler_params=_params(("arbitrary",)),
        name="moe_combine_residual",
    )(xs, yg, gate_t, mods_l, final_g.reshape(1, d))


def _moe_layer(xs, mods_l, g2, w_router, b_router, w_gu, b_gu, w_dn, b_dn, final_g, layer, final, dims):
    r, d = xs.shape
    bm = MOE_BLOCK
    hp, top_idx, gate, rank, cnt = _router(xs, g2, mods_l, w_router, b_router, dims)
    counts = cnt[:, 0]
    padded = (counts + bm - 1) // bm * bm
    pad_ends = jnp.cumsum(padded)
    pad_starts = pad_ends - padded
    eids = jnp.arange(N_EXPERTS, dtype=jnp.int32)
    onehot = top_idx[None] == eids[:, None, None]
    dest = jnp.sum(jnp.where(onehot, pad_starts[:, None, None], 0), axis=0) + rank
    n_blocks = -(-(r * TOP_K + N_EXPERTS * (bm - 1)) // bm)
    blk_start = jnp.arange(n_blocks, dtype=jnp.int32) * bm
    block_expert = jnp.minimum(jnp.sum((pad_ends[None, :] <= blk_start[:, None]).astype(jnp.int32), axis=1),
                               N_EXPERTS - 1)
    nb_used = (pad_ends[-1] // bm).astype(jnp.int32).reshape(1)
    tok = jnp.broadcast_to(jnp.arange(r, dtype=jnp.int32), (TOP_K, r))
    row_tok = jnp.zeros((n_blocks * bm,), jnp.int32).at[dest.reshape(-1)].set(tok.reshape(-1))
    xb = _sc_gather_rows(hp, row_tok)
    yb = _experts(xb, block_expert, nb_used, w_gu, b_gu, w_dn, b_dn, layer)
    yg = _sc_gather_rows(yb, dest.reshape(-1)).reshape(TOP_K, r, d)
    return _combine(xs, yg, gate.T, mods_l, final_g, final, dims)


def kernel(x, c, ctx, c_ctx, ada_w, ada_b, norm1_g, norm2_g, ssd_w_in, ssd_conv_w, ssd_conv_b, ssd_dt_bias,
           ssd_a_log, ssd_d, ssd_norm_w, ssd_w_out, diff_w_qkv, diff_lam, diff_subln_w, diff_w_out, ml_w_in,
           ml_b_gates, ml_norm_w, ml_w_out, moe_w_router, moe_b_router, moe_w_gu, moe_b_gu, moe_w_dn, moe_b_dn,
           final_g):
    bsz, seq, d = x.shape
    n_ctx = ctx.shape[1]
    depth = ada_w.shape[0]
    dims = (bsz, seq, n_ctx)
    n_lat = bsz * seq
    xs = jnp.concatenate([x.reshape(n_lat, d), ctx.reshape(bsz * n_ctx, d)], axis=0)
    cond_rows = jnp.zeros((8, d), F32).at[:bsz].set(c).at[bsz].set(c_ctx)
    mods = _mods(cond_rows, ada_w, ada_b)
    mods = mods[:, :, :bsz + 1].reshape(depth, N_MOD, bsz + 1, 1, d)
    for i in range(depth):
        mods_l = mods[i]
        kind, j = i % 3, i // 3
        if kind == 0:
            xs = _ssd_layer(xs, mods_l, norm1_g[i], ssd_w_in[j], ssd_conv_w[j], ssd_conv_b[j], ssd_dt_bias[j],
                            ssd_a_log[j], ssd_d[j], ssd_norm_w[j], ssd_w_out[j], dims)
        elif kind == 1:
            lambda_init = 0.8 - 0.6 * math.exp(-0.3 * i)
            xs = _diff_layer(xs, mods_l, norm1_g[i], diff_w_qkv[j], diff_lam[j], diff_subln_w[j], diff_w_out[j],
                             lambda_init, dims)
        else:
            xs = _mlstm_layer(xs, mods_l, norm1_g[i], ml_w_in[j], ml_b_gates[j], ml_norm_w[j], ml_w_out[j], dims)
        xs = _moe_layer(xs, mods_l, norm2_g[i], moe_w_router[i], moe_b_router[i], moe_w_gu, moe_b_gu,
                        moe_w_dn, moe_b_dn, final_g, i, i == depth - 1, dims)
    return xs[:n_lat].reshape(bsz, seq, d)
```

```python
import functools
import math

import numpy as np
import jax
import jax.numpy as jnp
from jax import lax
from jax.experimental import pallas as pl
from jax.experimental.pallas import tpu as pltpu
from jax.experimental.pallas import tpu_sc as plsc

F32 = jnp.float32
BF16 = jnp.bfloat16
HIGHEST = lax.Precision.HIGHEST

GRID_W = 64
RMS_EPS = 1e-6
N_MOD = 6
SSD_HEAD_DIM = 64
SSD_HEADS = 32
SSD_GROUPS = 4
SSD_STATE = 128
SSD_INNER = SSD_HEADS * SSD_HEAD_DIM
SSD_BC = SSD_GROUPS * SSD_STATE
SSD_CONV_DIM = SSD_INNER + 2 * SSD_BC
DIFF_HEADS = 8
DIFF_HEAD_DIM = 64
DIFF_V_DIM = 128
ROPE_BASE = 10000.0
ROPE_Q = DIFF_HEAD_DIM // 4
ML_HEADS = 4
ML_QK_DIM = 128
ML_V_DIM = 256
ML_QK = ML_HEADS * ML_QK_DIM
ML_V = ML_HEADS * ML_V_DIM
N_EXPERTS = 32
TOP_K = 4
SWIGLU_LIMIT = 7.0
SWIGLU_ALPHA = 1.702
LOG2_E = 1.4426950408889634

LANES = 128
CHUNK = 128
ROW_TILE = 256
MOE_BLOCK = 256
ATTN_Q_TILE = 256
ATTN_UNROLL = 4
SC_CORES = 2
SC_SUBCORES = 16
SC_GATHER_ROWS = 64
VMEM_LIMIT = 56 * 1024 * 1024


def _params(sem):
    return pltpu.CompilerParams(dimension_semantics=sem, vmem_limit_bytes=VMEM_LIMIT)


def _dot(a, b):
    return jnp.dot(a, b, preferred_element_type=F32)


def _dot_nt(a, b):
    return lax.dot_general(a, b, (((1,), (1,)), ((), ())), preferred_element_type=F32)


def _dot_tn(a, b):
    return lax.dot_general(a, b, (((0,), (0,)), ((), ())), preferred_element_type=F32)


def _rms(x):
    return x * lax.rsqrt(jnp.mean(x * x, axis=-1, keepdims=True) + RMS_EPS)


def _sigmoid(x):
    return 1.0 / (1.0 + jnp.exp(-x))


def _silu(x):
    return x * _sigmoid(x)


def _softplus(x):
    return jnp.maximum(x, 0.0) + jnp.log(1.0 + jnp.exp(-jnp.abs(x)))


def _log_sigmoid(x):
    return -_softplus(-x)


def _mod_spec(which, lat_tiles_per_batch, bsz, d):
    return pl.BlockSpec((None, None, 1, d),
                        lambda i: (which, jnp.minimum(i // lat_tiles_per_batch, bsz), 0, 0))


def _mods_kernel(c_ref, w_ref, b_ref, o_ref):
    c = c_ref[...]
    cond = _silu(c)
    o_ref[...] = jnp.dot(cond, w_ref[...], precision=HIGHEST, preferred_element_type=F32) + b_ref[...]


def _mods(cond_rows, ada_w, ada_b):
    depth, d, _ = ada_w.shape
    nr = cond_rows.shape[0]
    return pl.pallas_call(
        _mods_kernel,
        grid=(depth, N_MOD),
        in_specs=[pl.BlockSpec((nr, d), lambda l, j: (0, 0)),
                  pl.BlockSpec((None, d, d), lambda l, j: (l, 0, j)),
                  pl.BlockSpec((None, 1, d), lambda l, j: (l, 0, j))],
        out_specs=pl.BlockSpec((None, None, nr, d), lambda l, j: (l, j, 0, 0)),
        out_shape=jax.ShapeDtypeStruct((depth, N_MOD, nr, d), F32),
        compiler_params=_params(("arbitrary", "arbitrary")),
        name="adaln_mods",
    )(cond_rows, ada_w, ada_b.reshape(depth, 1, N_MOD * d))


def _norm_mod(x, g, sh, sc):
    return (_rms(x) * g) * (1.0 + sc) + sh


def _inproj_kernel(x_ref, g_ref, sh_ref, sc_ref, w_ref, w2_ref, o_ref, o2_ref, *, n_chunk):
    hb = _norm_mod(x_ref[...], g_ref[...], sh_ref[...], sc_ref[...]).astype(BF16)
    n = o_ref.shape[1]
    for n0 in range(0, n, n_chunk):
        o_ref[:, n0:n0 + n_chunk] = _dot(hb, w_ref[:, n0:n0 + n_chunk])
    o2_ref[...] = _dot(hb, w2_ref[...])


def _col_chunk(n):
    for c in (512, 384, 256, 128):
        if n % c == 0:
            return c
    raise ValueError(n)


def _inproj(xs, g, mods_l, w_bf16, w2_bf16, dims):
    bsz, seq, ctx = dims
    r, d = xs.shape
    n = w_bf16.shape[1]
    n2 = w2_bf16.shape[1]
    tm = ROW_TILE
    lt = seq // tm
    return pl.pallas_call(
        functools.partial(_inproj_kernel, n_chunk=_col_chunk(n)),
        grid=(r // tm,),
        in_specs=[pl.BlockSpec((tm, d), lambda i: (i, 0)),
                  pl.BlockSpec((1, d), lambda i: (0, 0)),
                  _mod_spec(0, lt, bsz, d), _mod_spec(1, lt, bsz, d),
                  pl.BlockSpec((d, n), lambda i: (0, 0)),
                  pl.BlockSpec((d, n2), lambda i: (0, 0))],
        out_specs=[pl.BlockSpec((tm, n), lambda i: (i, 0)), pl.BlockSpec((tm, n2), lambda i: (i, 0))],
        out_shape=[jax.ShapeDtypeStruct((r, n), F32), jax.ShapeDtypeStruct((r, n2), F32)],
        compiler_params=_params(("arbitrary",)),
        name="norm_mod_inproj",
    )(xs, g.reshape(1, d), mods_l, mods_l, w_bf16, w2_bf16)


def _conv_kernel(first_ref, last_ref, cur_ref, prev_ref, nxt_ref, w_ref, b_ref, o_ref):
    i = pl.program_id(0)
    cur = cur_ref[...]
    tm = cur.shape[0]
    row = lax.broadcasted_iota(jnp.int32, cur.shape, 0)
    prev_row = jnp.where(first_ref[i] == 1, 0.0, prev_ref[7:8, :])
    next_row = jnp.where(last_ref[i] == 1, 0.0, nxt_ref[0:1, :])
    down = jnp.where(row == 0, prev_row, pltpu.roll(cur, 1, 0))
    up = jnp.where(row == tm - 1, next_row, pltpu.roll(cur, tm - 1, 0))
    y = w_ref[0:1, :] * down + w_ref[1:2, :] * cur + w_ref[2:3, :] * up + b_ref[...]
    o_ref[...] = _silu(y)


def _ssd_conv(t_main, conv_w, conv_b, dims):
    bsz, seq, ctx = dims
    r = t_main.shape[0]
    tm = ROW_TILE
    cw = 1024
    ncol = SSD_CONV_DIM // cw
    off = SSD_INNER // cw
    starts = [b * seq for b in range(bsz)] + [bsz * seq + b * ctx for b in range(bsz)]
    ends = [s + (seq if k < bsz else ctx) for k, s in enumerate(starts)]
    first = np.array([1 if (i * tm) in starts else 0 for i in range(r // tm)], np.int32)
    last = np.array([1 if ((i + 1) * tm) in ends else 0 for i in range(r // tm)], np.int32)
    sub = tm // 8
    nblk8 = r // 8
    grid_spec = pltpu.PrefetchScalarGridSpec(
        num_scalar_prefetch=2,
        grid=(r // tm, ncol),
        in_specs=[pl.BlockSpec((tm, cw), lambda i, j, f, l: (i, off + j)),
                  pl.BlockSpec((8, cw), lambda i, j, f, l: (jnp.maximum(i * sub - 1, 0), off + j)),
                  pl.BlockSpec((8, cw), lambda i, j, f, l: (jnp.minimum((i + 1) * sub, nblk8 - 1), off + j)),
                  pl.BlockSpec((3, cw), lambda i, j, f, l: (0, j)),
                  pl.BlockSpec((1, cw), lambda i, j, f, l: (0, j))],
        out_specs=pl.BlockSpec((tm, cw), lambda i, j, f, l: (i, j)),
    )
    return pl.pallas_call(
        _conv_kernel,
        grid_spec=grid_spec,
        out_shape=jax.ShapeDtypeStruct((r, SSD_CONV_DIM), F32),
        compiler_params=_params(("arbitrary", "arbitrary")),
        name="ssd_conv_silu",
    )(jnp.asarray(first), jnp.asarray(last), t_main, t_main, t_main, conv_w, conv_b.reshape(1, -1))


def _chunk_block(b, d, s, bsz, seq, ctx):
    nc_ctx = ctx // CHUNK
    nc_lat = seq // CHUNK
    in_ctx = s < nc_ctx
    pc = jnp.where(d == 0, s, nc_ctx - 1 - s)
    pls = s - nc_ctx
    plat = jnp.where(d == 0, pls, nc_lat - 1 - pls)
    ctx_blk = bsz * nc_lat + b * nc_ctx + pc
    lat_blk = b * nc_lat + plat
    return jnp.where(in_ctx, ctx_blk, lat_blk)


def _dir_masks(d):
    ii = lax.broadcasted_iota(jnp.int32, (CHUNK, CHUNK), 0)
    jj = lax.broadcasted_iota(jnp.int32, (CHUNK, CHUNK), 1)
    sign = 1 - 2 * d
    return (ii - jj) * sign >= 0


def _ssd_scan_kernel(x_ref, b_ref, c_ref, dt_ref, bias_ref, a_ref, y_ref, st_ref):
    d = pl.program_id(1)
    s = pl.program_id(2)

    @pl.when(s == 0)
    def _():
        st_ref[...] = jnp.zeros_like(st_ref)

    mask = _dir_masks(d)
    tri = jnp.where(mask, 1.0, 0.0).astype(F32)
    dt = _softplus(dt_ref[...] + bias_ref[...])
    dta = dt * a_ref[...]
    la = jnp.dot(tri, dta, precision=HIGHEST, preferred_element_type=F32)
    total = jnp.sum(dta, axis=0, keepdims=True)
    la_t = la.T
    dt_t = dt.T
    ws_t = (jnp.exp(total - la) * dt).T
    e_tot = jnp.exp(total)
    lane = lax.broadcasted_iota(jnp.int32, (CHUNK, LANES), 1)
    lo_half = lane < SSD_HEAD_DIM
    hg = SSD_HEADS // SSD_GROUPS
    for g in range(SSD_GROUPS):
        bg32 = b_ref[:, g * SSD_STATE:(g + 1) * SSD_STATE]
        cg = c_ref[:, g * SSD_STATE:(g + 1) * SSD_STATE].astype(BF16)
        cb = _dot_nt(cg, bg32.astype(BF16))
        bg_t = bg32.T
        for pr in range(hg // 2):
            h0 = g * hg + 2 * pr
            col0 = slice((h0 * SSD_HEAD_DIM), (h0 + 2) * SSD_HEAD_DIM)
            xpb = x_ref[:, col0].astype(BF16)
            ys, sts, la_cols = [], [], []
            for h in (h0, h0 + 1):
                la_col = jnp.broadcast_to(la[:, h:h + 1], (CHUNK, LANES))
                la_cols.append(la_col)
                w = jnp.exp(jnp.where(mask, la_col - la_t[h:h + 1, :], -jnp.inf)) * cb * dt_t[h:h + 1, :]
                ys.append(_dot(w.astype(BF16), xpb))
                sts.append(_dot((bg_t * ws_t[h:h + 1, :]).astype(BF16), xpb))
            st = st_ref[g, :, pr * LANES:(pr + 1) * LANES]
            e_pair = jnp.exp(jnp.where(lo_half, la_cols[0], la_cols[1]))
            y_ref[:, col0] = jnp.where(lo_half, ys[0], ys[1]) + e_pair * _dot(cg, st.astype(BF16))
            et_pair = jnp.where(lo_half[0:1], e_tot[:, h0:h0 + 1], e_tot[:, h0 + 1:h0 + 2])
            st_ref[g, :, pr * LANES:(pr + 1) * LANES] = et_pair * st + jnp.where(lo_half, sts[0], sts[1])


def _ssd_scan(xbc, dt_raw, dt_bias_pad, a_pad, dims):
    bsz, seq, ctx = dims
    r = xbc.shape[0]
    nsteps = (seq + ctx) // CHUNK
    blk = functools.partial(_chunk_block, bsz=bsz, seq=seq, ctx=ctx)
    xw = SSD_INNER
    return pl.pallas_call(
        _ssd_scan_kernel,
        grid=(bsz, 2, nsteps),
        in_specs=[pl.BlockSpec((CHUNK, xw), lambda b, d, s: (blk(b, d, s), 0)),
                  pl.BlockSpec((CHUNK, SSD_BC), lambda b, d, s: (blk(b, d, s), xw // SSD_BC)),
                  pl.BlockSpec((CHUNK, SSD_BC), lambda b, d, s: (blk(b, d, s), xw // SSD_BC + 1)),
                  pl.BlockSpec((CHUNK, LANES), lambda b, d, s: (blk(b, d, s), d)),
                  pl.BlockSpec((None, 1, LANES), lambda b, d, s: (d, 0, 0)),
                  pl.BlockSpec((None, 1, LANES), lambda b, d, s: (d, 0, 0))],
        out_specs=pl.BlockSpec((None, CHUNK, xw), lambda b, d, s: (d, blk(b, d, s), 0)),
        out_shape=jax.ShapeDtypeStruct((2, r, xw), F32),
        scratch_shapes=[pltpu.VMEM((SSD_GROUPS, SSD_STATE, xw // SSD_GROUPS), F32)],
        compiler_params=_params(("arbitrary", "arbitrary", "arbitrary")),
        name="ssd_scan",
    )(xbc, xbc, xbc, dt_raw, dt_bias_pad, a_pad)


def _ssd_out_kernel(yf_ref, yb_ref, xc_ref, z_ref, dexp_ref, nw_ref, w_ref, xs_ref, gate_ref, o_ref):
    y = yf_ref[...] + yb_ref[...] + dexp_ref[...] * xc_ref[...]
    y = y * _silu(z_ref[...])
    a = (_rms(y) * nw_ref[...]).astype(BF16)
    o_ref[...] = xs_ref[...] + gate_ref[...] * _dot(a, w_ref[...])


def _ssd_out(y2, xbc, t_main, d_exp, norm_w, w_out_bf16, xs, mods_l, dims):
    bsz, seq, ctx = dims
    r, d = xs.shape
    tm = ROW_TILE
    lt = seq // tm
    xw = SSD_INNER
    return pl.pallas_call(
        _ssd_out_kernel,
        grid=(r // tm,),
        in_specs=[pl.BlockSpec((None, tm, xw), lambda i: (0, i, 0)),
                  pl.BlockSpec((None, tm, xw), lambda i: (1, i, 0)),
                  pl.BlockSpec((tm, xw), lambda i: (i, 0)),
                  pl.BlockSpec((tm, xw), lambda i: (i, 0)),
                  pl.BlockSpec((1, xw), lambda i: (0, 0)),
                  pl.BlockSpec((1, xw), lambda i: (0, 0)),
                  pl.BlockSpec((xw, d), lambda i: (0, 0)),
                  pl.BlockSpec((tm, d), lambda i: (i, 0)),
                  _mod_spec(2, lt, bsz, d)],
        out_specs=pl.BlockSpec((tm, d), lambda i: (i, 0)),
        out_shape=jax.ShapeDtypeStruct((r, d), F32),
        compiler_params=_params(("arbitrary",)),
        name="ssd_gated_norm_outproj",
    )(y2, y2, xbc, t_main, d_exp, norm_w.reshape(1, xw), w_out_bf16, xs, mods_l)


def _ssd_layer(xs, mods_l, g1, w_in, conv_w, conv_b, dt_bias, a_log, d_skip, norm_w, w_out, dims):
    d = xs.shape[1]
    main = SSD_INNER + SSD_CONV_DIM
    w_main = w_in[:, :main].astype(BF16)
    w_dt = jnp.zeros((d, 2 * LANES), F32)
    w_dt = w_dt.at[:, :SSD_HEADS].set(w_in[:, main:main + SSD_HEADS])
    w_dt = w_dt.at[:, LANES:LANES + SSD_HEADS].set(w_in[:, main + SSD_HEADS:]).astype(BF16)
    t_main, dt_raw = _inproj(xs, g1, mods_l, w_main, w_dt, dims)
    xbc = _ssd_conv(t_main, conv_w, conv_b, dims)
    pad = jnp.zeros((2, 1, LANES - SSD_HEADS), F32)
    bias_pad = jnp.concatenate([dt_bias.astype(F32).reshape(2, 1, SSD_HEADS), pad], axis=-1)
    a_pad = jnp.concatenate([-jnp.exp(a_log.astype(F32)).reshape(2, 1, SSD_HEADS), pad], axis=-1)
    y2 = _ssd_scan(xbc, dt_raw, bias_pad, a_pad, dims)
    d_exp = jnp.repeat(d_skip.astype(F32), SSD_HEAD_DIM).reshape(1, SSD_INNER)
    return _ssd_out(y2, xbc, t_main, d_exp, norm_w, w_out.astype(BF16), xs, mods_l, dims)


def _qkv_rope_kernel(x_ref, g_ref, sh_ref, sc_ref, w_ref, cos_ref, sin_ref, q_ref, k_ref, v_ref, *, n_lat_tiles):
    i = pl.program_id(0)
    hb = _norm_mod(x_ref[...], g_ref[...], sh_ref[...], sc_ref[...]).astype(BF16)
    d = x_ref.shape[1]
    is_ctx = i >= n_lat_tiles
    cos = jnp.where(is_ctx, 1.0, cos_ref[...])
    sin = jnp.where(is_ctx, 0.0, sin_ref[...])
    nrep = d // LANES
    cos = jnp.concatenate([cos] * nrep, axis=1)
    sin = jnp.concatenate([sin] * nrep, axis=1)

    def mm(c):
        return _dot(hb, w_ref[:, c * d:(c + 1) * d])

    q = mm(0) * cos + mm(3) * sin
    q_ref[...] = (q * (DIFF_HEAD_DIM ** -0.5 * LOG2_E)).astype(BF16)
    k_ref[...] = (mm(1) * cos + mm(4) * sin).astype(BF16)
    v_ref[...] = mm(2).astype(BF16)


def _rope_tables(seq):
    rows = seq // GRID_W
    row = jnp.repeat(jnp.arange(rows, dtype=F32), GRID_W)
    col = jnp.tile(jnp.arange(GRID_W, dtype=F32), rows)
    inv = ROPE_BASE ** (-jnp.arange(ROPE_Q, dtype=F32) / ROPE_Q)
    ang_r = row[:, None] * inv
    ang_c = col[:, None] * inv
    cos = jnp.concatenate([jnp.cos(ang_r), jnp.cos(ang_r), jnp.cos(ang_c), jnp.cos(ang_c)], axis=1)
    sin = jnp.concatenate([-jnp.sin(ang_r), jnp.sin(ang_r), -jnp.sin(ang_c), jnp.sin(ang_c)], axis=1)
    return jnp.tile(cos, (1, 2)), jnp.tile(sin, (1, 2))


def _rope_partner_cols(d):
    col = np.arange(d)
    within = col % (2 * ROPE_Q)
    return np.where(within < ROPE_Q, col + ROPE_Q, col - ROPE_Q)


def _qkv_rope(xs, g, mods_l, w_qkv, dims):
    bsz, seq, ctx = dims
    r, d = xs.shape
    tm = ROW_TILE
    lt = seq // tm
    perm = _rope_partner_cols(d)
    wq, wk, wv = w_qkv[:, :d], w_qkv[:, d:2 * d], w_qkv[:, 2 * d:]
    w_all = jnp.concatenate([wq, wk, wv, wq[:, perm], wk[:, perm]], axis=1).astype(BF16)
    cos, sin = _rope_tables(seq)
    out = jax.ShapeDtypeStruct((r, d), BF16)
    row_spec = pl.BlockSpec((tm, d), lambda i: (i, 0))
    tab_spec = pl.BlockSpec((tm, LANES), lambda i: (i % lt, 0))
    return pl.pallas_call(
        functools.partial(_qkv_rope_kernel, n_lat_tiles=bsz * lt),
        grid=(r // tm,),
        in_specs=[row_spec, pl.BlockSpec((1, d), lambda i: (0, 0)),
                  _mod_spec(0, lt, bsz, d), _mod_spec(1, lt, bsz, d),
                  pl.BlockSpec((d, 5 * d), lambda i: (0, 0)), tab_spec, tab_spec],
        out_specs=[row_spec, row_spec, row_spec],
        out_shape=[out, out, out],
        compiler_params=_params(("arbitrary",)),
        name="norm_mod_qkv_rope",
    )(xs, g.reshape(1, d), mods_l, mods_l, w_all, cos, sin)


def _lane_fold(x, op):
    parts = [x[:, t * LANES:(t + 1) * LANES] for t in range(x.shape[1] // LANES)]
    return functools.reduce(op, parts)


def _attn_kernel(lam_ref, q_ref, kc_ref, vc_ref, kx_ref, vx_ref, nw_ref, o_ref,
                 sc_ref, sx_ref, acc_ref, m_ref, l_ref, *, n_lat_q, tk, unroll, lambda_init):
    i = pl.program_id(2)
    q = q_ref[...]
    tq = q.shape[0]
    lane = lax.broadcasted_iota(jnp.int32, q.shape, 1)
    zero = jnp.zeros_like(q)
    qq = jnp.concatenate([jnp.where(lane < DIFF_HEAD_DIM, q, zero),
                          jnp.where(lane >= DIFF_HEAD_DIM, q, zero)], axis=0)
    is_latent = i < n_lat_q
    n_x = kx_ref.shape[0] // tk

    s = _dot_nt(qq, kc_ref[...])
    sc_ref[...] = s
    m_ref[...] = _lane_fold(s, jnp.maximum)

    @pl.when(is_latent)
    def _():
        def p1(j, m_):
            kj = kx_ref[pl.ds(pl.multiple_of(j * tk, tk), tk), :]
            sj = _dot_nt(qq, kj)
            sx_ref[j] = sj
            return jnp.maximum(m_, _lane_fold(sj, jnp.maximum))

        m_ref[...] = lax.fori_loop(0, n_x, p1, m_ref[...], unroll=unroll)

    mrow = jnp.max(m_ref[...], axis=1, keepdims=True)

    p = jnp.exp2(sc_ref[...] - mrow)
    acc_ref[...] = _dot(p.astype(BF16), vc_ref[...])
    l_ref[...] = _lane_fold(p, jnp.add)

    @pl.when(is_latent)
    def _():
        def p2(j, l_):
            vj = vx_ref[pl.ds(pl.multiple_of(j * tk, tk), tk), :]
            pj = jnp.exp2(sx_ref[j] - mrow)
            acc_ref[...] += _dot(pj.astype(BF16), vj)
            return l_ + _lane_fold(pj, jnp.add)

        l_ref[...] = lax.fori_loop(0, n_x, p2, l_ref[...], unroll=unroll)

    on = acc_ref[...] / jnp.sum(l_ref[...], axis=1, keepdims=True)
    o = on[:tq] - lam_ref[0] * on[tq:]
    o = _rms(o) * nw_ref[...] * (1.0 - lambda_init)
    o_ref[...] = o.astype(o_ref.dtype)


def _diff_attention(q, k, v, lam_full, subln_w, lambda_init, dims):
    bsz, seq, ctx = dims
    r, d = q.shape
    tq = ATTN_Q_TILE
    tk = min(512, seq)
    n_lat_q = seq // tq
    n_ctx_q = ctx // tq
    nq = n_lat_q + n_ctx_q

    def q_idx(b, h, i):
        return (jnp.where(i < n_lat_q, b * n_lat_q + i, bsz * n_lat_q + b * n_ctx_q + (i - n_lat_q)), h)

    ctx_spec = pl.BlockSpec((ctx, LANES), lambda b, h, i: (bsz * seq // ctx + b, h))
    lat_spec = pl.BlockSpec((seq, LANES), lambda b, h, i: (b, h))
    return pl.pallas_call(
        functools.partial(_attn_kernel, n_lat_q=n_lat_q, tk=tk, unroll=min(ATTN_UNROLL, seq // tk),
                          lambda_init=lambda_init),
        grid=(bsz, DIFF_HEADS, nq),
        in_specs=[pl.BlockSpec(memory_space=pltpu.SMEM),
                  pl.BlockSpec((tq, LANES), q_idx),
                  ctx_spec, ctx_spec, lat_spec, lat_spec,
                  pl.BlockSpec((1, LANES), lambda b, h, i: (0, 0))],
        out_specs=pl.BlockSpec((tq, LANES), q_idx),
        out_shape=jax.ShapeDtypeStruct((r, d), BF16),
        scratch_shapes=[pltpu.VMEM((2 * tq, ctx), F32),
                        pltpu.VMEM((seq // tk, 2 * tq, tk), F32),
                        pltpu.VMEM((2 * tq, LANES), F32),
                        pltpu.VMEM((2 * tq, LANES), F32),
                        pltpu.VMEM((2 * tq, LANES), F32)],
        compiler_params=_params(("arbitrary", "arbitrary", "arbitrary")),
        name="diff_attention",
    )(lam_full, q, k, v, k, v, subln_w.reshape(1, LANES))


def _proj_res_kernel(a_ref, w_ref, xs_ref, gate_ref, o_ref):
    o_ref[...] = xs_ref[...] + gate_ref[...] * _dot(a_ref[...], w_ref[...])


def _proj_res(a_bf16, w_bf16, xs, mods_l, dims):
    bsz, seq, ctx = dims
    r, d = xs.shape
    kdim = a_bf16.shape[1]
    tm = ROW_TILE
    lt = seq // tm
    return pl.pallas_call(
        _proj_res_kernel,
        grid=(r // tm,),
        in_specs=[pl.BlockSpec((tm, kdim), lambda i: (i, 0)),
                  pl.BlockSpec((kdim, d), lambda i: (0, 0)),
                  pl.BlockSpec((tm, d), lambda i: (i, 0)),
                  _mod_spec(2, lt, bsz, d)],
        out_specs=pl.BlockSpec((tm, d), lambda i: (i, 0)),
        out_shape=jax.ShapeDtypeStruct((r, d), F32),
        compiler_params=_params(("arbitrary",)),
        name="outproj_gate_residual",
    )(a_bf16, w_bf16, xs, mods_l)


def _diff_layer(xs, mods_l, g1, w_qkv, lam, subln_w, w_out, lambda_init, dims):
    q, k, v = _qkv_rope(xs, g1, mods_l, w_qkv, dims)
    lam32 = lam.astype(F32)
    lam_full = (jnp.exp(jnp.sum(lam32[0] * lam32[1])) - jnp.exp(jnp.sum(lam32[2] * lam32[3]))
                + lambda_init).reshape(1)
    o = _diff_attention(q, k, v, lam_full, subln_w, lambda_init, dims)
    return _proj_res(o, w_out.astype(BF16), xs, mods_l, dims)


def _mlstm_scan_kernel(q_ref, k_ref, v_ref, g_ref, bias_ref, h_ref, c_ref, n_ref, m_ref):
    d = pl.program_id(1)
    s = pl.program_id(2)

    @pl.when(s == 0)
    def _():
        c_ref[...] = jnp.zeros_like(c_ref)
        n_ref[...] = jnp.zeros_like(n_ref)
        m_ref[...] = jnp.zeros_like(m_ref)

    mask = _dir_masks(d)
    tri = jnp.where(mask, 1.0, 0.0).astype(F32)
    gates = g_ref[...] + bias_ref[...]
    ig = gates[:, 0:ML_HEADS]
    lf = _log_sigmoid(gates)
    bcum = jnp.dot(tri, lf, precision=HIGHEST, preferred_element_type=F32)
    btot = jnp.sum(lf, axis=0, keepdims=True)
    bcum_t = bcum.T
    gates_t = gates.T
    row_last = jnp.where(d == 0, CHUNK - 1, 0)
    rsel = lax.broadcasted_iota(jnp.int32, (CHUNK, 1), 0) == row_last
    for h in range(ML_HEADS):
        fcol = ML_HEADS + h
        m_prev = m_ref[h:h + 1, 0:1]
        bcol = bcum[:, fcol:fcol + 1]
        brow = bcum_t[fcol:fcol + 1, :]
        irow = gates_t[h:h + 1, :]
        icol = ig[:, h:h + 1]
        gcol = bcol + m_prev
        dmat = jnp.where(mask, bcol - brow + irow, -jnp.inf)
        mt = jnp.maximum(gcol, jnp.max(dmat, axis=1, keepdims=True))
        q32 = q_ref[:, h * ML_QK_DIM:(h + 1) * ML_QK_DIM] * (ML_QK_DIM ** -0.5)
        qh = q32.astype(BF16)
        kh32 = k_ref[:, h * ML_QK_DIM:(h + 1) * ML_QK_DIM]
        kh = kh32.astype(BF16)
        vh = v_ref[:, h * ML_V_DIM:(h + 1) * ML_V_DIM].astype(BF16)
        sm = _dot_nt(qh, kh) * jnp.exp(dmat - mt)
        inter = jnp.exp(gcol - mt)
        cst = c_ref[h]
        nst = n_ref[h:h + 1, :]
        num = _dot(sm.astype(BF16), vh) + inter * _dot(qh, cst.astype(BF16))
        qn = jnp.sum(q32 * nst, axis=1, keepdims=True)
        den = jnp.sum(sm, axis=1, keepdims=True) + inter * qn
        h_ref[:, h * ML_V_DIM:(h + 1) * ML_V_DIM] = num / jnp.maximum(jnp.abs(den), jnp.exp(-mt))
        m_new = jnp.sum(jnp.where(rsel, mt, 0.0), axis=0, keepdims=True)
        btot_h = btot[:, fcol:fcol + 1]
        wk = jnp.exp(btot_h - bcol + icol - m_new)
        cscale = jnp.exp(btot_h + m_prev - m_new)
        kw = kh32 * wk
        c_ref[h] = cscale * cst + _dot_tn(kw.astype(BF16), vh)
        n_ref[h:h + 1, :] = cscale * nst + jnp.sum(kw, axis=0, keepdims=True)
        m_ref[h:h + 1, :] = jnp.broadcast_to(m_new, (1, LANES))


def _mlstm_scan(t_main, gates_raw, bias_pad, dims):
    bsz, seq, ctx = dims
    r = t_main.shape[0]
    nsteps = (seq + ctx) // CHUNK
    blk = functools.partial(_chunk_block, bsz=bsz, seq=seq, ctx=ctx)
    return pl.pallas_call(
        _mlstm_scan_kernel,
        grid=(bsz, 2, nsteps),
        in_specs=[pl.BlockSpec((CHUNK, ML_QK), lambda b, d, s: (blk(b, d, s), 0)),
                  pl.BlockSpec((CHUNK, ML_QK), lambda b, d, s: (blk(b, d, s), 1)),
                  pl.BlockSpec((CHUNK, ML_V), lambda b, d, s: (blk(b, d, s), 1)),
                  pl.BlockSpec((CHUNK, LANES), lambda b, d, s: (blk(b, d, s), d)),
                  pl.BlockSpec((None, 1, LANES), lambda b, d, s: (d, 0, 0))],
        out_specs=pl.BlockSpec((None, CHUNK, ML_V), lambda b, d, s: (d, blk(b, d, s), 0)),
        out_shape=jax.ShapeDtypeStruct((2, r, ML_V), F32),
        scratch_shapes=[pltpu.VMEM((ML_HEADS, ML_QK_DIM, ML_V_DIM), F32),
                        pltpu.VMEM((8, ML_QK_DIM), F32),
                        pltpu.VMEM((8, LANES), F32)],
        compiler_params=_params(("arbitrary", "arbitrary", "arbitrary")),
        name="mlstm_scan",
    )(t_main, t_main, t_main, gates_raw, bias_pad)


def _mlstm_out_kernel(hf_ref, hb_ref, o_ref_in, nw_ref, w_ref, xs_ref, gate_ref, out_ref):
    u = None
    for h in range(ML_HEADS):
        cs = slice(h * ML_V_DIM, (h + 1) * ML_V_DIM)
        a = _sigmoid(o_ref_in[:, cs]) * (hf_ref[:, cs] + hb_ref[:, cs])
        a = (_rms(a) * nw_ref[...]).astype(BF16)
        part = _dot(a, w_ref[cs, :])
        u = part if u is None else u + part
    out_ref[...] = xs_ref[...] + gate_ref[...] * u


def _mlstm_out(h2, t_main, norm_w, w_out_bf16, xs, mods_l, dims):
    bsz, seq, ctx = dims
    r, d = xs.shape
    tm = ROW_TILE
    lt = seq // tm
    o_blk = (2 * ML_QK + ML_V) // ML_V
    return pl.pallas_call(
        _mlstm_out_kernel,
        grid=(r // tm,),
        in_specs=[pl.BlockSpec((None, tm, ML_V), lambda i: (0, i, 0)),
                  pl.BlockSpec((None, tm, ML_V), lambda i: (1, i, 0)),
                  pl.BlockSpec((tm, ML_V), lambda i: (i, o_blk)),
                  pl.BlockSpec((1, ML_V_DIM), lambda i: (0, 0)),
                  pl.BlockSpec((ML_V, d), lambda i: (0, 0)),
                  pl.BlockSpec((tm, d), lambda i: (i, 0)),
                  _mod_spec(2, lt, bsz, d)],
        out_specs=pl.BlockSpec((tm, d), lambda i: (i, 0)),
        out_shape=jax.ShapeDtypeStruct((r, d), F32),
        compiler_params=_params(("arbitrary",)),
        name="mlstm_norm_outproj",
    )(h2, h2, t_main, norm_w.reshape(1, ML_V_DIM), w_out_bf16, xs, mods_l)


def _mlstm_layer(xs, mods_l, g1, w_in, b_gates, norm_w, w_out, dims):
    d = xs.shape[1]
    main = 2 * ML_QK + 2 * ML_V
    w_main = w_in[:, :main].astype(BF16)
    wg = w_in[:, main:].reshape(d, 4, ML_HEADS)
    w_g = jnp.zeros((d, 2 * LANES), F32)
    bias = jnp.zeros((2, 1, LANES), F32)
    for dr in range(2):
        w_g = w_g.at[:, dr * LANES:dr * LANES + 2 * ML_HEADS].set(
            wg[:, 2 * dr:2 * dr + 2].reshape(d, 2 * ML_HEADS))
        bias = bias.at[dr, 0, :2 * ML_HEADS].set(b_gates.astype(F32)[2 * dr:2 * dr + 2].reshape(2 * ML_HEADS))
    t_main, gates_raw = _inproj(xs, g1, mods_l, w_main, w_g.astype(BF16), dims)
    h2 = _mlstm_scan(t_main, gates_raw, bias, dims)
    return _mlstm_out(h2, t_main, norm_w, w_out.astype(BF16), xs, mods_l, dims)


def _router_kernel(x_ref, g_ref, sh_ref, sc_ref, wr_ref, br_ref, h_ref, idx_ref, gate_ref, rank_ref, cnt_ref,
                   carry_ref):
    i = pl.program_id(0)

    @pl.when(i == 0)
    def _():
        carry_ref[...] = jnp.zeros_like(carry_ref)

    h = _norm_mod(x_ref[...], g_ref[...], sh_ref[...], sc_ref[...])
    tm, d = h.shape
    h_ref[...] = _pack_bf16_pairs(h)
    logits = lax.dot_general(wr_ref[...], h, (((1,), (1,)), ((), ())), precision=HIGHEST,
                             preferred_element_type=F32) + br_ref[...]
    eidx = lax.broadcasted_iota(jnp.int32, logits.shape, 0)
    work = logits
    vals, idxs = [], []
    picked = jnp.zeros(logits.shape, F32)
    for _ in range(TOP_K):
        mx = jnp.max(work, axis=0, keepdims=True)
        ix = jnp.min(jnp.where(work == mx, eidx, N_EXPERTS), axis=0, keepdims=True)
        sel = eidx == ix
        vals.append(mx)
        idxs.append(ix)
        picked = jnp.where(sel, 1.0, picked)
        work = jnp.where(sel, -jnp.inf, work)
    es = [jnp.exp(v - vals[0]) for v in vals]
    tot = es[0] + es[1] + es[2] + es[3]
    jj = lax.broadcasted_iota(jnp.int32, (tm, tm), 0)
    ii = lax.broadcasted_iota(jnp.int32, (tm, tm), 1)
    upper = jnp.where(jj <= ii, 1.0, 0.0).astype(BF16)
    incl = _dot(picked.astype(BF16), upper)
    carry = carry_ref[:, 0:1]
    excl = incl - picked + carry
    for k in range(TOP_K):
        idx_ref[k:k + 1, :] = idxs[k]
        gate_ref[k:k + 1, :] = es[k] / tot
        rk = jnp.sum(jnp.where(eidx == idxs[k], excl, 0.0), axis=0, keepdims=True)
        rank_ref[k:k + 1, :] = rk.astype(jnp.int32)
    new_carry = carry + jnp.sum(picked, axis=1, keepdims=True)
    carry_ref[...] = jnp.broadcast_to(new_carry, carry_ref.shape)
    cnt_ref[...] = jnp.broadcast_to(new_carry, cnt_ref.shape).astype(jnp.int32)


def _router(xs, g2, mods_l, w_router, b_router, dims):
    bsz, seq, ctx = dims
    r, d = xs.shape
    tm = ROW_TILE
    lt = seq // tm
    tok_spec = pl.BlockSpec((TOP_K, tm), lambda i: (0, i))
    return pl.pallas_call(
        _router_kernel,
        grid=(r // tm,),
        in_specs=[pl.BlockSpec((tm, d), lambda i: (i, 0)),
                  pl.BlockSpec((1, d), lambda i: (0, 0)),
                  _mod_spec(3, lt, bsz, d), _mod_spec(4, lt, bsz, d),
                  pl.BlockSpec((N_EXPERTS, d), lambda i: (0, 0)),
                  pl.BlockSpec((N_EXPERTS, 1), lambda i: (0, 0))],
        out_specs=[pl.BlockSpec((tm, d // 2), lambda i: (i, 0)), tok_spec, tok_spec, tok_spec,
                   pl.BlockSpec((N_EXPERTS, LANES), lambda i: (0, 0))],
        out_shape=[jax.ShapeDtypeStruct((r, d // 2), jnp.int32),
                   jax.ShapeDtypeStruct((TOP_K, r), jnp.int32),
                   jax.ShapeDtypeStruct((TOP_K, r), F32),
                   jax.ShapeDtypeStruct((TOP_K, r), jnp.int32),
                   jax.ShapeDtypeStruct((N_EXPERTS, LANES), jnp.int32)],
        scratch_shapes=[pltpu.VMEM((N_EXPERTS, LANES), F32)],
        compiler_params=_params(("arbitrary",)),
        name="norm_mod_router_top4",
    )(xs, g2.reshape(1, d), mods_l, mods_l, w_router.T, b_router.reshape(N_EXPERTS, 1))


def _pack_bf16_pairs(x):
    w = x.shape[1]
    xr = x.astype(BF16).astype(F32)
    hi = lax.bitcast_convert_type(xr[:, :w // 2], jnp.int32)
    lo = lax.bitcast_convert_type(xr[:, w // 2:], jnp.int32)
    return hi | lax.shift_right_logical(lo, 16)


def _unpack_bf16_pairs(xp):
    hi = lax.bitcast_convert_type(xp & jnp.int32(-65536), F32)
    lo = lax.bitcast_convert_type(lax.shift_left(xp, 16), F32)
    return hi, lo


def _expert_kernel(be_ref, nb_ref, first_ref, slot_ref, next_ref, x_ref, wgu_hbm, bgu_ref, wdn_hbm, bdn_ref, y_ref,
                   wgu_buf, wdn_buf, wgu_bf, wdn_bf, sems, *, layer):
    i = pl.program_id(0)
    active = i < nb_ref[0]
    slot = slot_ref[i]

    def fetch(e, s):
        return (pltpu.make_async_copy(wgu_hbm.at[layer, e], wgu_buf.at[s], sems.at[s, 0]),
                pltpu.make_async_copy(wdn_hbm.at[layer, e], wdn_buf.at[s], sems.at[s, 1]))

    @pl.when(jnp.logical_and(active, first_ref[i] == 1))
    def _():
        @pl.when(i == 0)
        def _():
            for cp in fetch(be_ref[0], 0):
                cp.start()

        for cp in fetch(be_ref[i], slot):
            cp.wait()
        wgu_bf[...] = wgu_buf[slot].astype(BF16)
        wdn_bf[...] = wdn_buf[slot].astype(BF16)

        @pl.when(next_ref[i] >= 0)
        def _():
            for cp in fetch(next_ref[i], 1 - slot):
                cp.start()

    @pl.when(active)
    def _():
        de = wdn_bf.shape[0]
        half = x_ref.shape[1]
        xa, xb = _unpack_bf16_pairs(x_ref[...])
        gu = (_dot(xa.astype(BF16), wgu_bf[:half, :]) + _dot(xb.astype(BF16), wgu_bf[half:, :]) + bgu_ref[...])
        g = jnp.minimum(gu[:, :de], SWIGLU_LIMIT)
        u = jnp.clip(gu[:, de:], -SWIGLU_LIMIT, SWIGLU_LIMIT)
        a = (u + 1.0) * g * _sigmoid(SWIGLU_ALPHA * g)
        y_ref[...] = _pack_bf16_pairs(_dot(a.astype(BF16), wdn_bf[...]) + bdn_ref[...])


def _experts(xb, block_expert, nb_used, grp_first, grp_slot, grp_next, w_gu, b_gu, w_dn, b_dn, layer):
    nrows, half = xb.shape
    bm = MOE_BLOCK
    depth, ne, d, two_de = w_gu.shape
    de = two_de // 2
    nsp = 5
    grid_spec = pltpu.PrefetchScalarGridSpec(
        num_scalar_prefetch=nsp,
        grid=(nrows // bm,),
        in_specs=[pl.BlockSpec((bm, half), lambda i, be, *_: (i, 0)),
                  pl.BlockSpec(memory_space=pl.ANY),
                  pl.BlockSpec((None, None, 1, two_de), lambda i, be, *_: (layer, be[i], 0, 0)),
                  pl.BlockSpec(memory_space=pl.ANY),
                  pl.BlockSpec((None, None, 1, d), lambda i, be, *_: (layer, be[i], 0, 0))],
        out_specs=pl.BlockSpec((bm, d // 2), lambda i, be, *_: (i, 0)),
        scratch_shapes=[pltpu.VMEM((2, d, two_de), F32), pltpu.VMEM((2, de, d), F32),
                        pltpu.VMEM((d, two_de), BF16), pltpu.VMEM((de, d), BF16),
                        pltpu.SemaphoreType.DMA((2, 2))],
    )
    return pl.pallas_call(
        functools.partial(_expert_kernel, layer=layer),
        grid_spec=grid_spec,
        out_shape=jax.ShapeDtypeStruct((nrows, d // 2), jnp.int32),
        compiler_params=_params(("arbitrary",)),
        name="moe_expert_ffn",
    )(block_expert, nb_used, grp_first, grp_slot, grp_next, xb, w_gu, b_gu.reshape(depth, ne, 1, two_de),
      w_dn, b_dn.reshape(depth, ne, 1, d))


def _sc_gather_rows(table, idx):
    n = idx.shape[0]
    width = table.shape[1]
    nw = SC_CORES * SC_SUBCORES
    nb = SC_GATHER_ROWS
    per_w = n // nw
    assert n % nw == 0 and per_w % nb == 0, (n, nw, nb)
    steps = per_w // nb
    mesh = plsc.VectorSubcoreMesh(core_axis_name="c", subcore_axis_name="s",
                                  num_cores=SC_CORES, num_subcores=SC_SUBCORES)

    def body(table_hbm, idx_hbm, out_hbm, idx_v, rows_v, sem):
        wid = lax.axis_index("s") * SC_CORES + lax.axis_index("c")
        base = wid * per_w

        @pl.loop(0, steps)
        def _(j):
            off = pl.multiple_of(base + j * nb, 8)
            pltpu.sync_copy(idx_hbm.at[pl.ds(off, nb)], idx_v)
            pltpu.async_copy(table_hbm.at[idx_v], rows_v, sem).wait()
            pltpu.sync_copy(rows_v, out_hbm.at[pl.ds(off, nb)])

    return pl.kernel(
        body,
        out_type=jax.ShapeDtypeStruct((n, width), table.dtype),
        mesh=mesh,
        scratch_types=[pltpu.VMEM((nb,), jnp.int32),
                       pltpu.VMEM((nb, width), table.dtype),
                       pltpu.SemaphoreType.DMA],
        name="sc_gather_rows",
    )(table, idx)


def _combine_kernel(xs_ref, yg_ref, gt_ref, gate_ref, fg_ref, o_ref, *, final):
    f_hi = f_lo = None
    for k in range(TOP_K):
        y_hi, y_lo = _unpack_bf16_pairs(yg_ref[k])
        gk = gt_ref[:, k:k + 1]
        f_hi = y_hi * gk if f_hi is None else f_hi + y_hi * gk
        f_lo = y_lo * gk if f_lo is None else f_lo + y_lo * gk
    out = xs_ref[...] + gate_ref[...] * jnp.concatenate([f_hi, f_lo], axis=1)
    if final:
        out = _rms(out) * fg_ref[...]
    o_ref[...] = out


def _combine(xs, yg, gate_t, mods_l, final_g, final, dims):
    bsz, seq, ctx = dims
    r, d = xs.shape
    tm = ROW_TILE
    lt = seq // tm
    return pl.pallas_call(
        functools.partial(_combine_kernel, final=final),
        grid=(r // tm,),
        in_specs=[pl.BlockSpec((tm, d), lambda i: (i, 0)),
                  pl.BlockSpec((TOP_K, tm, d // 2), lambda i: (0, i, 0)),
                  pl.BlockSpec((tm, TOP_K), lambda i: (i, 0)),
                  _mod_spec(5, lt, bsz, d),
                  pl.BlockSpec((1, d), lambda i: (0, 0))],
        out_specs=pl.BlockSpec((tm, d), lambda i: (i, 0)),
        out_shape=jax.ShapeDtypeStruct((r, d), F32),
        compiler_params=_params(("arbitrary",)),
        name="moe_combine_residual",
    )(xs, yg, gate_t, mods_l, final_g.reshape(1, d))


def _moe_layer(xs, mods_l, g2, w_router, b_router, w_gu, b_gu, w_dn, b_dn, final_g, layer, final, dims):
    r, d = xs.shape
    bm = MOE_BLOCK
    hp, top_idx, gate, rank, cnt = _router(xs, g2, mods_l, w_router, b_router, dims)
    counts = cnt[:, 0]
    padded = (counts + bm - 1) // bm * bm
    pad_ends = jnp.cumsum(padded)
    pad_starts = pad_ends - padded
    eids = jnp.arange(N_EXPERTS, dtype=jnp.int32)
    onehot = top_idx[None] == eids[:, None, None]
    dest = jnp.sum(jnp.where(onehot, pad_starts[:, None, None], 0), axis=0) + rank
    n_blocks = -(-(r * TOP_K + N_EXPERTS * (bm - 1)) // bm)
    blk_start = jnp.arange(n_blocks, dtype=jnp.int32) * bm
    block_expert = jnp.minimum(jnp.sum((pad_ends[None, :] <= blk_start[:, None]).astype(jnp.int32), axis=1),
                               N_EXPERTS - 1)
    nb_total = pad_ends[-1] // bm
    nb_used = nb_total.astype(jnp.int32).reshape(1)
    blk = jnp.arange(n_blocks, dtype=jnp.int32)
    prev_e = jnp.concatenate([jnp.full((1,), -1, jnp.int32), block_expert[:-1]])
    grp_first = ((block_expert != prev_e) & (blk < nb_total)).astype(jnp.int32)
    grp_slot = (jnp.cumsum(grp_first) - 1) % 2
    later = (eids[None, :] > eids[:, None]) & (counts[None, :] > 0)
    next_of_e = jnp.min(jnp.where(later, eids[None, :], N_EXPERTS), axis=1)
    next_of_e = jnp.where(next_of_e == N_EXPERTS, -1, next_of_e)
    grp_next = jnp.sum(jnp.where(block_expert[:, None] == eids[None, :], next_of_e[None, :], 0), axis=1)
    tok = jnp.broadcast_to(jnp.arange(r, dtype=jnp.int32), (TOP_K, r))
    row_tok = jnp.zeros((n_blocks * bm,), jnp.int32).at[dest.reshape(-1)].set(tok.reshape(-1))
    xb = _sc_gather_rows(hp, row_tok)
    yb = _experts(xb, block_expert, nb_used, grp_first, grp_slot.astype(jnp.int32), grp_next.astype(jnp.int32),
                  w_gu, b_gu, w_dn, b_dn, layer)
    yg = _sc_gather_rows(yb, dest.reshape(-1)).reshape(TOP_K, r, d // 2)
    return _combine(xs, yg, gate.T, mods_l, final_g, final, dims)


def kernel(x, c, ctx, c_ctx, ada_w, ada_b, norm1_g, norm2_g, ssd_w_in, ssd_conv_w, ssd_conv_b, ssd_dt_bias,
           ssd_a_log, ssd_d, ssd_norm_w, ssd_w_out, diff_w_qkv, diff_lam, diff_subln_w, diff_w_out, ml_w_in,
           ml_b_gates, ml_norm_w, ml_w_out, moe_w_router, moe_b_router, moe_w_gu, moe_b_gu, moe_w_dn, moe_b_dn,
           final_g):
    bsz, seq, d = x.shape
    n_ctx = ctx.shape[1]
    depth = ada_w.shape[0]
    dims = (bsz, seq, n_ctx)
    n_lat = bsz * seq
    xs = jnp.concatenate([x.reshape(n_lat, d), ctx.reshape(bsz * n_ctx, d)], axis=0)
    cond_rows = jnp.zeros((8, d), F32).at[:bsz].set(c).at[bsz].set(c_ctx)
    mods = _mods(cond_rows, ada_w, ada_b)
    mods = mods[:, :, :bsz + 1].reshape(depth, N_MOD, bsz + 1, 1, d)
    for i in range(depth):
        mods_l = mods[i]
        kind, j = i % 3, i // 3
        if kind == 0:
            xs = _ssd_layer(xs, mods_l, norm1_g[i], ssd_w_in[j], ssd_conv_w[j], ssd_conv_b[j], ssd_dt_bias[j],
                            ssd_a_log[j], ssd_d[j], ssd_norm_w[j], ssd_w_out[j], dims)
        elif kind == 1:
            lambda_init = 0.8 - 0.6 * math.exp(-0.3 * i)
            xs = _diff_layer(xs, mods_l, norm1_g[i], diff_w_qkv[j], diff_lam[j], diff_subln_w[j], diff_w_out[j],
                             lambda_init, dims)
        else:
            xs = _mlstm_layer(xs, mods_l, norm1_g[i], ml_w_in[j], ml_b_gates[j], ml_norm_w[j], ml_w_out[j], dims)
        xs = _moe_layer(xs, mods_l, norm2_g[i], moe_w_router[i], moe_b_router[i], moe_w_gu, moe_b_gu,
                        moe_w_dn, moe_b_dn, final_g, i, i == depth - 1, dims)
    return xs[:n_lat].reshape(bsz, seq, d)
```

```python
import functools
import math

import numpy as np
import jax
import jax.numpy as jnp
from jax import lax
from jax.experimental import pallas as pl
from jax.experimental.pallas import tpu as pltpu
from jax.experimental.pallas import tpu_sc as plsc

F32 = jnp.float32
BF16 = jnp.bfloat16
HIGHEST = lax.Precision.HIGHEST

GRID_W = 64
RMS_EPS = 1e-6
N_MOD = 6
SSD_HEAD_DIM = 64
SSD_HEADS = 32
SSD_GROUPS = 4
SSD_STATE = 128
SSD_INNER = SSD_HEADS * SSD_HEAD_DIM
SSD_BC = SSD_GROUPS * SSD_STATE
SSD_CONV_DIM = SSD_INNER + 2 * SSD_BC
DIFF_HEADS = 8
DIFF_HEAD_DIM = 64
DIFF_V_DIM = 128
ROPE_BASE = 10000.0
ROPE_Q = DIFF_HEAD_DIM // 4
ML_HEADS = 4
ML_QK_DIM = 128
ML_V_DIM = 256
ML_QK = ML_HEADS * ML_QK_DIM
ML_V = ML_HEADS * ML_V_DIM
N_EXPERTS = 32
TOP_K = 4
SWIGLU_LIMIT = 7.0
SWIGLU_ALPHA = 1.702
LOG2_E = 1.4426950408889634

LANES = 128
CHUNK = 128
ROW_TILE = 256
MOE_BLOCK = 256
ATTN_Q_TILE = 256
ATTN_UNROLL = 16
SC_CORES = 2
SC_SUBCORES = 16
SC_LANES = 16
SC_GATHER_ROWS = 64
VMEM_LIMIT = 56 * 1024 * 1024


def _params(sem):
    return pltpu.CompilerParams(dimension_semantics=sem, vmem_limit_bytes=VMEM_LIMIT)


def _dot(a, b):
    return jnp.dot(a, b, preferred_element_type=F32)


def _dot_nt(a, b):
    return lax.dot_general(a, b, (((1,), (1,)), ((), ())), preferred_element_type=F32)


def _dot_tn(a, b):
    return lax.dot_general(a, b, (((0,), (0,)), ((), ())), preferred_element_type=F32)


def _rms(x):
    return x * lax.rsqrt(jnp.mean(x * x, axis=-1, keepdims=True) + RMS_EPS)


def _sigmoid(x):
    return 1.0 / (1.0 + jnp.exp(-x))


def _silu(x):
    return x * _sigmoid(x)


def _softplus(x):
    return jnp.maximum(x, 0.0) + jnp.log(1.0 + jnp.exp(-jnp.abs(x)))


def _log_sigmoid(x):
    return -_softplus(-x)


def _mod_spec(which, lat_tiles_per_batch, bsz, d):
    return pl.BlockSpec((None, None, 1, d),
                        lambda i: (which, jnp.minimum(i // lat_tiles_per_batch, bsz), 0, 0))


def _mods_kernel(c_ref, w_ref, b_ref, o_ref):
    c = c_ref[...]
    cond = _silu(c)
    o_ref[...] = jnp.dot(cond, w_ref[...], precision=HIGHEST, preferred_element_type=F32) + b_ref[...]


def _mods(cond_rows, ada_w, ada_b):
    depth, d, _ = ada_w.shape
    nr = cond_rows.shape[0]
    return pl.pallas_call(
        _mods_kernel,
        grid=(depth, N_MOD),
        in_specs=[pl.BlockSpec((nr, d), lambda l, j: (0, 0)),
                  pl.BlockSpec((None, d, d), lambda l, j: (l, 0, j)),
                  pl.BlockSpec((None, 1, d), lambda l, j: (l, 0, j))],
        out_specs=pl.BlockSpec((None, None, nr, d), lambda l, j: (l, j, 0, 0)),
        out_shape=jax.ShapeDtypeStruct((depth, N_MOD, nr, d), F32),
        compiler_params=_params(("arbitrary", "arbitrary")),
        name="adaln_mods",
    )(cond_rows, ada_w, ada_b.reshape(depth, 1, N_MOD * d))


def _norm_mod(x, g, sh, sc):
    return (_rms(x) * g) * (1.0 + sc) + sh


def _inproj_kernel(x_ref, g_ref, sh_ref, sc_ref, w_ref, w2_ref, o_ref, o2_ref, *, n_chunk):
    hb = _norm_mod(x_ref[...], g_ref[...], sh_ref[...], sc_ref[...]).astype(BF16)
    n = o_ref.shape[1]
    for n0 in range(0, n, n_chunk):
        o_ref[:, n0:n0 + n_chunk] = _dot(hb, w_ref[:, n0:n0 + n_chunk])
    o2_ref[...] = _dot(hb, w2_ref[...])


def _col_chunk(n):
    for c in (512, 384, 256, 128):
        if n % c == 0:
            return c
    raise ValueError(n)


def _inproj(xs, g, mods_l, w_bf16, w2_bf16, dims):
    bsz, seq, ctx = dims
    r, d = xs.shape
    n = w_bf16.shape[1]
    n2 = w2_bf16.shape[1]
    tm = ROW_TILE
    lt = seq // tm
    return pl.pallas_call(
        functools.partial(_inproj_kernel, n_chunk=_col_chunk(n)),
        grid=(r // tm,),
        in_specs=[pl.BlockSpec((tm, d), lambda i: (i, 0)),
                  pl.BlockSpec((1, d), lambda i: (0, 0)),
                  _mod_spec(0, lt, bsz, d), _mod_spec(1, lt, bsz, d),
                  pl.BlockSpec((d, n), lambda i: (0, 0)),
                  pl.BlockSpec((d, n2), lambda i: (0, 0))],
        out_specs=[pl.BlockSpec((tm, n), lambda i: (i, 0)), pl.BlockSpec((tm, n2), lambda i: (i, 0))],
        out_shape=[jax.ShapeDtypeStruct((r, n), F32), jax.ShapeDtypeStruct((r, n2), F32)],
        compiler_params=_params(("arbitrary",)),
        name="norm_mod_inproj",
    )(xs, g.reshape(1, d), mods_l, mods_l, w_bf16, w2_bf16)


def _conv_kernel(first_ref, last_ref, cur_ref, prev_ref, nxt_ref, w_ref, b_ref, o_ref):
    i = pl.program_id(0)
    cur = cur_ref[...]
    tm = cur.shape[0]
    row = lax.broadcasted_iota(jnp.int32, cur.shape, 0)
    prev_row = jnp.where(first_ref[i] == 1, 0.0, prev_ref[7:8, :])
    next_row = jnp.where(last_ref[i] == 1, 0.0, nxt_ref[0:1, :])
    down = jnp.where(row == 0, prev_row, pltpu.roll(cur, 1, 0))
    up = jnp.where(row == tm - 1, next_row, pltpu.roll(cur, tm - 1, 0))
    y = w_ref[0:1, :] * down + w_ref[1:2, :] * cur + w_ref[2:3, :] * up + b_ref[...]
    o_ref[...] = _silu(y)


def _ssd_conv(t_main, conv_w, conv_b, dims):
    bsz, seq, ctx = dims
    r = t_main.shape[0]
    tm = ROW_TILE
    cw = 1024
    ncol = SSD_CONV_DIM // cw
    off = SSD_INNER // cw
    starts = [b * seq for b in range(bsz)] + [bsz * seq + b * ctx for b in range(bsz)]
    ends = [s + (seq if k < bsz else ctx) for k, s in enumerate(starts)]
    first = np.array([1 if (i * tm) in starts else 0 for i in range(r // tm)], np.int32)
    last = np.array([1 if ((i + 1) * tm) in ends else 0 for i in range(r // tm)], np.int32)
    sub = tm // 8
    nblk8 = r // 8
    grid_spec = pltpu.PrefetchScalarGridSpec(
        num_scalar_prefetch=2,
        grid=(r // tm, ncol),
        in_specs=[pl.BlockSpec((tm, cw), lambda i, j, f, l: (i, off + j)),
                  pl.BlockSpec((8, cw), lambda i, j, f, l: (jnp.maximum(i * sub - 1, 0), off + j)),
                  pl.BlockSpec((8, cw), lambda i, j, f, l: (jnp.minimum((i + 1) * sub, nblk8 - 1), off + j)),
                  pl.BlockSpec((3, cw), lambda i, j, f, l: (0, j)),
                  pl.BlockSpec((1, cw), lambda i, j, f, l: (0, j))],
        out_specs=pl.BlockSpec((tm, cw), lambda i, j, f, l: (i, j)),
    )
    return pl.pallas_call(
        _conv_kernel,
        grid_spec=grid_spec,
        out_shape=jax.ShapeDtypeStruct((r, SSD_CONV_DIM), F32),
        compiler_params=_params(("arbitrary", "arbitrary")),
        name="ssd_conv_silu",
    )(jnp.asarray(first), jnp.asarray(last), t_main, t_main, t_main, conv_w, conv_b.reshape(1, -1))


def _chunk_block(b, d, s, bsz, seq, ctx):
    nc_ctx = ctx // CHUNK
    nc_lat = seq // CHUNK
    in_ctx = s < nc_ctx
    pc = jnp.where(d == 0, s, nc_ctx - 1 - s)
    pls = s - nc_ctx
    plat = jnp.where(d == 0, pls, nc_lat - 1 - pls)
    ctx_blk = bsz * nc_lat + b * nc_ctx + pc
    lat_blk = b * nc_lat + plat
    return jnp.where(in_ctx, ctx_blk, lat_blk)


def _dir_masks(d):
    ii = lax.broadcasted_iota(jnp.int32, (CHUNK, CHUNK), 0)
    jj = lax.broadcasted_iota(jnp.int32, (CHUNK, CHUNK), 1)
    sign = 1 - 2 * d
    return (ii - jj) * sign >= 0


def _ssd_scan_kernel(x_ref, b_ref, c_ref, dt_ref, bias_ref, a_ref, y_ref, st_ref):
    d = pl.program_id(1)
    s = pl.program_id(2)

    @pl.when(s == 0)
    def _():
        st_ref[...] = jnp.zeros_like(st_ref)

    mask = _dir_masks(d)
    tri = jnp.where(mask, 1.0, 0.0).astype(F32)
    dt = _softplus(dt_ref[...] + bias_ref[...])
    dta = dt * a_ref[...]
    la = jnp.dot(tri, dta, precision=HIGHEST, preferred_element_type=F32)
    total = jnp.sum(dta, axis=0, keepdims=True)
    la_t = la.T
    dt_t = dt.T
    ws_t = (jnp.exp(total - la) * dt).T
    e_tot = jnp.exp(total)
    lane = lax.broadcasted_iota(jnp.int32, (CHUNK, LANES), 1)
    lo_half = lane < SSD_HEAD_DIM
    hg = SSD_HEADS // SSD_GROUPS
    for g in range(SSD_GROUPS):
        bg32 = b_ref[:, g * SSD_STATE:(g + 1) * SSD_STATE]
        cg = c_ref[:, g * SSD_STATE:(g + 1) * SSD_STATE].astype(BF16)
        cb = _dot_nt(cg, bg32.astype(BF16))
        bg_t = bg32.T
        for pr in range(hg // 2):
            h0 = g * hg + 2 * pr
            col0 = slice((h0 * SSD_HEAD_DIM), (h0 + 2) * SSD_HEAD_DIM)
            xpb = x_ref[:, col0].astype(BF16)
            ys, sts, la_cols = [], [], []
            for h in (h0, h0 + 1):
                la_col = jnp.broadcast_to(la[:, h:h + 1], (CHUNK, LANES))
                la_cols.append(la_col)
                w = jnp.exp(jnp.where(mask, la_col - la_t[h:h + 1, :], -jnp.inf)) * cb * dt_t[h:h + 1, :]
                ys.append(_dot(w.astype(BF16), xpb))
                sts.append(_dot((bg_t * ws_t[h:h + 1, :]).astype(BF16), xpb))
            st = st_ref[g, :, pr * LANES:(pr + 1) * LANES]
            e_pair = jnp.exp(jnp.where(lo_half, la_cols[0], la_cols[1]))
            y_ref[:, col0] = jnp.where(lo_half, ys[0], ys[1]) + e_pair * _dot(cg, st.astype(BF16))
            et_pair = jnp.where(lo_half[0:1], e_tot[:, h0:h0 + 1], e_tot[:, h0 + 1:h0 + 2])
            st_ref[g, :, pr * LANES:(pr + 1) * LANES] = et_pair * st + jnp.where(lo_half, sts[0], sts[1])


def _ssd_scan(xbc, dt_raw, dt_bias_pad, a_pad, dims):
    bsz, seq, ctx = dims
    r = xbc.shape[0]
    nsteps = (seq + ctx) // CHUNK
    blk = functools.partial(_chunk_block, bsz=bsz, seq=seq, ctx=ctx)
    xw = SSD_INNER
    return pl.pallas_call(
        _ssd_scan_kernel,
        grid=(bsz, 2, nsteps),
        in_specs=[pl.BlockSpec((CHUNK, xw), lambda b, d, s: (blk(b, d, s), 0)),
                  pl.BlockSpec((CHUNK, SSD_BC), lambda b, d, s: (blk(b, d, s), xw // SSD_BC)),
                  pl.BlockSpec((CHUNK, SSD_BC), lambda b, d, s: (blk(b, d, s), xw // SSD_BC + 1)),
                  pl.BlockSpec((CHUNK, LANES), lambda b, d, s: (blk(b, d, s), d)),
                  pl.BlockSpec((None, 1, LANES), lambda b, d, s: (d, 0, 0)),
                  pl.BlockSpec((None, 1, LANES), lambda b, d, s: (d, 0, 0))],
        out_specs=pl.BlockSpec((None, CHUNK, xw), lambda b, d, s: (d, blk(b, d, s), 0)),
        out_shape=jax.ShapeDtypeStruct((2, r, xw), F32),
        scratch_shapes=[pltpu.VMEM((SSD_GROUPS, SSD_STATE, xw // SSD_GROUPS), F32)],
        compiler_params=_params(("arbitrary", "arbitrary", "arbitrary")),
        name="ssd_scan",
    )(xbc, xbc, xbc, dt_raw, dt_bias_pad, a_pad)


def _ssd_out_kernel(yf_ref, yb_ref, xc_ref, z_ref, dexp_ref, nw_ref, w_ref, xs_ref, gate_ref, o_ref):
    y = yf_ref[...] + yb_ref[...] + dexp_ref[...] * xc_ref[...]
    y = y * _silu(z_ref[...])
    a = (_rms(y) * nw_ref[...]).astype(BF16)
    o_ref[...] = xs_ref[...] + gate_ref[...] * _dot(a, w_ref[...])


def _ssd_out(y2, xbc, t_main, d_exp, norm_w, w_out_bf16, xs, mods_l, dims):
    bsz, seq, ctx = dims
    r, d = xs.shape
    tm = ROW_TILE
    lt = seq // tm
    xw = SSD_INNER
    return pl.pallas_call(
        _ssd_out_kernel,
        grid=(r // tm,),
        in_specs=[pl.BlockSpec((None, tm, xw), lambda i: (0, i, 0)),
                  pl.BlockSpec((None, tm, xw), lambda i: (1, i, 0)),
                  pl.BlockSpec((tm, xw), lambda i: (i, 0)),
                  pl.BlockSpec((tm, xw), lambda i: (i, 0)),
                  pl.BlockSpec((1, xw), lambda i: (0, 0)),
                  pl.BlockSpec((1, xw), lambda i: (0, 0)),
                  pl.BlockSpec((xw, d), lambda i: (0, 0)),
                  pl.BlockSpec((tm, d), lambda i: (i, 0)),
                  _mod_spec(2, lt, bsz, d)],
        out_specs=pl.BlockSpec((tm, d), lambda i: (i, 0)),
        out_shape=jax.ShapeDtypeStruct((r, d), F32),
        compiler_params=_params(("arbitrary",)),
        name="ssd_gated_norm_outproj",
    )(y2, y2, xbc, t_main, d_exp, norm_w.reshape(1, xw), w_out_bf16, xs, mods_l)


def _ssd_layer(xs, mods_l, g1, w_in, conv_w, conv_b, dt_bias, a_log, d_skip, norm_w, w_out, dims):
    d = xs.shape[1]
    main = SSD_INNER + SSD_CONV_DIM
    w_main = w_in[:, :main].astype(BF16)
    w_dt = jnp.zeros((d, 2 * LANES), F32)
    w_dt = w_dt.at[:, :SSD_HEADS].set(w_in[:, main:main + SSD_HEADS])
    w_dt = w_dt.at[:, LANES:LANES + SSD_HEADS].set(w_in[:, main + SSD_HEADS:]).astype(BF16)
    t_main, dt_raw = _inproj(xs, g1, mods_l, w_main, w_dt, dims)
    xbc = _ssd_conv(t_main, conv_w, conv_b, dims)
    pad = jnp.zeros((2, 1, LANES - SSD_HEADS), F32)
    bias_pad = jnp.concatenate([dt_bias.astype(F32).reshape(2, 1, SSD_HEADS), pad], axis=-1)
    a_pad = jnp.concatenate([-jnp.exp(a_log.astype(F32)).reshape(2, 1, SSD_HEADS), pad], axis=-1)
    y2 = _ssd_scan(xbc, dt_raw, bias_pad, a_pad, dims)
    d_exp = jnp.repeat(d_skip.astype(F32), SSD_HEAD_DIM).reshape(1, SSD_INNER)
    return _ssd_out(y2, xbc, t_main, d_exp, norm_w, w_out.astype(BF16), xs, mods_l, dims)


def _qkv_rope_kernel(x_ref, g_ref, sh_ref, sc_ref, w_ref, cos_ref, sin_ref, q_ref, k_ref, v_ref, *, n_lat_tiles):
    i = pl.program_id(0)
    hb = _norm_mod(x_ref[...], g_ref[...], sh_ref[...], sc_ref[...]).astype(BF16)
    d = x_ref.shape[1]
    is_ctx = i >= n_lat_tiles
    cos = jnp.where(is_ctx, 1.0, cos_ref[...])
    sin = jnp.where(is_ctx, 0.0, sin_ref[...])
    nrep = d // LANES
    cos = jnp.concatenate([cos] * nrep, axis=1)
    sin = jnp.concatenate([sin] * nrep, axis=1)

    def mm(c):
        return _dot(hb, w_ref[:, c * d:(c + 1) * d])

    q = mm(0) * cos + mm(3) * sin
    q_ref[...] = (q * (DIFF_HEAD_DIM ** -0.5 * LOG2_E)).astype(BF16)
    k_ref[...] = (mm(1) * cos + mm(4) * sin).astype(BF16)
    v_ref[...] = mm(2).astype(BF16)


def _rope_tables(seq):
    rows = seq // GRID_W
    row = jnp.repeat(jnp.arange(rows, dtype=F32), GRID_W)
    col = jnp.tile(jnp.arange(GRID_W, dtype=F32), rows)
    inv = ROPE_BASE ** (-jnp.arange(ROPE_Q, dtype=F32) / ROPE_Q)
    ang_r = row[:, None] * inv
    ang_c = col[:, None] * inv
    cos = jnp.concatenate([jnp.cos(ang_r), jnp.cos(ang_r), jnp.cos(ang_c), jnp.cos(ang_c)], axis=1)
    sin = jnp.concatenate([-jnp.sin(ang_r), jnp.sin(ang_r), -jnp.sin(ang_c), jnp.sin(ang_c)], axis=1)
    return jnp.tile(cos, (1, 2)), jnp.tile(sin, (1, 2))


def _rope_partner_cols(d):
    col = np.arange(d)
    within = col % (2 * ROPE_Q)
    return np.where(within < ROPE_Q, col + ROPE_Q, col - ROPE_Q)


def _qkv_rope(xs, g, mods_l, w_qkv, dims):
    bsz, seq, ctx = dims
    r, d = xs.shape
    tm = ROW_TILE
    lt = seq // tm
    perm = _rope_partner_cols(d)
    wq, wk, wv = w_qkv[:, :d], w_qkv[:, d:2 * d], w_qkv[:, 2 * d:]
    w_all = jnp.concatenate([wq, wk, wv, wq[:, perm], wk[:, perm]], axis=1).astype(BF16)
    cos, sin = _rope_tables(seq)
    out = jax.ShapeDtypeStruct((r, d), BF16)
    row_spec = pl.BlockSpec((tm, d), lambda i: (i, 0))
    tab_spec = pl.BlockSpec((tm, LANES), lambda i: (i % lt, 0))
    return pl.pallas_call(
        functools.partial(_qkv_rope_kernel, n_lat_tiles=bsz * lt),
        grid=(r // tm,),
        in_specs=[row_spec, pl.BlockSpec((1, d), lambda i: (0, 0)),
                  _mod_spec(0, lt, bsz, d), _mod_spec(1, lt, bsz, d),
                  pl.BlockSpec((d, 5 * d), lambda i: (0, 0)), tab_spec, tab_spec],
        out_specs=[row_spec, row_spec, row_spec],
        out_shape=[out, out, out],
        compiler_params=_params(("arbitrary",)),
        name="norm_mod_qkv_rope",
    )(xs, g.reshape(1, d), mods_l, mods_l, w_all, cos, sin)


def _lane_fold(x, op):
    parts = [x[:, t * LANES:(t + 1) * LANES] for t in range(x.shape[1] // LANES)]
    return functools.reduce(op, parts)


def _attn_kernel(lam_ref, q_ref, kc_ref, vc_ref, kx_ref, vx_ref, nw_ref, o_ref,
                 sc_ref, sx_ref, acc_ref, m_ref, l_ref, *, n_lat_q, tk, unroll, lambda_init):
    i = pl.program_id(2)
    q = q_ref[...]
    tq = q.shape[0]
    lane = lax.broadcasted_iota(jnp.int32, q.shape, 1)
    zero = jnp.zeros_like(q)
    qq = jnp.concatenate([jnp.where(lane < DIFF_HEAD_DIM, q, zero),
                          jnp.where(lane >= DIFF_HEAD_DIM, q, zero)], axis=0)
    is_latent = i < n_lat_q
    n_x = kx_ref.shape[0] // tk

    s = _dot_nt(qq, kc_ref[...])
    sc_ref[...] = s
    m_ref[...] = _lane_fold(s, jnp.maximum)

    @pl.when(is_latent)
    def _():
        def p1(j, m_):
            kj = kx_ref[pl.ds(pl.multiple_of(j * tk, tk), tk), :]
            sj = _dot_nt(qq, kj)
            sx_ref[j] = sj
            return jnp.maximum(m_, _lane_fold(sj, jnp.maximum))

        m_ref[...] = lax.fori_loop(0, n_x, p1, m_ref[...], unroll=unroll)

    mrow = jnp.max(m_ref[...], axis=1, keepdims=True)

    p = jnp.exp2(sc_ref[...] - mrow)
    acc_ref[...] = _dot(p.astype(BF16), vc_ref[...])
    l_ref[...] = _lane_fold(p, jnp.add)

    @pl.when(is_latent)
    def _():
        def p2(j, l_):
            vj = vx_ref[pl.ds(pl.multiple_of(j * tk, tk), tk), :]
            pj = jnp.exp2(sx_ref[j] - mrow)
            acc_ref[...] += _dot(pj.astype(BF16), vj)
            return l_ + _lane_fold(pj, jnp.add)

        l_ref[...] = lax.fori_loop(0, n_x, p2, l_ref[...], unroll=unroll)

    on = acc_ref[...] / jnp.sum(l_ref[...], axis=1, keepdims=True)
    o = on[:tq] - lam_ref[0] * on[tq:]
    o = _rms(o) * nw_ref[...] * (1.0 - lambda_init)
    o_ref[...] = o.astype(o_ref.dtype)


def _diff_attention(q, k, v, lam_full, subln_w, lambda_init, dims):
    bsz, seq, ctx = dims
    r, d = q.shape
    tq = ATTN_Q_TILE
    tk = min(512, seq)
    n_lat_q = seq // tq
    n_ctx_q = ctx // tq
    nq = n_lat_q + n_ctx_q

    def q_idx(b, h, i):
        return (jnp.where(i < n_lat_q, b * n_lat_q + i, bsz * n_lat_q + b * n_ctx_q + (i - n_lat_q)), h)

    ctx_spec = pl.BlockSpec((ctx, LANES), lambda b, h, i: (bsz * seq // ctx + b, h))
    lat_spec = pl.BlockSpec((seq, LANES), lambda b, h, i: (b, h))
    return pl.pallas_call(
        functools.partial(_attn_kernel, n_lat_q=n_lat_q, tk=tk, unroll=min(ATTN_UNROLL, seq // tk),
                          lambda_init=lambda_init),
        grid=(bsz, DIFF_HEADS, nq),
        in_specs=[pl.BlockSpec(memory_space=pltpu.SMEM),
                  pl.BlockSpec((tq, LANES), q_idx),
                  ctx_spec, ctx_spec, lat_spec, lat_spec,
                  pl.BlockSpec((1, LANES), lambda b, h, i: (0, 0))],
        out_specs=pl.BlockSpec((tq, LANES), q_idx),
        out_shape=jax.ShapeDtypeStruct((r, d), BF16),
        scratch_shapes=[pltpu.VMEM((2 * tq, ctx), F32),
                        pltpu.VMEM((seq // tk, 2 * tq, tk), F32),
                        pltpu.VMEM((2 * tq, LANES), F32),
                        pltpu.VMEM((2 * tq, LANES), F32),
                        pltpu.VMEM((2 * tq, LANES), F32)],
        compiler_params=_params(("arbitrary", "arbitrary", "arbitrary")),
        name="diff_attention",
    )(lam_full, q, k, v, k, v, subln_w.reshape(1, LANES))


def _proj_res_kernel(a_ref, w_ref, xs_ref, gate_ref, o_ref):
    o_ref[...] = xs_ref[...] + gate_ref[...] * _dot(a_ref[...], w_ref[...])


def _proj_res(a_bf16, w_bf16, xs, mods_l, dims):
    bsz, seq, ctx = dims
    r, d = xs.shape
    kdim = a_bf16.shape[1]
    tm = ROW_TILE
    lt = seq // tm
    return pl.pallas_call(
        _proj_res_kernel,
        grid=(r // tm,),
        in_specs=[pl.BlockSpec((tm, kdim), lambda i: (i, 0)),
                  pl.BlockSpec((kdim, d), lambda i: (0, 0)),
                  pl.BlockSpec((tm, d), lambda i: (i, 0)),
                  _mod_spec(2, lt, bsz, d)],
        out_specs=pl.BlockSpec((tm, d), lambda i: (i, 0)),
        out_shape=jax.ShapeDtypeStruct((r, d), F32),
        compiler_params=_params(("arbitrary",)),
        name="outproj_gate_residual",
    )(a_bf16, w_bf16, xs, mods_l)


def _diff_layer(xs, mods_l, g1, w_qkv, lam, subln_w, w_out, lambda_init, dims):
    q, k, v = _qkv_rope(xs, g1, mods_l, w_qkv, dims)
    lam32 = lam.astype(F32)
    lam_full = (jnp.exp(jnp.sum(lam32[0] * lam32[1])) - jnp.exp(jnp.sum(lam32[2] * lam32[3]))
                + lambda_init).reshape(1)
    o = _diff_attention(q, k, v, lam_full, subln_w, lambda_init, dims)
    return _proj_res(o, w_out.astype(BF16), xs, mods_l, dims)


def _mlstm_scan_kernel(q_ref, k_ref, v_ref, g_ref, bias_ref, h_ref, c_ref, n_ref, m_ref):
    d = pl.program_id(1)
    s = pl.program_id(2)

    @pl.when(s == 0)
    def _():
        c_ref[...] = jnp.zeros_like(c_ref)
        n_ref[...] = jnp.zeros_like(n_ref)
        m_ref[...] = jnp.zeros_like(m_ref)

    mask = _dir_masks(d)
    tri = jnp.where(mask, 1.0, 0.0).astype(F32)
    gates = g_ref[...] + bias_ref[...]
    ig = gates[:, 0:ML_HEADS]
    lf = _log_sigmoid(gates)
    bcum = jnp.dot(tri, lf, precision=HIGHEST, preferred_element_type=F32)
    btot = jnp.sum(lf, axis=0, keepdims=True)
    bcum_t = bcum.T
    gates_t = gates.T
    row_last = jnp.where(d == 0, CHUNK - 1, 0)
    rsel = lax.broadcasted_iota(jnp.int32, (CHUNK, 1), 0) == row_last
    for h in range(ML_HEADS):
        fcol = ML_HEADS + h
        m_prev = m_ref[h:h + 1, 0:1]
        bcol = bcum[:, fcol:fcol + 1]
        brow = bcum_t[fcol:fcol + 1, :]
        irow = gates_t[h:h + 1, :]
        icol = ig[:, h:h + 1]
        gcol = bcol + m_prev
        dmat = jnp.where(mask, bcol - brow + irow, -jnp.inf)
        mt = jnp.maximum(gcol, jnp.max(dmat, axis=1, keepdims=True))
        q32 = q_ref[:, h * ML_QK_DIM:(h + 1) * ML_QK_DIM] * (ML_QK_DIM ** -0.5)
        qh = q32.astype(BF16)
        kh32 = k_ref[:, h * ML_QK_DIM:(h + 1) * ML_QK_DIM]
        kh = kh32.astype(BF16)
        vh = v_ref[:, h * ML_V_DIM:(h + 1) * ML_V_DIM].astype(BF16)
        sm = _dot_nt(qh, kh) * jnp.exp(dmat - mt)
        inter = jnp.exp(gcol - mt)
        cst = c_ref[h]
        nst = n_ref[h:h + 1, :]
        num = _dot(sm.astype(BF16), vh) + inter * _dot(qh, cst.astype(BF16))
        qn = jnp.sum(q32 * nst, axis=1, keepdims=True)
        den = jnp.sum(sm, axis=1, keepdims=True) + inter * qn
        h_ref[:, h * ML_V_DIM:(h + 1) * ML_V_DIM] = num / jnp.maximum(jnp.abs(den), jnp.exp(-mt))
        m_new = jnp.sum(jnp.where(rsel, mt, 0.0), axis=0, keepdims=True)
        btot_h = btot[:, fcol:fcol + 1]
        wk = jnp.exp(btot_h - bcol + icol - m_new)
        cscale = jnp.exp(btot_h + m_prev - m_new)
        kw = kh32 * wk
        c_ref[h] = cscale * cst + _dot_tn(kw.astype(BF16), vh)
        n_ref[h:h + 1, :] = cscale * nst + jnp.sum(kw, axis=0, keepdims=True)
        m_ref[h:h + 1, :] = jnp.broadcast_to(m_new, (1, LANES))


def _mlstm_scan(t_main, gates_raw, bias_pad, dims):
    bsz, seq, ctx = dims
    r = t_main.shape[0]
    nsteps = (seq + ctx) // CHUNK
    blk = functools.partial(_chunk_block, bsz=bsz, seq=seq, ctx=ctx)
    return pl.pallas_call(
        _mlstm_scan_kernel,
        grid=(bsz, 2, nsteps),
        in_specs=[pl.BlockSpec((CHUNK, ML_QK), lambda b, d, s: (blk(b, d, s), 0)),
                  pl.BlockSpec((CHUNK, ML_QK), lambda b, d, s: (blk(b, d, s), 1)),
                  pl.BlockSpec((CHUNK, ML_V), lambda b, d, s: (blk(b, d, s), 1)),
                  pl.BlockSpec((CHUNK, LANES), lambda b, d, s: (blk(b, d, s), d)),
                  pl.BlockSpec((None, 1, LANES), lambda b, d, s: (d, 0, 0))],
        out_specs=pl.BlockSpec((None, CHUNK, ML_V), lambda b, d, s: (d, blk(b, d, s), 0)),
        out_shape=jax.ShapeDtypeStruct((2, r, ML_V), F32),
        scratch_shapes=[pltpu.VMEM((ML_HEADS, ML_QK_DIM, ML_V_DIM), F32),
                        pltpu.VMEM((8, ML_QK_DIM), F32),
                        pltpu.VMEM((8, LANES), F32)],
        compiler_params=_params(("arbitrary", "arbitrary", "arbitrary")),
        name="mlstm_scan",
    )(t_main, t_main, t_main, gates_raw, bias_pad)


def _mlstm_out_kernel(hf_ref, hb_ref, o_ref_in, nw_ref, w_ref, xs_ref, gate_ref, out_ref):
    u = None
    for h in range(ML_HEADS):
        cs = slice(h * ML_V_DIM, (h + 1) * ML_V_DIM)
        a = _sigmoid(o_ref_in[:, cs]) * (hf_ref[:, cs] + hb_ref[:, cs])
        a = (_rms(a) * nw_ref[...]).astype(BF16)
        part = _dot(a, w_ref[cs, :])
        u = part if u is None else u + part
    out_ref[...] = xs_ref[...] + gate_ref[...] * u


def _mlstm_out(h2, t_main, norm_w, w_out_bf16, xs, mods_l, dims):
    bsz, seq, ctx = dims
    r, d = xs.shape
    tm = ROW_TILE
    lt = seq // tm
    o_blk = (2 * ML_QK + ML_V) // ML_V
    return pl.pallas_call(
        _mlstm_out_kernel,
        grid=(r // tm,),
        in_specs=[pl.BlockSpec((None, tm, ML_V), lambda i: (0, i, 0)),
                  pl.BlockSpec((None, tm, ML_V), lambda i: (1, i, 0)),
                  pl.BlockSpec((tm, ML_V), lambda i: (i, o_blk)),
                  pl.BlockSpec((1, ML_V_DIM), lambda i: (0, 0)),
                  pl.BlockSpec((ML_V, d), lambda i: (0, 0)),
                  pl.BlockSpec((tm, d), lambda i: (i, 0)),
                  _mod_spec(2, lt, bsz, d)],
        out_specs=pl.BlockSpec((tm, d), lambda i: (i, 0)),
        out_shape=jax.ShapeDtypeStruct((r, d), F32),
        compiler_params=_params(("arbitrary",)),
        name="mlstm_norm_outproj",
    )(h2, h2, t_main, norm_w.reshape(1, ML_V_DIM), w_out_bf16, xs, mods_l)


def _mlstm_layer(xs, mods_l, g1, w_in, b_gates, norm_w, w_out, dims):
    d = xs.shape[1]
    main = 2 * ML_QK + 2 * ML_V
    w_main = w_in[:, :main].astype(BF16)
    wg = w_in[:, main:].reshape(d, 4, ML_HEADS)
    w_g = jnp.zeros((d, 2 * LANES), F32)
    bias = jnp.zeros((2, 1, LANES), F32)
    for dr in range(2):
        w_g = w_g.at[:, dr * LANES:dr * LANES + 2 * ML_HEADS].set(
            wg[:, 2 * dr:2 * dr + 2].reshape(d, 2 * ML_HEADS))
        bias = bias.at[dr, 0, :2 * ML_HEADS].set(b_gates.astype(F32)[2 * dr:2 * dr + 2].reshape(2 * ML_HEADS))
    t_main, gates_raw = _inproj(xs, g1, mods_l, w_main, w_g.astype(BF16), dims)
    h2 = _mlstm_scan(t_main, gates_raw, bias, dims)
    return _mlstm_out(h2, t_main, norm_w, w_out.astype(BF16), xs, mods_l, dims)


def _router_kernel(x_ref, g_ref, sh_ref, sc_ref, wr_ref, br_ref, h_ref, idx_ref, gate_ref, rank_ref, cnt_ref,
                   carry_ref):
    i = pl.program_id(0)

    @pl.when(i == 0)
    def _():
        carry_ref[...] = jnp.zeros_like(carry_ref)

    h = _norm_mod(x_ref[...], g_ref[...], sh_ref[...], sc_ref[...])
    tm, d = h.shape
    h_ref[...] = _pack_bf16_pairs(h)
    logits = lax.dot_general(wr_ref[...], h, (((1,), (1,)), ((), ())), precision=HIGHEST,
                             preferred_element_type=F32) + br_ref[...]
    eidx = lax.broadcasted_iota(jnp.int32, logits.shape, 0)
    work = logits
    vals, idxs = [], []
    picked = jnp.zeros(logits.shape, F32)
    for _ in range(TOP_K):
        mx = jnp.max(work, axis=0, keepdims=True)
        ix = jnp.min(jnp.where(work == mx, eidx, N_EXPERTS), axis=0, keepdims=True)
        sel = eidx == ix
        vals.append(mx)
        idxs.append(ix)
        picked = jnp.where(sel, 1.0, picked)
        work = jnp.where(sel, -jnp.inf, work)
    es = [jnp.exp(v - vals[0]) for v in vals]
    tot = es[0] + es[1] + es[2] + es[3]
    jj = lax.broadcasted_iota(jnp.int32, (tm, tm), 0)
    ii = lax.broadcasted_iota(jnp.int32, (tm, tm), 1)
    upper = jnp.where(jj <= ii, 1.0, 0.0).astype(BF16)
    incl = _dot(picked.astype(BF16), upper)
    carry = carry_ref[:, 0:1]
    excl = incl - picked + carry
    for k in range(TOP_K):
        idx_ref[k:k + 1, :] = idxs[k]
        gate_ref[k:k + 1, :] = es[k] / tot
        rk = jnp.sum(jnp.where(eidx == idxs[k], excl, 0.0), axis=0, keepdims=True)
        rank_ref[k:k + 1, :] = rk.astype(jnp.int32)
    new_carry = carry + jnp.sum(picked, axis=1, keepdims=True)
    carry_ref[...] = jnp.broadcast_to(new_carry, carry_ref.shape)
    cnt_ref[...] = jnp.broadcast_to(new_carry, cnt_ref.shape).astype(jnp.int32)


def _router(xs, g2, mods_l, w_router, b_router, dims):
    bsz, seq, ctx = dims
    r, d = xs.shape
    tm = ROW_TILE
    lt = seq // tm
    tok_spec = pl.BlockSpec((TOP_K, tm), lambda i: (0, i))
    return pl.pallas_call(
        _router_kernel,
        grid=(r // tm,),
        in_specs=[pl.BlockSpec((tm, d), lambda i: (i, 0)),
                  pl.BlockSpec((1, d), lambda i: (0, 0)),
                  _mod_spec(3, lt, bsz, d), _mod_spec(4, lt, bsz, d),
                  pl.BlockSpec((N_EXPERTS, d), lambda i: (0, 0)),
                  pl.BlockSpec((N_EXPERTS, 1), lambda i: (0, 0))],
        out_specs=[pl.BlockSpec((tm, d // 2), lambda i: (i, 0)), tok_spec, tok_spec, tok_spec,
                   pl.BlockSpec((N_EXPERTS, LANES), lambda i: (0, 0))],
        out_shape=[jax.ShapeDtypeStruct((r, d // 2), jnp.int32),
                   jax.ShapeDtypeStruct((TOP_K, r), jnp.int32),
                   jax.ShapeDtypeStruct((TOP_K, r), F32),
                   jax.ShapeDtypeStruct((TOP_K, r), jnp.int32),
                   jax.ShapeDtypeStruct((N_EXPERTS, LANES), jnp.int32)],
        scratch_shapes=[pltpu.VMEM((N_EXPERTS, LANES), F32)],
        compiler_params=_params(("arbitrary",)),
        name="norm_mod_router_top4",
    )(xs, g2.reshape(1, d), mods_l, mods_l, w_router.T, b_router.reshape(N_EXPERTS, 1))


def _pack_bf16_pairs(x):
    w = x.shape[1]
    xr = x.astype(BF16).astype(F32)
    hi = lax.bitcast_convert_type(xr[:, :w // 2], jnp.int32)
    lo = lax.bitcast_convert_type(xr[:, w // 2:], jnp.int32)
    return hi | lax.shift_right_logical(lo, 16)


def _unpack_bf16_pairs(xp):
    hi = lax.bitcast_convert_type(xp & jnp.int32(-65536), F32)
    lo = lax.bitcast_convert_type(lax.shift_left(xp, 16), F32)
    return hi, lo


def _expert_kernel(be_ref, nb_ref, first_ref, slot_ref, next_ref, x_ref, wgu_hbm, bgu_ref, wdn_hbm, bdn_ref, y_ref,
                   wgu_buf, wdn_buf, wgu_bf, wdn_bf, sems, *, layer):
    i = pl.program_id(0)
    active = i < nb_ref[0]
    slot = slot_ref[i]

    def fetch(e, s):
        return (pltpu.make_async_copy(wgu_hbm.at[layer, e], wgu_buf.at[s], sems.at[s, 0]),
                pltpu.make_async_copy(wdn_hbm.at[layer, e], wdn_buf.at[s], sems.at[s, 1]))

    @pl.when(jnp.logical_and(active, first_ref[i] == 1))
    def _():
        @pl.when(i == 0)
        def _():
            for cp in fetch(be_ref[0], 0):
                cp.start()

        for cp in fetch(be_ref[i], slot):
            cp.wait()
        wgu_bf[...] = wgu_buf[slot].astype(BF16)
        wdn_bf[...] = wdn_buf[slot].astype(BF16)

        @pl.when(next_ref[i] >= 0)
        def _():
            for cp in fetch(next_ref[i], 1 - slot):
                cp.start()

    @pl.when(active)
    def _():
        de = wdn_bf.shape[0]
        half = x_ref.shape[1]
        xa, xb = _unpack_bf16_pairs(x_ref[...])
        gu = (_dot(xa.astype(BF16), wgu_bf[:half, :]) + _dot(xb.astype(BF16), wgu_bf[half:, :]) + bgu_ref[...])
        g = jnp.minimum(gu[:, :de], SWIGLU_LIMIT)
        u = jnp.clip(gu[:, de:], -SWIGLU_LIMIT, SWIGLU_LIMIT)
        a = (u + 1.0) * g * _sigmoid(SWIGLU_ALPHA * g)
        y_ref[...] = _pack_bf16_pairs(_dot(a.astype(BF16), wdn_bf[...]) + bdn_ref[...])


def _experts(xb, block_expert, nb_used, grp_first, grp_slot, grp_next, w_gu, b_gu, w_dn, b_dn, layer):
    nrows, half = xb.shape
    bm = MOE_BLOCK
    depth, ne, d, two_de = w_gu.shape
    de = two_de // 2
    nsp = 5
    grid_spec = pltpu.PrefetchScalarGridSpec(
        num_scalar_prefetch=nsp,
        grid=(nrows // bm,),
        in_specs=[pl.BlockSpec((bm, half), lambda i, be, *_: (i, 0)),
                  pl.BlockSpec(memory_space=pl.ANY),
                  pl.BlockSpec((None, None, 1, two_de), lambda i, be, *_: (layer, be[i], 0, 0)),
                  pl.BlockSpec(memory_space=pl.ANY),
                  pl.BlockSpec((None, None, 1, d), lambda i, be, *_: (layer, be[i], 0, 0))],
        out_specs=pl.BlockSpec((bm, d // 2), lambda i, be, *_: (i, 0)),
        scratch_shapes=[pltpu.VMEM((2, d, two_de), F32), pltpu.VMEM((2, de, d), F32),
                        pltpu.VMEM((d, two_de), BF16), pltpu.VMEM((de, d), BF16),
                        pltpu.SemaphoreType.DMA((2, 2))],
    )
    return pl.pallas_call(
        functools.partial(_expert_kernel, layer=layer),
        grid_spec=grid_spec,
        out_shape=jax.ShapeDtypeStruct((nrows, d // 2), jnp.int32),
        compiler_params=_params(("arbitrary",)),
        name="moe_expert_ffn",
    )(block_expert, nb_used, grp_first, grp_slot, grp_next, xb, w_gu, b_gu.reshape(depth, ne, 1, two_de),
      w_dn, b_dn.reshape(depth, ne, 1, d))


def _sc_row_tokens(dest, n_rows):
    top_k, r = dest.shape
    lanes = SC_LANES
    chunk = r // 2
    assert n_rows % lanes == 0 and r % 2 == 0 and chunk % lanes == 0
    fill_mask = (1 << (r.bit_length() - 1)) - 1
    mesh = plsc.VectorSubcoreMesh(core_axis_name="c", subcore_axis_name="s",
                                  num_cores=SC_CORES, num_subcores=SC_SUBCORES)

    def body(dest_hbm, out_hbm, dest_v, table_v):
        wid = lax.axis_index("s") * SC_CORES + lax.axis_index("c")

        @pl.when(wid == 0)
        def _():
            lane_id = lax.iota(jnp.int32, lanes)

            @pl.loop(0, n_rows // lanes)
            def _(c):
                table_v[pl.ds(c * lanes, lanes)] = (lane_id + c * lanes) & fill_mask

            for k in range(top_k):
                for half in range(2):
                    t0 = half * chunk
                    pltpu.sync_copy(dest_hbm.at[pl.ds(k * r + t0, chunk)], dest_v)

                    @pl.loop(0, chunk // lanes)
                    def _(v):
                        d = dest_v[pl.ds(v * lanes, lanes)]
                        plsc.store_scatter(table_v, [d], lane_id + (t0 + v * lanes))

            pltpu.sync_copy(table_v, out_hbm)

    cp = pltpu.CompilerParams(needs_layout_passes=False)
    return pl.kernel(
        body,
        out_type=jax.ShapeDtypeStruct((n_rows,), jnp.int32),
        mesh=mesh,
        scratch_types=[pltpu.VMEM((chunk,), jnp.int32), pltpu.VMEM((n_rows,), jnp.int32)],
        compiler_params=cp,
        name="sc_row_tokens",
    )(dest.reshape(top_k * r))


def _sc_gather_rows(table, idx):
    n = idx.shape[0]
    width = table.shape[1]
    nw = SC_CORES * SC_SUBCORES
    nb = SC_GATHER_ROWS
    per_w = n // nw
    assert n % nw == 0 and per_w % nb == 0, (n, nw, nb)
    steps = per_w // nb
    mesh = plsc.VectorSubcoreMesh(core_axis_name="c", subcore_axis_name="s",
                                  num_cores=SC_CORES, num_subcores=SC_SUBCORES)

    def body(table_hbm, idx_hbm, out_hbm, idx_a, idx_b, rows_a, rows_b, sem_a, sem_b):
        wid = lax.axis_index("s") * SC_CORES + lax.axis_index("c")
        base = wid * per_w
        slots = ((idx_a, rows_a, sem_a), (idx_b, rows_b, sem_b))

        def gather(j, slot):
            idx_v, rows_v, sem = slots[slot]
            return pltpu.make_async_copy(table_hbm.at[idx_v], rows_v, sem)

        def start(j, slot):
            off = pl.multiple_of(base + j * nb, 8)
            pltpu.sync_copy(idx_hbm.at[pl.ds(off, nb)], slots[slot][0])
            gather(j, slot).start()

        def finish(j, slot):
            off = pl.multiple_of(base + j * nb, 8)
            gather(j, slot).wait()
            pltpu.sync_copy(slots[slot][1], out_hbm.at[pl.ds(off, nb)])

        start(0, 0)

        @pl.loop(0, steps // 2)
        def _(p):
            j = 2 * p
            start(j + 1, 1)
            finish(j, 0)
            if steps % 2 == 1:
                start(j + 2, 0)
            else:
                @pl.when(j + 2 < steps)
                def _():
                    start(j + 2, 0)
            finish(j + 1, 1)

        if steps % 2 == 1:
            finish(steps - 1, 0)

    return pl.kernel(
        body,
        out_type=jax.ShapeDtypeStruct((n, width), table.dtype),
        mesh=mesh,
        scratch_types=[pltpu.VMEM((nb,), jnp.int32), pltpu.VMEM((nb,), jnp.int32),
                       pltpu.VMEM((nb, width), table.dtype), pltpu.VMEM((nb, width), table.dtype),
                       pltpu.SemaphoreType.DMA, pltpu.SemaphoreType.DMA],
        name="sc_gather_rows",
    )(table, idx)


def _combine_kernel(xs_ref, yg_ref, gt_ref, gate_ref, fg_ref, o_ref, *, final):
    f_hi = f_lo = None
    for k in range(TOP_K):
        y_hi, y_lo = _unpack_bf16_pairs(yg_ref[k])
        gk = gt_ref[:, k:k + 1]
        f_hi = y_hi * gk if f_hi is None else f_hi + y_hi * gk
        f_lo = y_lo * gk if f_lo is None else f_lo + y_lo * gk
    out = xs_ref[...] + gate_ref[...] * jnp.concatenate([f_hi, f_lo], axis=1)
    if final:
        out = _rms(out) * fg_ref[...]
    o_ref[...] = out


def _combine(xs, yg, gate_t, mods_l, final_g, final, dims):
    bsz, seq, ctx = dims
    r, d = xs.shape
    tm = ROW_TILE
    lt = seq // tm
    rows_out = bsz * seq if final else r
    return pl.pallas_call(
        functools.partial(_combine_kernel, final=final),
        grid=(rows_out // tm,),
        in_specs=[pl.BlockSpec((tm, d), lambda i: (i, 0)),
                  pl.BlockSpec((TOP_K, tm, d // 2), lambda i: (0, i, 0)),
                  pl.BlockSpec((tm, TOP_K), lambda i: (i, 0)),
                  _mod_spec(5, lt, bsz, d),
                  pl.BlockSpec((1, d), lambda i: (0, 0))],
        out_specs=pl.BlockSpec((tm, d), lambda i: (i, 0)),
        out_shape=jax.ShapeDtypeStruct((rows_out, d), F32),
        compiler_params=_params(("arbitrary",)),
        name="moe_combine_residual",
    )(xs, yg, gate_t, mods_l, final_g.reshape(1, d))


def _moe_layer(xs, mods_l, g2, w_router, b_router, w_gu, b_gu, w_dn, b_dn, final_g, layer, final, dims):
    r, d = xs.shape
    bm = MOE_BLOCK
    hp, top_idx, gate, rank, cnt = _router(xs, g2, mods_l, w_router, b_router, dims)
    counts = cnt[:, 0]
    padded = (counts + bm - 1) // bm * bm
    pad_ends = jnp.cumsum(padded)
    pad_starts = pad_ends - padded
    eids = jnp.arange(N_EXPERTS, dtype=jnp.int32)
    onehot = top_idx[None] == eids[:, None, None]
    dest = jnp.sum(jnp.where(onehot, pad_starts[:, None, None], 0), axis=0) + rank
    n_blocks = -(-(r * TOP_K + N_EXPERTS * (bm - 1)) // bm)
    blk_start = jnp.arange(n_blocks, dtype=jnp.int32) * bm
    block_expert = jnp.minimum(jnp.sum((pad_ends[None, :] <= blk_start[:, None]).astype(jnp.int32), axis=1),
                               N_EXPERTS - 1)
    nb_total = pad_ends[-1] // bm
    nb_used = nb_total.astype(jnp.int32).reshape(1)
    blk = jnp.arange(n_blocks, dtype=jnp.int32)
    prev_e = jnp.concatenate([jnp.full((1,), -1, jnp.int32), block_expert[:-1]])
    grp_first = ((block_expert != prev_e) & (blk < nb_total)).astype(jnp.int32)
    grp_slot = (jnp.cumsum(grp_first) - 1) % 2
    later = (eids[None, :] > eids[:, None]) & (counts[None, :] > 0)
    next_of_e = jnp.min(jnp.where(later, eids[None, :], N_EXPERTS), axis=1)
    next_of_e = jnp.where(next_of_e == N_EXPERTS, -1, next_of_e)
    grp_next = jnp.sum(jnp.where(block_expert[:, None] == eids[None, :], next_of_e[None, :], 0), axis=1)
    row_tok = _sc_row_tokens(dest, n_blocks * bm)
    xb = _sc_gather_rows(hp, row_tok)
    yb = _experts(xb, block_expert, nb_used, grp_first, grp_slot.astype(jnp.int32), grp_next.astype(jnp.int32),
                  w_gu, b_gu, w_dn, b_dn, layer)
    yg = _sc_gather_rows(yb, dest.reshape(-1)).reshape(TOP_K, r, d // 2)
    return _combine(xs, yg, gate.T, mods_l, final_g, final, dims)


def kernel(x, c, ctx, c_ctx, ada_w, ada_b, norm1_g, norm2_g, ssd_w_in, ssd_conv_w, ssd_conv_b, ssd_dt_bias,
           ssd_a_log, ssd_d, ssd_norm_w, ssd_w_out, diff_w_qkv, diff_lam, diff_subln_w, diff_w_out, ml_w_in,
           ml_b_gates, ml_norm_w, ml_w_out, moe_w_router, moe_b_router, moe_w_gu, moe_b_gu, moe_w_dn, moe_b_dn,
           final_g):
    bsz, seq, d = x.shape
    n_ctx = ctx.shape[1]
    depth = ada_w.shape[0]
    dims = (bsz, seq, n_ctx)
    n_lat = bsz * seq
    xs = jnp.concatenate([x.reshape(n_lat, d), ctx.reshape(bsz * n_ctx, d)], axis=0)
    cond_rows = jnp.zeros((8, d), F32).at[:bsz].set(c).at[bsz].set(c_ctx)
    mods = _mods(cond_rows, ada_w, ada_b)
    mods = mods[:, :, :bsz + 1].reshape(depth, N_MOD, bsz + 1, 1, d)
    for i in range(depth):
        mods_l = mods[i]
        kind, j = i % 3, i // 3
        if kind == 0:
            xs = _ssd_layer(xs, mods_l, norm1_g[i], ssd_w_in[j], ssd_conv_w[j], ssd_conv_b[j], ssd_dt_bias[j],
                            ssd_a_log[j], ssd_d[j], ssd_norm_w[j], ssd_w_out[j], dims)
        elif kind == 1:
            lambda_init = 0.8 - 0.6 * math.exp(-0.3 * i)
            xs = _diff_layer(xs, mods_l, norm1_g[i], diff_w_qkv[j], diff_lam[j], diff_subln_w[j], diff_w_out[j],
                             lambda_init, dims)
        else:
            xs = _mlstm_layer(xs, mods_l, norm1_g[i], ml_w_in[j], ml_b_gates[j], ml_norm_w[j], ml_w_out[j], dims)
        xs = _moe_layer(xs, mods_l, norm2_g[i], moe_w_router[i], moe_b_router[i], moe_w_gu, moe_b_gu,
                        moe_w_dn, moe_b_dn, final_g, i, i == depth - 1, dims)
    return xs.reshape(bsz, seq, d)
```

```python
import functools
import math

import numpy as np
import jax
import jax.numpy as jnp
from jax import lax
from jax.experimental import pallas as pl
from jax.experimental.pallas import tpu as pltpu
from jax.experimental.pallas import tpu_sc as plsc

F32 = jnp.float32
BF16 = jnp.bfloat16
HIGHEST = lax.Precision.HIGHEST

GRID_W = 64
RMS_EPS = 1e-6
N_MOD = 6
SSD_HEAD_DIM = 64
SSD_HEADS = 32
SSD_GROUPS = 4
SSD_STATE = 128
SSD_INNER = SSD_HEADS * SSD_HEAD_DIM
SSD_BC = SSD_GROUPS * SSD_STATE
SSD_CONV_DIM = SSD_INNER + 2 * SSD_BC
DIFF_HEADS = 8
DIFF_HEAD_DIM = 64
DIFF_V_DIM = 128
ROPE_BASE = 10000.0
ROPE_Q = DIFF_HEAD_DIM // 4
ML_HEADS = 4
ML_QK_DIM = 128
ML_V_DIM = 256
ML_QK = ML_HEADS * ML_QK_DIM
ML_V = ML_HEADS * ML_V_DIM
N_EXPERTS = 32
TOP_K = 4
SWIGLU_LIMIT = 7.0
SWIGLU_ALPHA = 1.702
LOG2_E = 1.4426950408889634

LANES = 128
CHUNK = 128
ROW_TILE = 256
ROW_TILE_WIDE = 512
MOE_BLOCK = 256
ATTN_Q_TILE = 256
ATTN_UNROLL = 16
SC_CORES = 2
SC_SUBCORES = 16
SC_LANES = 16
SC_GATHER_ROWS = 64
VMEM_LIMIT = 56 * 1024 * 1024


def _wide_tile(dims):
    bsz, seq, ctx = dims
    return min(ROW_TILE_WIDE, seq, bsz * ctx)


def _params(sem):
    return pltpu.CompilerParams(dimension_semantics=sem, vmem_limit_bytes=VMEM_LIMIT)


def _dot(a, b):
    return jnp.dot(a, b, preferred_element_type=F32)


def _dot_nt(a, b):
    return lax.dot_general(a, b, (((1,), (1,)), ((), ())), preferred_element_type=F32)


def _dot_tn(a, b):
    return lax.dot_general(a, b, (((0,), (0,)), ((), ())), preferred_element_type=F32)


def _rms(x):
    return x * lax.rsqrt(jnp.mean(x * x, axis=-1, keepdims=True) + RMS_EPS)


def _sigmoid(x):
    return 1.0 / (1.0 + jnp.exp(-x))


def _silu(x):
    return x * _sigmoid(x)


def _softplus(x):
    return jnp.maximum(x, 0.0) + jnp.log(1.0 + jnp.exp(-jnp.abs(x)))


def _log_sigmoid(x):
    return -_softplus(-x)


def _mod_spec(which, lat_tiles_per_batch, bsz, d):
    return pl.BlockSpec((None, None, 1, d),
                        lambda i: (which, jnp.minimum(i // lat_tiles_per_batch, bsz), 0, 0))


def _mods_kernel(c_ref, w_ref, b_ref, o_ref):
    c = c_ref[...]
    cond = _silu(c)
    o_ref[...] = jnp.dot(cond, w_ref[...], precision=HIGHEST, preferred_element_type=F32) + b_ref[...]


def _mods(cond_rows, ada_w, ada_b):
    depth, d, _ = ada_w.shape
    nr = cond_rows.shape[0]
    return pl.pallas_call(
        _mods_kernel,
        grid=(depth, N_MOD),
        in_specs=[pl.BlockSpec((nr, d), lambda l, j: (0, 0)),
                  pl.BlockSpec((None, d, d), lambda l, j: (l, 0, j)),
                  pl.BlockSpec((None, 1, d), lambda l, j: (l, 0, j))],
        out_specs=pl.BlockSpec((None, None, nr, d), lambda l, j: (l, j, 0, 0)),
        out_shape=jax.ShapeDtypeStruct((depth, N_MOD, nr, d), F32),
        compiler_params=_params(("arbitrary", "arbitrary")),
        name="adaln_mods",
    )(cond_rows, ada_w, ada_b.reshape(depth, 1, N_MOD * d))


def _norm_mod(x, g, sh, sc):
    return (_rms(x) * g) * (1.0 + sc) + sh


def _inproj_kernel(x_ref, g_ref, sh_ref, sc_ref, w_ref, w2_ref, o_ref, o2_ref, *, n_chunk):
    hb = _norm_mod(x_ref[...], g_ref[...], sh_ref[...], sc_ref[...]).astype(BF16)
    n = o_ref.shape[1]
    for n0 in range(0, n, n_chunk):
        o_ref[:, n0:n0 + n_chunk] = _dot(hb, w_ref[:, n0:n0 + n_chunk])
    o2_ref[...] = _dot(hb, w2_ref[...])


def _col_chunk(n):
    for c in (512, 384, 256, 128):
        if n % c == 0:
            return c
    raise ValueError(n)


def _inproj(xs, g, mods_l, w_bf16, w2_bf16, dims):
    bsz, seq, ctx = dims
    r, d = xs.shape
    n = w_bf16.shape[1]
    n2 = w2_bf16.shape[1]
    tm = _wide_tile(dims)
    lt = seq // tm
    return pl.pallas_call(
        functools.partial(_inproj_kernel, n_chunk=_col_chunk(n)),
        grid=(r // tm,),
        in_specs=[pl.BlockSpec((tm, d), lambda i: (i, 0)),
                  pl.BlockSpec((1, d), lambda i: (0, 0)),
                  _mod_spec(0, lt, bsz, d), _mod_spec(1, lt, bsz, d),
                  pl.BlockSpec((d, n), lambda i: (0, 0)),
                  pl.BlockSpec((d, n2), lambda i: (0, 0))],
        out_specs=[pl.BlockSpec((tm, n), lambda i: (i, 0)), pl.BlockSpec((tm, n2), lambda i: (i, 0))],
        out_shape=[jax.ShapeDtypeStruct((r, n), F32), jax.ShapeDtypeStruct((r, n2), F32)],
        compiler_params=_params(("arbitrary",)),
        name="norm_mod_inproj",
    )(xs, g.reshape(1, d), mods_l, mods_l, w_bf16, w2_bf16)


def _ssd_inproj_kernel(first_ref, last_ref, x_ref, xp_ref, xn_ref, g_ref, sh_ref, sc_ref, w_ref, wdt_ref, cw_ref,
                       cb_ref, z_ref, xbc_ref, dt_ref, *, n_chunk):
    i = pl.program_id(0)
    g, sh, sc = g_ref[...], sh_ref[...], sc_ref[...]
    hb = _norm_mod(x_ref[...], g, sh, sc).astype(BF16)
    halo = jnp.concatenate([xp_ref[...], xn_ref[...]], axis=0)
    hb_ext = jnp.concatenate([hb, _norm_mod(halo, g, sh, sc).astype(BF16)], axis=0)
    tm = hb.shape[0]
    nz = z_ref.shape[1]
    for n0 in range(0, nz, n_chunk):
        z_ref[:, n0:n0 + n_chunk] = _dot(hb, w_ref[:, n0:n0 + n_chunk])
    dt_ref[...] = _dot(hb, wdt_ref[...])
    row = lax.broadcasted_iota(jnp.int32, (tm, n_chunk), 0)
    keep_prev = first_ref[i] == 0
    keep_next = last_ref[i] == 0
    for n0 in range(0, xbc_ref.shape[1], n_chunk):
        cs = slice(n0, n0 + n_chunk)
        t = _dot(hb_ext, w_ref[:, nz + n0:nz + n0 + n_chunk])
        cur = t[:tm]
        prev_row = jnp.where(keep_prev, t[tm + 7:tm + 8], 0.0)
        next_row = jnp.where(keep_next, t[tm + 8:tm + 9], 0.0)
        down = jnp.where(row == 0, prev_row, pltpu.roll(cur, 1, 0))
        up = jnp.where(row == tm - 1, next_row, pltpu.roll(cur, tm - 1, 0))
        y = cw_ref[0:1, cs] * down + cw_ref[1:2, cs] * cur + cw_ref[2:3, cs] * up + cb_ref[:, cs]
        xbc_ref[:, cs] = _silu(y)


def _ssd_inproj(xs, g, mods_l, w_main, w_dt, conv_w, conv_b, dims):
    bsz, seq, ctx = dims
    r, d = xs.shape
    tm = ROW_TILE
    lt = seq // tm
    n2 = w_dt.shape[1]
    starts = [b * seq for b in range(bsz)] + [bsz * seq + b * ctx for b in range(bsz)]
    ends = [s + (seq if k < bsz else ctx) for k, s in enumerate(starts)]
    first = np.array([1 if (i * tm) in starts else 0 for i in range(r // tm)], np.int32)
    last = np.array([1 if ((i + 1) * tm) in ends else 0 for i in range(r // tm)], np.int32)
    sub = tm // 8
    nblk8 = r // 8

    def mod_spec(which):
        return pl.BlockSpec((None, None, 1, d), lambda i, f, l: (which, jnp.minimum(i // lt, bsz), 0, 0))

    def const(shape):
        return pl.BlockSpec(shape, lambda i, f, l: (0, 0))

    grid_spec = pltpu.PrefetchScalarGridSpec(
        num_scalar_prefetch=2,
        grid=(r // tm,),
        in_specs=[pl.BlockSpec((tm, d), lambda i, f, l: (i, 0)),
                  pl.BlockSpec((8, d), lambda i, f, l: (jnp.maximum(i * sub - 1, 0), 0)),
                  pl.BlockSpec((8, d), lambda i, f, l: (jnp.minimum((i + 1) * sub, nblk8 - 1), 0)),
                  const((1, d)), mod_spec(0), mod_spec(1),
                  const((d, SSD_INNER + SSD_CONV_DIM)), const((d, n2)),
                  const((3, SSD_CONV_DIM)), const((1, SSD_CONV_DIM))],
        out_specs=[pl.BlockSpec((tm, SSD_INNER), lambda i, f, l: (i, 0)),
                   pl.BlockSpec((tm, SSD_CONV_DIM), lambda i, f, l: (i, 0)),
                   pl.BlockSpec((tm, n2), lambda i, f, l: (i, 0))],
    )
    return pl.pallas_call(
        functools.partial(_ssd_inproj_kernel, n_chunk=512),
        grid_spec=grid_spec,
        out_shape=[jax.ShapeDtypeStruct((r, SSD_INNER), F32),
                   jax.ShapeDtypeStruct((r, SSD_CONV_DIM), F32),
                   jax.ShapeDtypeStruct((r, n2), F32)],
        compiler_params=_params(("arbitrary",)),
        name="ssd_inproj_conv_silu",
    )(jnp.asarray(first), jnp.asarray(last), xs, xs, xs, g.reshape(1, d), mods_l, mods_l, w_main, w_dt,
      conv_w, conv_b.reshape(1, -1))


def _chunk_block(b, d, s, bsz, seq, ctx):
    nc_ctx = ctx // CHUNK
    nc_lat = seq // CHUNK
    in_ctx = s < nc_ctx
    pc = jnp.where(d == 0, s, nc_ctx - 1 - s)
    pls = s - nc_ctx
    plat = jnp.where(d == 0, pls, nc_lat - 1 - pls)
    ctx_blk = bsz * nc_lat + b * nc_ctx + pc
    lat_blk = b * nc_lat + plat
    return jnp.where(in_ctx, ctx_blk, lat_blk)


def _dir_masks(d):
    ii = lax.broadcasted_iota(jnp.int32, (CHUNK, CHUNK), 0)
    jj = lax.broadcasted_iota(jnp.int32, (CHUNK, CHUNK), 1)
    sign = 1 - 2 * d
    return (ii - jj) * sign >= 0


def _ssd_scan_kernel(x_ref, b_ref, c_ref, dt_ref, bias_ref, a_ref, y_ref, st_ref):
    d = pl.program_id(1)
    s = pl.program_id(2)

    @pl.when(s == 0)
    def _():
        st_ref[...] = jnp.zeros_like(st_ref)

    mask = _dir_masks(d)
    tri = jnp.where(mask, 1.0, 0.0).astype(F32)
    dt = _softplus(dt_ref[...] + bias_ref[...])
    dta = dt * a_ref[...]
    la = jnp.dot(tri, dta, precision=HIGHEST, preferred_element_type=F32)
    total = jnp.sum(dta, axis=0, keepdims=True)
    la_t = la.T
    dt_t = dt.T
    ws_t = (jnp.exp(total - la) * dt).T
    e_tot = jnp.exp(total)
    lane = lax.broadcasted_iota(jnp.int32, (CHUNK, LANES), 1)
    lo_half = lane < SSD_HEAD_DIM
    hg = SSD_HEADS // SSD_GROUPS
    for g in range(SSD_GROUPS):
        bg32 = b_ref[:, g * SSD_STATE:(g + 1) * SSD_STATE]
        cg = c_ref[:, g * SSD_STATE:(g + 1) * SSD_STATE].astype(BF16)
        cb = _dot_nt(cg, bg32.astype(BF16))
        bg_t = bg32.T
        for pr in range(hg // 2):
            h0 = g * hg + 2 * pr
            col0 = slice((h0 * SSD_HEAD_DIM), (h0 + 2) * SSD_HEAD_DIM)
            xpb = x_ref[:, col0].astype(BF16)
            ys, sts, la_cols = [], [], []
            for h in (h0, h0 + 1):
                la_col = jnp.broadcast_to(la[:, h:h + 1], (CHUNK, LANES))
                la_cols.append(la_col)
                w = jnp.exp(jnp.where(mask, la_col - la_t[h:h + 1, :], -jnp.inf)) * cb * dt_t[h:h + 1, :]
                ys.append(_dot(w.astype(BF16), xpb))
                sts.append(_dot((bg_t * ws_t[h:h + 1, :]).astype(BF16), xpb))
            st = st_ref[g, :, pr * LANES:(pr + 1) * LANES]
            e_pair = jnp.exp(jnp.where(lo_half, la_cols[0], la_cols[1]))
            y_ref[:, col0] = jnp.where(lo_half, ys[0], ys[1]) + e_pair * _dot(cg, st.astype(BF16))
            et_pair = jnp.where(lo_half[0:1], e_tot[:, h0:h0 + 1], e_tot[:, h0 + 1:h0 + 2])
            st_ref[g, :, pr * LANES:(pr + 1) * LANES] = et_pair * st + jnp.where(lo_half, sts[0], sts[1])


def _ssd_scan(xbc, dt_raw, dt_bias_pad, a_pad, dims):
    bsz, seq, ctx = dims
    r = xbc.shape[0]
    nsteps = (seq + ctx) // CHUNK
    blk = functools.partial(_chunk_block, bsz=bsz, seq=seq, ctx=ctx)
    xw = SSD_INNER
    return pl.pallas_call(
        _ssd_scan_kernel,
        grid=(bsz, 2, nsteps),
        in_specs=[pl.BlockSpec((CHUNK, xw), lambda b, d, s: (blk(b, d, s), 0)),
                  pl.BlockSpec((CHUNK, SSD_BC), lambda b, d, s: (blk(b, d, s), xw // SSD_BC)),
                  pl.BlockSpec((CHUNK, SSD_BC), lambda b, d, s: (blk(b, d, s), xw // SSD_BC + 1)),
                  pl.BlockSpec((CHUNK, LANES), lambda b, d, s: (blk(b, d, s), d)),
                  pl.BlockSpec((None, 1, LANES), lambda b, d, s: (d, 0, 0)),
                  pl.BlockSpec((None, 1, LANES), lambda b, d, s: (d, 0, 0))],
        out_specs=pl.BlockSpec((None, CHUNK, xw), lambda b, d, s: (d, blk(b, d, s), 0)),
        out_shape=jax.ShapeDtypeStruct((2, r, xw), F32),
        scratch_shapes=[pltpu.VMEM((SSD_GROUPS, SSD_STATE, xw // SSD_GROUPS), F32)],
        compiler_params=_params(("arbitrary", "arbitrary", "arbitrary")),
        name="ssd_scan",
    )(xbc, xbc, xbc, dt_raw, dt_bias_pad, a_pad)


def _ssd_out_kernel(yf_ref, yb_ref, xc_ref, z_ref, dexp_ref, nw_ref, w_ref, xs_ref, gate_ref, o_ref):
    y = yf_ref[...] + yb_ref[...] + dexp_ref[...] * xc_ref[...]
    y = y * _silu(z_ref[...])
    a = (_rms(y) * nw_ref[...]).astype(BF16)
    o_ref[...] = xs_ref[...] + gate_ref[...] * _dot(a, w_ref[...])


def _ssd_out(y2, xbc, t_main, d_exp, norm_w, w_out_bf16, xs, mods_l, dims):
    bsz, seq, ctx = dims
    r, d = xs.shape
    tm = ROW_TILE
    lt = seq // tm
    xw = SSD_INNER
    return pl.pallas_call(
        _ssd_out_kernel,
        grid=(r // tm,),
        in_specs=[pl.BlockSpec((None, tm, xw), lambda i: (0, i, 0)),
                  pl.BlockSpec((None, tm, xw), lambda i: (1, i, 0)),
                  pl.BlockSpec((tm, xw), lambda i: (i, 0)),
                  pl.BlockSpec((tm, xw), lambda i: (i, 0)),
                  pl.BlockSpec((1, xw), lambda i: (0, 0)),
                  pl.BlockSpec((1, xw), lambda i: (0, 0)),
                  pl.BlockSpec((xw, d), lambda i: (0, 0)),
                  pl.BlockSpec((tm, d), lambda i: (i, 0)),
                  _mod_spec(2, lt, bsz, d)],
        out_specs=pl.BlockSpec((tm, d), lambda i: (i, 0)),
        out_shape=jax.ShapeDtypeStruct((r, d), F32),
        compiler_params=_params(("arbitrary",)),
        name="ssd_gated_norm_outproj",
    )(y2, y2, xbc, t_main, d_exp, norm_w.reshape(1, xw), w_out_bf16, xs, mods_l)


def _ssd_layer(xs, mods_l, g1, w_in, conv_w, conv_b, dt_bias, a_log, d_skip, norm_w, w_out, dims):
    d = xs.shape[1]
    main = SSD_INNER + SSD_CONV_DIM
    w_main = w_in[:, :main].astype(BF16)
    w_dt = jnp.zeros((d, 2 * LANES), F32)
    w_dt = w_dt.at[:, :SSD_HEADS].set(w_in[:, main:main + SSD_HEADS])
    w_dt = w_dt.at[:, LANES:LANES + SSD_HEADS].set(w_in[:, main + SSD_HEADS:]).astype(BF16)
    t_z, xbc, dt_raw = _ssd_inproj(xs, g1, mods_l, w_main, w_dt, conv_w, conv_b, dims)
    pad = jnp.zeros((2, 1, LANES - SSD_HEADS), F32)
    bias_pad = jnp.concatenate([dt_bias.astype(F32).reshape(2, 1, SSD_HEADS), pad], axis=-1)
    a_pad = jnp.concatenate([-jnp.exp(a_log.astype(F32)).reshape(2, 1, SSD_HEADS), pad], axis=-1)
    y2 = _ssd_scan(xbc, dt_raw, bias_pad, a_pad, dims)
    d_exp = jnp.repeat(d_skip.astype(F32), SSD_HEAD_DIM).reshape(1, SSD_INNER)
    return _ssd_out(y2, xbc, t_z, d_exp, norm_w, w_out.astype(BF16), xs, mods_l, dims)


def _qkv_rope_kernel(x_ref, g_ref, sh_ref, sc_ref, w_ref, cos_ref, sin_ref, q_ref, k_ref, v_ref, *, n_lat_tiles):
    i = pl.program_id(0)
    hb = _norm_mod(x_ref[...], g_ref[...], sh_ref[...], sc_ref[...]).astype(BF16)
    d = x_ref.shape[1]
    is_ctx = i >= n_lat_tiles
    cos = jnp.where(is_ctx, 1.0, cos_ref[...])
    sin = jnp.where(is_ctx, 0.0, sin_ref[...])
    nrep = d // LANES
    cos = jnp.concatenate([cos] * nrep, axis=1)
    sin = jnp.concatenate([sin] * nrep, axis=1)

    def mm(c):
        return _dot(hb, w_ref[:, c * d:(c + 1) * d])

    q = mm(0) * cos + mm(3) * sin
    q_ref[...] = (q * (DIFF_HEAD_DIM ** -0.5 * LOG2_E)).astype(BF16)
    k_ref[...] = (mm(1) * cos + mm(4) * sin).astype(BF16)
    v_ref[...] = mm(2).astype(BF16)


def _rope_tables(seq):
    rows = seq // GRID_W
    row = jnp.repeat(jnp.arange(rows, dtype=F32), GRID_W)
    col = jnp.tile(jnp.arange(GRID_W, dtype=F32), rows)
    inv = ROPE_BASE ** (-jnp.arange(ROPE_Q, dtype=F32) / ROPE_Q)
    ang_r = row[:, None] * inv
    ang_c = col[:, None] * inv
    cos = jnp.concatenate([jnp.cos(ang_r), jnp.cos(ang_r), jnp.cos(ang_c), jnp.cos(ang_c)], axis=1)
    sin = jnp.concatenate([-jnp.sin(ang_r), jnp.sin(ang_r), -jnp.sin(ang_c), jnp.sin(ang_c)], axis=1)
    return jnp.tile(cos, (1, 2)), jnp.tile(sin, (1, 2))


def _rope_partner_cols(d):
    col = np.arange(d)
    within = col % (2 * ROPE_Q)
    return np.where(within < ROPE_Q, col + ROPE_Q, col - ROPE_Q)


def _qkv_rope(xs, g, mods_l, w_qkv, dims):
    bsz, seq, ctx = dims
    r, d = xs.shape
    tm = _wide_tile(dims)
    lt = seq // tm
    perm = _rope_partner_cols(d)
    wq, wk, wv = w_qkv[:, :d], w_qkv[:, d:2 * d], w_qkv[:, 2 * d:]
    w_all = jnp.concatenate([wq, wk, wv, wq[:, perm], wk[:, perm]], axis=1).astype(BF16)
    cos, sin = _rope_tables(seq)
    out = jax.ShapeDtypeStruct((r, d), BF16)
    row_spec = pl.BlockSpec((tm, d), lambda i: (i, 0))
    tab_spec = pl.BlockSpec((tm, LANES), lambda i: (i % lt, 0))
    return pl.pallas_call(
        functools.partial(_qkv_rope_kernel, n_lat_tiles=bsz * lt),
        grid=(r // tm,),
        in_specs=[row_spec, pl.BlockSpec((1, d), lambda i: (0, 0)),
                  _mod_spec(0, lt, bsz, d), _mod_spec(1, lt, bsz, d),
                  pl.BlockSpec((d, 5 * d), lambda i: (0, 0)), tab_spec, tab_spec],
        out_specs=[row_spec, row_spec, row_spec],
        out_shape=[out, out, out],
        compiler_params=_params(("arbitrary",)),
        name="norm_mod_qkv_rope",
    )(xs, g.reshape(1, d), mods_l, mods_l, w_all, cos, sin)


def _lane_fold(x, op):
    parts = [x[:, t * LANES:(t + 1) * LANES] for t in range(x.shape[1] // LANES)]
    return functools.reduce(op, parts)


def _attn_kernel(lam_ref, q_ref, kc_ref, vc_ref, kx_ref, vx_ref, nw_ref, o_ref,
                 sc_ref, sx_ref, acc_ref, m_ref, l_ref, *, n_lat_q, tk, unroll, lambda_init):
    i = pl.program_id(2)
    q = q_ref[...]
    tq = q.shape[0]
    lane = lax.broadcasted_iota(jnp.int32, q.shape, 1)
    zero = jnp.zeros_like(q)
    qq = jnp.concatenate([jnp.where(lane < DIFF_HEAD_DIM, q, zero),
                          jnp.where(lane >= DIFF_HEAD_DIM, q, zero)], axis=0)
    is_latent = i < n_lat_q
    n_x = kx_ref.shape[0] // tk

    s = _dot_nt(qq, kc_ref[...])
    sc_ref[...] = s
    m_ref[...] = _lane_fold(s, jnp.maximum)

    @pl.when(is_latent)
    def _():
        def p1(j, m_):
            kj = kx_ref[pl.ds(pl.multiple_of(j * tk, tk), tk), :]
            sj = _dot_nt(qq, kj)
            sx_ref[j] = sj
            return jnp.maximum(m_, _lane_fold(sj, jnp.maximum))

        m_ref[...] = lax.fori_loop(0, n_x, p1, m_ref[...], unroll=unroll)

    mrow = jnp.max(m_ref[...], axis=1, keepdims=True)

    p = jnp.exp2(sc_ref[...] - mrow)
    acc_ref[...] = _dot(p.astype(BF16), vc_ref[...])
    l_ref[...] = _lane_fold(p, jnp.add)

    @pl.when(is_latent)
    def _():
        def p2(j, l_):
            vj = vx_ref[pl.ds(pl.multiple_of(j * tk, tk), tk), :]
            pj = jnp.exp2(sx_ref[j] - mrow)
            acc_ref[...] += _dot(pj.astype(BF16), vj)
            return l_ + _lane_fold(pj, jnp.add)

        l_ref[...] = lax.fori_loop(0, n_x, p2, l_ref[...], unroll=unroll)

    on = acc_ref[...] / jnp.sum(l_ref[...], axis=1, keepdims=True)
    o = on[:tq] - lam_ref[0] * on[tq:]
    o = _rms(o) * nw_ref[...] * (1.0 - lambda_init)
    o_ref[...] = o.astype(o_ref.dtype)


def _diff_attention(q, k, v, lam_full, subln_w, lambda_init, dims):
    bsz, seq, ctx = dims
    r, d = q.shape
    tq = ATTN_Q_TILE
    tk = min(512, seq)
    n_lat_q = seq // tq
    n_ctx_q = ctx // tq
    nq = n_lat_q + n_ctx_q

    def q_idx(b, h, i):
        return (jnp.where(i < n_lat_q, b * n_lat_q + i, bsz * n_lat_q + b * n_ctx_q + (i - n_lat_q)), h)

    ctx_spec = pl.BlockSpec((ctx, LANES), lambda b, h, i: (bsz * seq // ctx + b, h))
    lat_spec = pl.BlockSpec((seq, LANES), lambda b, h, i: (b, h))
    return pl.pallas_call(
        functools.partial(_attn_kernel, n_lat_q=n_lat_q, tk=tk, unroll=min(ATTN_UNROLL, seq // tk),
                          lambda_init=lambda_init),
        grid=(bsz, DIFF_HEADS, nq),
        in_specs=[pl.BlockSpec(memory_space=pltpu.SMEM),
                  pl.BlockSpec((tq, LANES), q_idx),
                  ctx_spec, ctx_spec, lat_spec, lat_spec,
                  pl.BlockSpec((1, LANES), lambda b, h, i: (0, 0))],
        out_specs=pl.BlockSpec((tq, LANES), q_idx),
        out_shape=jax.ShapeDtypeStruct((r, d), BF16),
        scratch_shapes=[pltpu.VMEM((2 * tq, ctx), F32),
                        pltpu.VMEM((seq // tk, 2 * tq, tk), F32),
                        pltpu.VMEM((2 * tq, LANES), F32),
                        pltpu.VMEM((2 * tq, LANES), F32),
                        pltpu.VMEM((2 * tq, LANES), F32)],
        compiler_params=_params(("arbitrary", "arbitrary", "arbitrary")),
        name="diff_attention",
    )(lam_full, q, k, v, k, v, subln_w.reshape(1, LANES))


def _proj_res_kernel(a_ref, w_ref, xs_ref, gate_ref, o_ref):
    o_ref[...] = xs_ref[...] + gate_ref[...] * _dot(a_ref[...], w_ref[...])


def _proj_res(a_bf16, w_bf16, xs, mods_l, dims):
    bsz, seq, ctx = dims
    r, d = xs.shape
    kdim = a_bf16.shape[1]
    tm = _wide_tile(dims)
    lt = seq // tm
    return pl.pallas_call(
        _proj_res_kernel,
        grid=(r // tm,),
        in_specs=[pl.BlockSpec((tm, kdim), lambda i: (i, 0)),
                  pl.BlockSpec((kdim, d), lambda i: (0, 0)),
                  pl.BlockSpec((tm, d), lambda i: (i, 0)),
                  _mod_spec(2, lt, bsz, d)],
        out_specs=pl.BlockSpec((tm, d), lambda i: (i, 0)),
        out_shape=jax.ShapeDtypeStruct((r, d), F32),
        compiler_params=_params(("arbitrary",)),
        name="outproj_gate_residual",
    )(a_bf16, w_bf16, xs, mods_l)


def _diff_layer(xs, mods_l, g1, w_qkv, lam, subln_w, w_out, lambda_init, dims):
    q, k, v = _qkv_rope(xs, g1, mods_l, w_qkv, dims)
    lam32 = lam.astype(F32)
    lam_full = (jnp.exp(jnp.sum(lam32[0] * lam32[1])) - jnp.exp(jnp.sum(lam32[2] * lam32[3]))
                + lambda_init).reshape(1)
    o = _diff_attention(q, k, v, lam_full, subln_w, lambda_init, dims)
    return _proj_res(o, w_out.astype(BF16), xs, mods_l, dims)


def _mlstm_scan_kernel(q_ref, k_ref, v_ref, g_ref, bias_ref, h_ref, c_ref, n_ref, m_ref):
    d = pl.program_id(1)
    s = pl.program_id(2)

    @pl.when(s == 0)
    def _():
        c_ref[...] = jnp.zeros_like(c_ref)
        n_ref[...] = jnp.zeros_like(n_ref)
        m_ref[...] = jnp.zeros_like(m_ref)

    mask = _dir_masks(d)
    tri = jnp.where(mask, 1.0, 0.0).astype(F32)
    gates = g_ref[...] + bias_ref[...]
    ig = gates[:, 0:ML_HEADS]
    lf = _log_sigmoid(gates)
    bcum = jnp.dot(tri, lf, precision=HIGHEST, preferred_element_type=F32)
    btot = jnp.sum(lf, axis=0, keepdims=True)
    bcum_t = bcum.T
    gates_t = gates.T
    row_last = jnp.where(d == 0, CHUNK - 1, 0)
    rsel = lax.broadcasted_iota(jnp.int32, (CHUNK, 1), 0) == row_last
    for h in range(ML_HEADS):
        fcol = ML_HEADS + h
        m_prev = m_ref[h:h + 1, 0:1]
        bcol = bcum[:, fcol:fcol + 1]
        brow = bcum_t[fcol:fcol + 1, :]
        irow = gates_t[h:h + 1, :]
        icol = ig[:, h:h + 1]
        gcol = bcol + m_prev
        dmat = jnp.where(mask, bcol - brow + irow, -jnp.inf)
        mt = jnp.maximum(gcol, jnp.max(dmat, axis=1, keepdims=True))
        q32 = q_ref[:, h * ML_QK_DIM:(h + 1) * ML_QK_DIM] * (ML_QK_DIM ** -0.5)
        qh = q32.astype(BF16)
        kh32 = k_ref[:, h * ML_QK_DIM:(h + 1) * ML_QK_DIM]
        kh = kh32.astype(BF16)
        vh = v_ref[:, h * ML_V_DIM:(h + 1) * ML_V_DIM].astype(BF16)
        sm = _dot_nt(qh, kh) * jnp.exp(dmat - mt)
        inter = jnp.exp(gcol - mt)
        cst = c_ref[h]
        nst = n_ref[h:h + 1, :]
        num = _dot(sm.astype(BF16), vh) + inter * _dot(qh, cst.astype(BF16))
        qn = jnp.sum(q32 * nst, axis=1, keepdims=True)
        den = jnp.sum(sm, axis=1, keepdims=True) + inter * qn
        h_ref[:, h * ML_V_DIM:(h + 1) * ML_V_DIM] = num / jnp.maximum(jnp.abs(den), jnp.exp(-mt))
        m_new = jnp.sum(jnp.where(rsel, mt, 0.0), axis=0, keepdims=True)
        btot_h = btot[:, fcol:fcol + 1]
        wk = jnp.exp(btot_h - bcol + icol - m_new)
        cscale = jnp.exp(btot_h + m_prev - m_new)
        kw = kh32 * wk
        c_ref[h] = cscale * cst + _dot_tn(kw.astype(BF16), vh)
        n_ref[h:h + 1, :] = cscale * nst + jnp.sum(kw, axis=0, keepdims=True)
        m_ref[h:h + 1, :] = jnp.broadcast_to(m_new, (1, LANES))


def _mlstm_scan(t_main, gates_raw, bias_pad, dims):
    bsz, seq, ctx = dims
    r = t_main.shape[0]
    nsteps = (seq + ctx) // CHUNK
    blk = functools.partial(_chunk_block, bsz=bsz, seq=seq, ctx=ctx)
    return pl.pallas_call(
        _mlstm_scan_kernel,
        grid=(bsz, 2, nsteps),
        in_specs=[pl.BlockSpec((CHUNK, ML_QK), lambda b, d, s: (blk(b, d, s), 0)),
                  pl.BlockSpec((CHUNK, ML_QK), lambda b, d, s: (blk(b, d, s), 1)),
                  pl.BlockSpec((CHUNK, ML_V), lambda b, d, s: (blk(b, d, s), 1)),
                  pl.BlockSpec((CHUNK, LANES), lambda b, d, s: (blk(b, d, s), d)),
                  pl.BlockSpec((None, 1, LANES), lambda b, d, s: (d, 0, 0))],
        out_specs=pl.BlockSpec((None, CHUNK, ML_V), lambda b, d, s: (d, blk(b, d, s), 0)),
        out_shape=jax.ShapeDtypeStruct((2, r, ML_V), F32),
        scratch_shapes=[pltpu.VMEM((ML_HEADS, ML_QK_DIM, ML_V_DIM), F32),
                        pltpu.VMEM((8, ML_QK_DIM), F32),
                        pltpu.VMEM((8, LANES), F32)],
        compiler_params=_params(("arbitrary", "arbitrary", "arbitrary")),
        name="mlstm_scan",
    )(t_main, t_main, t_main, gates_raw, bias_pad)


def _mlstm_out_kernel(hf_ref, hb_ref, o_ref_in, nw_ref, w_ref, xs_ref, gate_ref, out_ref):
    u = None
    for h in range(ML_HEADS):
        cs = slice(h * ML_V_DIM, (h + 1) * ML_V_DIM)
        a = _sigmoid(o_ref_in[:, cs]) * (hf_ref[:, cs] + hb_ref[:, cs])
        a = (_rms(a) * nw_ref[...]).astype(BF16)
        part = _dot(a, w_ref[cs, :])
        u = part if u is None else u + part
    out_ref[...] = xs_ref[...] + gate_ref[...] * u


def _mlstm_out(h2, t_main, norm_w, w_out_bf16, xs, mods_l, dims):
    bsz, seq, ctx = dims
    r, d = xs.shape
    tm = _wide_tile(dims)
    lt = seq // tm
    o_blk = (2 * ML_QK + ML_V) // ML_V
    return pl.pallas_call(
        _mlstm_out_kernel,
        grid=(r // tm,),
        in_specs=[pl.BlockSpec((None, tm, ML_V), lambda i: (0, i, 0)),
                  pl.BlockSpec((None, tm, ML_V), lambda i: (1, i, 0)),
                  pl.BlockSpec((tm, ML_V), lambda i: (i, o_blk)),
                  pl.BlockSpec((1, ML_V_DIM), lambda i: (0, 0)),
                  pl.BlockSpec((ML_V, d), lambda i: (0, 0)),
                  pl.BlockSpec((tm, d), lambda i: (i, 0)),
                  _mod_spec(2, lt, bsz, d)],
        out_specs=pl.BlockSpec((tm, d), lambda i: (i, 0)),
        out_shape=jax.ShapeDtypeStruct((r, d), F32),
        compiler_params=_params(("arbitrary",)),
        name="mlstm_norm_outproj",
    )(h2, h2, t_main, norm_w.reshape(1, ML_V_DIM), w_out_bf16, xs, mods_l)


def _mlstm_layer(xs, mods_l, g1, w_in, b_gates, norm_w, w_out, dims):
    d = xs.shape[1]
    main = 2 * ML_QK + 2 * ML_V
    w_main = w_in[:, :main].astype(BF16)
    wg = w_in[:, main:].reshape(d, 4, ML_HEADS)
    w_g = jnp.zeros((d, 2 * LANES), F32)
    bias = jnp.zeros((2, 1, LANES), F32)
    for dr in range(2):
        w_g = w_g.at[:, dr * LANES:dr * LANES + 2 * ML_HEADS].set(
            wg[:, 2 * dr:2 * dr + 2].reshape(d, 2 * ML_HEADS))
        bias = bias.at[dr, 0, :2 * ML_HEADS].set(b_gates.astype(F32)[2 * dr:2 * dr + 2].reshape(2 * ML_HEADS))
    t_main, gates_raw = _inproj(xs, g1, mods_l, w_main, w_g.astype(BF16), dims)
    h2 = _mlstm_scan(t_main, gates_raw, bias, dims)
    return _mlstm_out(h2, t_main, norm_w, w_out.astype(BF16), xs, mods_l, dims)


def _router_kernel(x_ref, g_ref, sh_ref, sc_ref, wr_ref, br_ref, h_ref, idx_ref, gate_ref, rank_ref, cnt_ref,
                   carry_ref):
    i = pl.program_id(0)

    @pl.when(i == 0)
    def _():
        carry_ref[...] = jnp.zeros_like(carry_ref)

    h = _norm_mod(x_ref[...], g_ref[...], sh_ref[...], sc_ref[...])
    tm, d = h.shape
    h_ref[...] = _pack_bf16_pairs(h)
    logits = lax.dot_general(wr_ref[...], h, (((1,), (1,)), ((), ())), precision=HIGHEST,
                             preferred_element_type=F32) + br_ref[...]
    eidx = lax.broadcasted_iota(jnp.int32, logits.shape, 0)
    work = logits
    vals, idxs = [], []
    picked = jnp.zeros(logits.shape, F32)
    for _ in range(TOP_K):
        mx = jnp.max(work, axis=0, keepdims=True)
        ix = jnp.min(jnp.where(work == mx, eidx, N_EXPERTS), axis=0, keepdims=True)
        sel = eidx == ix
        vals.append(mx)
        idxs.append(ix)
        picked = jnp.where(sel, 1.0, picked)
        work = jnp.where(sel, -jnp.inf, work)
    es = [jnp.exp(v - vals[0]) for v in vals]
    tot = es[0] + es[1] + es[2] + es[3]
    jj = lax.broadcasted_iota(jnp.int32, (tm, tm), 0)
    ii = lax.broadcasted_iota(jnp.int32, (tm, tm), 1)
    upper = jnp.where(jj <= ii, 1.0, 0.0).astype(BF16)
    incl = _dot(picked.astype(BF16), upper)
    carry = carry_ref[:, 0:1]
    excl = incl - picked + carry
    for k in range(TOP_K):
        idx_ref[k:k + 1, :] = idxs[k]
        gate_ref[k:k + 1, :] = es[k] / tot
        rk = jnp.sum(jnp.where(eidx == idxs[k], excl, 0.0), axis=0, keepdims=True)
        rank_ref[k:k + 1, :] = rk.astype(jnp.int32)
    new_carry = carry + jnp.sum(picked, axis=1, keepdims=True)
    carry_ref[...] = jnp.broadcast_to(new_carry, carry_ref.shape)
    cnt_ref[...] = jnp.broadcast_to(new_carry, cnt_ref.shape).astype(jnp.int32)


def _router(xs, g2, mods_l, w_router, b_router, dims):
    bsz, seq, ctx = dims
    r, d = xs.shape
    tm = _wide_tile(dims)
    lt = seq // tm
    tok_spec = pl.BlockSpec((TOP_K, tm), lambda i: (0, i))
    return pl.pallas_call(
        _router_kernel,
        grid=(r // tm,),
        in_specs=[pl.BlockSpec((tm, d), lambda i: (i, 0)),
                  pl.BlockSpec((1, d), lambda i: (0, 0)),
                  _mod_spec(3, lt, bsz, d), _mod_spec(4, lt, bsz, d),
                  pl.BlockSpec((N_EXPERTS, d), lambda i: (0, 0)),
                  pl.BlockSpec((N_EXPERTS, 1), lambda i: (0, 0))],
        out_specs=[pl.BlockSpec((tm, d // 2), lambda i: (i, 0)), tok_spec, tok_spec, tok_spec,
                   pl.BlockSpec((N_EXPERTS, LANES), lambda i: (0, 0))],
        out_shape=[jax.ShapeDtypeStruct((r, d // 2), jnp.int32),
                   jax.ShapeDtypeStruct((TOP_K, r), jnp.int32),
                   jax.ShapeDtypeStruct((TOP_K, r), F32),
                   jax.ShapeDtypeStruct((TOP_K, r), jnp.int32),
                   jax.ShapeDtypeStruct((N_EXPERTS, LANES), jnp.int32)],
        scratch_shapes=[pltpu.VMEM((N_EXPERTS, LANES), F32)],
        compiler_params=_params(("arbitrary",)),
        name="norm_mod_router_top4",
    )(xs, g2.reshape(1, d), mods_l, mods_l, w_router.T, b_router.reshape(N_EXPERTS, 1))


def _pack_bf16_pairs(x):
    w = x.shape[1]
    xr = x.astype(BF16).astype(F32)
    hi = lax.bitcast_convert_type(xr[:, :w // 2], jnp.int32)
    lo = lax.bitcast_convert_type(xr[:, w // 2:], jnp.int32)
    return hi | lax.shift_right_logical(lo, 16)


def _unpack_bf16_pairs(xp):
    hi = lax.bitcast_convert_type(xp & jnp.int32(-65536), F32)
    lo = lax.bitcast_convert_type(lax.shift_left(xp, 16), F32)
    return hi, lo


def _expert_kernel(be_ref, nb_ref, first_ref, slot_ref, next_ref, x_ref, wgu_hbm, bgu_ref, wdn_hbm, bdn_ref, y_ref,
                   wgu_buf, wdn_buf, wgu_bf, wdn_bf, sems, *, layer):
    i = pl.program_id(0)
    active = i < nb_ref[0]
    slot = slot_ref[i]

    def fetch(e, s):
        return (pltpu.make_async_copy(wgu_hbm.at[layer, e], wgu_buf.at[s], sems.at[s, 0]),
                pltpu.make_async_copy(wdn_hbm.at[layer, e], wdn_buf.at[s], sems.at[s, 1]))

    @pl.when(jnp.logical_and(active, first_ref[i] == 1))
    def _():
        @pl.when(i == 0)
        def _():
            for cp in fetch(be_ref[0], 0):
                cp.start()

        for cp in fetch(be_ref[i], slot):
            cp.wait()
        wgu_bf[...] = wgu_buf[slot].astype(BF16)
        wdn_bf[...] = wdn_buf[slot].astype(BF16)

        @pl.when(next_ref[i] >= 0)
        def _():
            for cp in fetch(next_ref[i], 1 - slot):
                cp.start()

    @pl.when(active)
    def _():
        de = wdn_bf.shape[0]
        half = x_ref.shape[1]
        xa, xb = _unpack_bf16_pairs(x_ref[...])
        gu = (_dot(xa.astype(BF16), wgu_bf[:half, :]) + _dot(xb.astype(BF16), wgu_bf[half:, :]) + bgu_ref[...])
        g = jnp.minimum(gu[:, :de], SWIGLU_LIMIT)
        u = jnp.clip(gu[:, de:], -SWIGLU_LIMIT, SWIGLU_LIMIT)
        a = (u + 1.0) * g * _sigmoid(SWIGLU_ALPHA * g)
        y_ref[...] = _pack_bf16_pairs(_dot(a.astype(BF16), wdn_bf[...]) + bdn_ref[...])


def _experts(xb, block_expert, nb_used, grp_first, grp_slot, grp_next, w_gu, b_gu, w_dn, b_dn, layer):
    nrows, half = xb.shape
    bm = MOE_BLOCK
    depth, ne, d, two_de = w_gu.shape
    de = two_de // 2
    nsp = 5
    grid_spec = pltpu.PrefetchScalarGridSpec(
        num_scalar_prefetch=nsp,
        grid=(nrows // bm,),
        in_specs=[pl.BlockSpec((bm, half), lambda i, be, *_: (i, 0)),
                  pl.BlockSpec(memory_space=pl.ANY),
                  pl.BlockSpec((None, None, 1, two_de), lambda i, be, *_: (layer, be[i], 0, 0)),
                  pl.BlockSpec(memory_space=pl.ANY),
                  pl.BlockSpec((None, None, 1, d), lambda i, be, *_: (layer, be[i], 0, 0))],
        out_specs=pl.BlockSpec((bm, d // 2), lambda i, be, *_: (i, 0)),
        scratch_shapes=[pltpu.VMEM((2, d, two_de), F32), pltpu.VMEM((2, de, d), F32),
                        pltpu.VMEM((d, two_de), BF16), pltpu.VMEM((de, d), BF16),
                        pltpu.SemaphoreType.DMA((2, 2))],
    )
    return pl.pallas_call(
        functools.partial(_expert_kernel, layer=layer),
        grid_spec=grid_spec,
        out_shape=jax.ShapeDtypeStruct((nrows, d // 2), jnp.int32),
        compiler_params=_params(("arbitrary",)),
        name="moe_expert_ffn",
    )(block_expert, nb_used, grp_first, grp_slot, grp_next, xb, w_gu, b_gu.reshape(depth, ne, 1, two_de),
      w_dn, b_dn.reshape(depth, ne, 1, d))


def _sc_row_tokens(dest, n_rows):
    top_k, r = dest.shape
    lanes = SC_LANES
    chunk = r // 2
    assert n_rows % lanes == 0 and r % 2 == 0 and chunk % lanes == 0
    fill_mask = (1 << (r.bit_length() - 1)) - 1
    mesh = plsc.VectorSubcoreMesh(core_axis_name="c", subcore_axis_name="s",
                                  num_cores=SC_CORES, num_subcores=SC_SUBCORES)

    def body(dest_hbm, out_hbm, dest_v, table_v):
        wid = lax.axis_index("s") * SC_CORES + lax.axis_index("c")

        @pl.when(wid == 0)
        def _():
            lane_id = lax.iota(jnp.int32, lanes)

            @pl.loop(0, n_rows // lanes, unroll=8)
            def _(c):
                table_v[pl.ds(c * lanes, lanes)] = (lane_id + c * lanes) & fill_mask

            for k in range(top_k):
                for half in range(2):
                    t0 = half * chunk
                    pltpu.sync_copy(dest_hbm.at[pl.ds(k * r + t0, chunk)], dest_v)

                    @pl.loop(0, chunk // lanes, unroll=8)
                    def _(v):
                        d = dest_v[pl.ds(v * lanes, lanes)]
                        plsc.store_scatter(table_v, [d], lane_id + (t0 + v * lanes))

            pltpu.sync_copy(table_v, out_hbm)

    cp = pltpu.CompilerParams(needs_layout_passes=False)
    return pl.kernel(
        body,
        out_type=jax.ShapeDtypeStruct((n_rows,), jnp.int32),
        mesh=mesh,
        scratch_types=[pltpu.VMEM((chunk,), jnp.int32), pltpu.VMEM((n_rows,), jnp.int32)],
        compiler_params=cp,
        name="sc_row_tokens",
    )(dest.reshape(top_k * r))


def _sc_gather_rows(table, idx):
    n = idx.shape[0]
    width = table.shape[1]
    nw = SC_CORES * SC_SUBCORES
    nb = SC_GATHER_ROWS
    per_w = n // nw
    assert n % nw == 0 and per_w % nb == 0, (n, nw, nb)
    steps = per_w // nb
    mesh = plsc.VectorSubcoreMesh(core_axis_name="c", subcore_axis_name="s",
                                  num_cores=SC_CORES, num_subcores=SC_SUBCORES)

    def body(table_hbm, idx_hbm, out_hbm, idx_a, idx_b, rows_a, rows_b, sem_a, sem_b):
        wid = lax.axis_index("s") * SC_CORES + lax.axis_index("c")
        base = wid * per_w
        slots = ((idx_a, rows_a, sem_a), (idx_b, rows_b, sem_b))

        def gather(j, slot):
            idx_v, rows_v, sem = slots[slot]
            return pltpu.make_async_copy(table_hbm.at[idx_v], rows_v, sem)

        def start(j, slot):
            off = pl.multiple_of(base + j * nb, 8)
            pltpu.sync_copy(idx_hbm.at[pl.ds(off, nb)], slots[slot][0])
            gather(j, slot).start()

        def finish(j, slot):
            off = pl.multiple_of(base + j * nb, 8)
            gather(j, slot).wait()
            pltpu.sync_copy(slots[slot][1], out_hbm.at[pl.ds(off, nb)])

        start(0, 0)

        @pl.loop(0, steps // 2)
        def _(p):
            j = 2 * p
            start(j + 1, 1)
            finish(j, 0)
            if steps % 2 == 1:
                start(j + 2, 0)
            else:
                @pl.when(j + 2 < steps)
                def _():
                    start(j + 2, 0)
            finish(j + 1, 1)

        if steps % 2 == 1:
            finish(steps - 1, 0)

    return pl.kernel(
        body,
        out_type=jax.ShapeDtypeStruct((n, width), table.dtype),
        mesh=mesh,
        scratch_types=[pltpu.VMEM((nb,), jnp.int32), pltpu.VMEM((nb,), jnp.int32),
                       pltpu.VMEM((nb, width), table.dtype), pltpu.VMEM((nb, width), table.dtype),
                       pltpu.SemaphoreType.DMA, pltpu.SemaphoreType.DMA],
        name="sc_gather_rows",
    )(table, idx)


def _combine_kernel(xs_ref, yg_ref, gt_ref, gate_ref, fg_ref, o_ref, *, final):
    f_hi = f_lo = None
    for k in range(TOP_K):
        y_hi, y_lo = _unpack_bf16_pairs(yg_ref[k])
        gk = gt_ref[:, k:k + 1]
        f_hi = y_hi * gk if f_hi is None else f_hi + y_hi * gk
        f_lo = y_lo * gk if f_lo is None else f_lo + y_lo * gk
    out = xs_ref[...] + gate_ref[...] * jnp.concatenate([f_hi, f_lo], axis=1)
    if final:
        out = _rms(out) * fg_ref[...]
    o_ref[...] = out


def _combine(xs, yg, gate_t, mods_l, final_g, final, dims):
    bsz, seq, ctx = dims
    r, d = xs.shape
    tm = _wide_tile(dims)
    lt = seq // tm
    rows_out = bsz * seq if final else r
    return pl.pallas_call(
        functools.partial(_combine_kernel, final=final),
        grid=(rows_out // tm,),
        in_specs=[pl.BlockSpec((tm, d), lambda i: (i, 0)),
                  pl.BlockSpec((TOP_K, tm, d // 2), lambda i: (0, i, 0)),
                  pl.BlockSpec((tm, TOP_K), lambda i: (i, 0)),
                  _mod_spec(5, lt, bsz, d),
                  pl.BlockSpec((1, d), lambda i: (0, 0))],
        out_specs=pl.BlockSpec((tm, d), lambda i: (i, 0)),
        out_shape=jax.ShapeDtypeStruct((rows_out, d), F32),
        compiler_params=_params(("arbitrary",)),
        name="moe_combine_residual",
    )(xs, yg, gate_t, mods_l, final_g.reshape(1, d))


def _moe_layer(xs, mods_l, g2, w_router, b_router, w_gu, b_gu, w_dn, b_dn, final_g, layer, final, dims):
    r, d = xs.shape
    bm = MOE_BLOCK
    hp, top_idx, gate, rank, cnt = _router(xs, g2, mods_l, w_router, b_router, dims)
    counts = cnt[:, 0]
    padded = (counts + bm - 1) // bm * bm
    pad_ends = jnp.cumsum(padded)
    pad_starts = pad_ends - padded
    eids = jnp.arange(N_EXPERTS, dtype=jnp.int32)
    onehot = top_idx[None] == eids[:, None, None]
    dest = jnp.sum(jnp.where(onehot, pad_starts[:, None, None], 0), axis=0) + rank
    n_blocks = -(-(r * TOP_K + N_EXPERTS * (bm - 1)) // bm)
    blk_start = jnp.arange(n_blocks, dtype=jnp.int32) * bm
    block_expert = jnp.minimum(jnp.sum((pad_ends[None, :] <= blk_start[:, None]).astype(jnp.int32), axis=1),
                               N_EXPERTS - 1)
    nb_total = pad_ends[-1] // bm
    nb_used = nb_total.astype(jnp.int32).reshape(1)
    blk = jnp.arange(n_blocks, dtype=jnp.int32)
    prev_e = jnp.concatenate([jnp.full((1,), -1, jnp.int32), block_expert[:-1]])
    grp_first = ((block_expert != prev_e) & (blk < nb_total)).astype(jnp.int32)
    grp_slot = (jnp.cumsum(grp_first) - 1) % 2
    later = (eids[None, :] > eids[:, None]) & (counts[None, :] > 0)
    next_of_e = jnp.min(jnp.where(later, eids[None, :], N_EXPERTS), axis=1)
    next_of_e = jnp.where(next_of_e == N_EXPERTS, -1, next_of_e)
    grp_next = jnp.sum(jnp.where(block_expert[:, None] == eids[None, :], next_of_e[None, :], 0), axis=1)
    row_tok = _sc_row_tokens(dest, n_blocks * bm)
    xb = _sc_gather_rows(hp, row_tok)
    yb = _experts(xb, block_expert, nb_used, grp_first, grp_slot.astype(jnp.int32), grp_next.astype(jnp.int32),
                  w_gu, b_gu, w_dn, b_dn, layer)
    yg = _sc_gather_rows(yb, dest.reshape(-1)).reshape(TOP_K, r, d // 2)
    return _combine(xs, yg, gate.T, mods_l, final_g, final, dims)


def kernel(x, c, ctx, c_ctx, ada_w, ada_b, norm1_g, norm2_g, ssd_w_in, ssd_conv_w, ssd_conv_b, ssd_dt_bias,
           ssd_a_log, ssd_d, ssd_norm_w, ssd_w_out, diff_w_qkv, diff_lam, diff_subln_w, diff_w_out, ml_w_in,
           ml_b_gates, ml_norm_w, ml_w_out, moe_w_router, moe_b_router, moe_w_gu, moe_b_gu, moe_w_dn, moe_b_dn,
           final_g):
    bsz, seq, d = x.shape
    n_ctx = ctx.shape[1]
    depth = ada_w.shape[0]
    dims = (bsz, seq, n_ctx)
    n_lat = bsz * seq
    xs = jnp.concatenate([x.reshape(n_lat, d), ctx.reshape(bsz * n_ctx, d)], axis=0)
    cond_rows = jnp.zeros((8, d), F32).at[:bsz].set(c).at[bsz].set(c_ctx)
    mods = _mods(cond_rows, ada_w, ada_b)
    mods = mods[:, :, :bsz + 1].reshape(depth, N_MOD, bsz + 1, 1, d)
    for i in range(depth):
        mods_l = mods[i]
        kind, j = i % 3, i // 3
        if kind == 0:
            xs = _ssd_layer(xs, mods_l, norm1_g[i], ssd_w_in[j], ssd_conv_w[j], ssd_conv_b[j], ssd_dt_bias[j],
                            ssd_a_log[j], ssd_d[j], ssd_norm_w[j], ssd_w_out[j], dims)
        elif kind == 1:
            lambda_init = 0.8 - 0.6 * math.exp(-0.3 * i)
            xs = _diff_layer(xs, mods_l, norm1_g[i], diff_w_qkv[j], diff_lam[j], diff_subln_w[j], diff_w_out[j],
                             lambda_init, dims)
        else:
            xs = _mlstm_layer(xs, mods_l, norm1_g[i], ml_w_in[j], ml_b_gates[j], ml_norm_w[j], ml_w_out[j], dims)
        xs = _moe_layer(xs, mods_l, norm2_g[i], moe_w_router[i], moe_b_router[i], moe_w_gu, moe_b_gu,
                        moe_w_dn, moe_b_dn, final_g, i, i == depth - 1, dims)
    return xs.reshape(bsz, seq, d)
```

```python
import functools
import math

import numpy as np
import jax
import jax.numpy as jnp
from jax import lax
from jax.experimental import pallas as pl
from jax.experimental.pallas import tpu as pltpu
from jax.experimental.pallas import tpu_sc as plsc

F32 = jnp.float32
BF16 = jnp.bfloat16
HIGHEST = lax.Precision.HIGHEST

GRID_W = 64
RMS_EPS = 1e-6
N_MOD = 6
SSD_HEAD_DIM = 64
SSD_HEADS = 32
SSD_GROUPS = 4
SSD_STATE = 128
SSD_INNER = SSD_HEADS * SSD_HEAD_DIM
SSD_BC = SSD_GROUPS * SSD_STATE
SSD_CONV_DIM = SSD_INNER + 2 * SSD_BC
DIFF_HEADS = 8
DIFF_HEAD_DIM = 64
DIFF_V_DIM = 128
ROPE_BASE = 10000.0
ROPE_Q = DIFF_HEAD_DIM // 4
ML_HEADS = 4
ML_QK_DIM = 128
ML_V_DIM = 256
ML_QK = ML_HEADS * ML_QK_DIM
ML_V = ML_HEADS * ML_V_DIM
N_EXPERTS = 32
TOP_K = 4
SWIGLU_LIMIT = 7.0
SWIGLU_ALPHA = 1.702
LOG2_E = 1.4426950408889634

LANES = 128
CHUNK = 128
ROW_TILE = 256
ROW_TILE_WIDE = 512
MOE_BLOCK = 256
MOE_BLOCKS_PER_STEP = 2
ATTN_Q_TILE = 256
ATTN_K_TILE = 512
ATTN_UNROLL = 16
SC_CORES = 2
SC_SUBCORES = 16
SC_LANES = 16
SC_GATHER_ROWS = 64
VMEM_LIMIT = 56 * 1024 * 1024


def _wide_tile(dims):
    bsz, seq, ctx = dims
    return min(ROW_TILE_WIDE, seq, bsz * ctx)


def _params(sem):
    return pltpu.CompilerParams(dimension_semantics=sem, vmem_limit_bytes=VMEM_LIMIT)


def _dot(a, b):
    return jnp.dot(a, b, preferred_element_type=F32)


def _dot_nt(a, b):
    return lax.dot_general(a, b, (((1,), (1,)), ((), ())), preferred_element_type=F32)


def _dot_tn(a, b):
    return lax.dot_general(a, b, (((0,), (0,)), ((), ())), preferred_element_type=F32)


def _rms(x):
    return x * lax.rsqrt(jnp.mean(x * x, axis=-1, keepdims=True) + RMS_EPS)


def _sigmoid(x):
    return 1.0 / (1.0 + jnp.exp(-x))


def _silu(x):
    return x * _sigmoid(x)


def _softplus(x):
    return jnp.maximum(x, 0.0) + jnp.log(1.0 + jnp.exp(-jnp.abs(x)))


def _log_sigmoid(x):
    return -_softplus(-x)


def _mod_spec(which, lat_tiles_per_batch, bsz, d):
    return pl.BlockSpec((None, None, 1, d),
                        lambda i: (which, jnp.minimum(i // lat_tiles_per_batch, bsz), 0, 0))


def _mods_kernel(c_ref, w_ref, b_ref, o_ref):
    c = c_ref[...]
    cond = _silu(c)
    o_ref[...] = jnp.dot(cond, w_ref[...], precision=HIGHEST, preferred_element_type=F32) + b_ref[...]


def _mods(cond_rows, ada_w, ada_b):
    depth, d, _ = ada_w.shape
    nr = cond_rows.shape[0]
    return pl.pallas_call(
        _mods_kernel,
        grid=(depth, N_MOD),
        in_specs=[pl.BlockSpec((nr, d), lambda l, j: (0, 0)),
                  pl.BlockSpec((None, d, d), lambda l, j: (l, 0, j)),
                  pl.BlockSpec((None, 1, d), lambda l, j: (l, 0, j))],
        out_specs=pl.BlockSpec((None, None, nr, d), lambda l, j: (l, j, 0, 0)),
        out_shape=jax.ShapeDtypeStruct((depth, N_MOD, nr, d), F32),
        compiler_params=_params(("arbitrary", "arbitrary")),
        name="adaln_mods",
    )(cond_rows, ada_w, ada_b.reshape(depth, 1, N_MOD * d))


def _norm_mod(x, g, sh, sc):
    return (_rms(x) * g) * (1.0 + sc) + sh


def _inproj_kernel(x_ref, g_ref, sh_ref, sc_ref, w_ref, w2_ref, o_ref, o2_ref, *, n_chunk):
    hb = _norm_mod(x_ref[...], g_ref[...], sh_ref[...], sc_ref[...]).astype(BF16)
    n = o_ref.shape[1]
    for n0 in range(0, n, n_chunk):
        o_ref[:, n0:n0 + n_chunk] = _dot(hb, w_ref[:, n0:n0 + n_chunk])
    o2_ref[...] = _dot(hb, w2_ref[...])


def _col_chunk(n):
    for c in (512, 384, 256, 128):
        if n % c == 0:
            return c
    raise ValueError(n)


def _inproj(xs, g, mods_l, w_bf16, w2_bf16, dims):
    bsz, seq, ctx = dims
    r, d = xs.shape
    n = w_bf16.shape[1]
    n2 = w2_bf16.shape[1]
    tm = _wide_tile(dims)
    lt = seq // tm
    return pl.pallas_call(
        functools.partial(_inproj_kernel, n_chunk=_col_chunk(n)),
        grid=(r // tm,),
        in_specs=[pl.BlockSpec((tm, d), lambda i: (i, 0)),
                  pl.BlockSpec((1, d), lambda i: (0, 0)),
                  _mod_spec(0, lt, bsz, d), _mod_spec(1, lt, bsz, d),
                  pl.BlockSpec((d, n), lambda i: (0, 0)),
                  pl.BlockSpec((d, n2), lambda i: (0, 0))],
        out_specs=[pl.BlockSpec((tm, n), lambda i: (i, 0)), pl.BlockSpec((tm, n2), lambda i: (i, 0))],
        out_shape=[jax.ShapeDtypeStruct((r, n), F32), jax.ShapeDtypeStruct((r, n2), F32)],
        compiler_params=_params(("arbitrary",)),
        name="norm_mod_inproj",
    )(xs, g.reshape(1, d), mods_l, mods_l, w_bf16, w2_bf16)


def _ssd_inproj_kernel(first_ref, last_ref, x_ref, xp_ref, xn_ref, g_ref, sh_ref, sc_ref, w_ref, wdt_ref, cw_ref,
                       cb_ref, z_ref, xbc_ref, dt_ref, *, n_chunk):
    i = pl.program_id(0)
    g, sh, sc = g_ref[...], sh_ref[...], sc_ref[...]
    hb = _norm_mod(x_ref[...], g, sh, sc).astype(BF16)
    halo = jnp.concatenate([xp_ref[...], xn_ref[...]], axis=0)
    hb_ext = jnp.concatenate([hb, _norm_mod(halo, g, sh, sc).astype(BF16)], axis=0)
    tm = hb.shape[0]
    nz = z_ref.shape[1]
    for n0 in range(0, nz, n_chunk):
        z_ref[:, n0:n0 + n_chunk] = _dot(hb, w_ref[:, n0:n0 + n_chunk])
    dt_ref[...] = _dot(hb, wdt_ref[...])
    row = lax.broadcasted_iota(jnp.int32, (tm, n_chunk), 0)
    keep_prev = first_ref[i] == 0
    keep_next = last_ref[i] == 0
    for n0 in range(0, xbc_ref.shape[1], n_chunk):
        cs = slice(n0, n0 + n_chunk)
        t = _dot(hb_ext, w_ref[:, nz + n0:nz + n0 + n_chunk])
        cur = t[:tm]
        prev_row = jnp.where(keep_prev, t[tm + 7:tm + 8], 0.0)
        next_row = jnp.where(keep_next, t[tm + 8:tm + 9], 0.0)
        down = jnp.where(row == 0, prev_row, pltpu.roll(cur, 1, 0))
        up = jnp.where(row == tm - 1, next_row, pltpu.roll(cur, tm - 1, 0))
        y = cw_ref[0:1, cs] * down + cw_ref[1:2, cs] * cur + cw_ref[2:3, cs] * up + cb_ref[:, cs]
        xbc_ref[:, cs] = _silu(y)


def _ssd_inproj(xs, g, mods_l, w_main, w_dt, conv_w, conv_b, dims):
    bsz, seq, ctx = dims
    r, d = xs.shape
    tm = ROW_TILE
    lt = seq // tm
    n2 = w_dt.shape[1]
    starts = [b * seq for b in range(bsz)] + [bsz * seq + b * ctx for b in range(bsz)]
    ends = [s + (seq if k < bsz else ctx) for k, s in enumerate(starts)]
    first = np.array([1 if (i * tm) in starts else 0 for i in range(r // tm)], np.int32)
    last = np.array([1 if ((i + 1) * tm) in ends else 0 for i in range(r // tm)], np.int32)
    sub = tm // 8
    nblk8 = r // 8

    def mod_spec(which):
        return pl.BlockSpec((None, None, 1, d), lambda i, f, l: (which, jnp.minimum(i // lt, bsz), 0, 0))

    def const(shape):
        return pl.BlockSpec(shape, lambda i, f, l: (0, 0))

    grid_spec = pltpu.PrefetchScalarGridSpec(
        num_scalar_prefetch=2,
        grid=(r // tm,),
        in_specs=[pl.BlockSpec((tm, d), lambda i, f, l: (i, 0)),
                  pl.BlockSpec((8, d), lambda i, f, l: (jnp.maximum(i * sub - 1, 0), 0)),
                  pl.BlockSpec((8, d), lambda i, f, l: (jnp.minimum((i + 1) * sub, nblk8 - 1), 0)),
                  const((1, d)), mod_spec(0), mod_spec(1),
                  const((d, SSD_INNER + SSD_CONV_DIM)), const((d, n2)),
                  const((3, SSD_CONV_DIM)), const((1, SSD_CONV_DIM))],
        out_specs=[pl.BlockSpec((tm, SSD_INNER), lambda i, f, l: (i, 0)),
                   pl.BlockSpec((tm, SSD_CONV_DIM), lambda i, f, l: (i, 0)),
                   pl.BlockSpec((tm, n2), lambda i, f, l: (i, 0))],
    )
    return pl.pallas_call(
        functools.partial(_ssd_inproj_kernel, n_chunk=512),
        grid_spec=grid_spec,
        out_shape=[jax.ShapeDtypeStruct((r, SSD_INNER), F32),
                   jax.ShapeDtypeStruct((r, SSD_CONV_DIM), F32),
                   jax.ShapeDtypeStruct((r, n2), F32)],
        compiler_params=_params(("arbitrary",)),
        name="ssd_inproj_conv_silu",
    )(jnp.asarray(first), jnp.asarray(last), xs, xs, xs, g.reshape(1, d), mods_l, mods_l, w_main, w_dt,
      conv_w, conv_b.reshape(1, -1))


def _chunk_block(b, d, s, bsz, seq, ctx):
    nc_ctx = ctx // CHUNK
    nc_lat = seq // CHUNK
    in_ctx = s < nc_ctx
    pc = jnp.where(d == 0, s, nc_ctx - 1 - s)
    pls = s - nc_ctx
    plat = jnp.where(d == 0, pls, nc_lat - 1 - pls)
    ctx_blk = bsz * nc_lat + b * nc_ctx + pc
    lat_blk = b * nc_lat + plat
    return jnp.where(in_ctx, ctx_blk, lat_blk)


def _dir_masks(d):
    ii = lax.broadcasted_iota(jnp.int32, (CHUNK, CHUNK), 0)
    jj = lax.broadcasted_iota(jnp.int32, (CHUNK, CHUNK), 1)
    sign = 1 - 2 * d
    return (ii - jj) * sign >= 0


def _ssd_scan_kernel(xf_ref, bf_ref, cf_ref, dtf_ref, xb_ref, bb_ref, cb_ref, dtb_ref, bias_ref, a_ref,
                     yf_ref, yb_ref, st_ref):
    s = pl.program_id(1)

    @pl.when(s == 0)
    def _():
        st_ref[...] = jnp.zeros_like(st_ref)

    _ssd_chunk(0, xf_ref, bf_ref, cf_ref, dtf_ref, bias_ref, a_ref, yf_ref, st_ref)
    _ssd_chunk(1, xb_ref, bb_ref, cb_ref, dtb_ref, bias_ref, a_ref, yb_ref, st_ref)


def _ssd_chunk(d, x_ref, b_ref, c_ref, dt_ref, bias_ref, a_ref, y_ref, st_ref):
    mask = _dir_masks(d)
    tri = jnp.where(mask, 1.0, 0.0).astype(F32)
    dt = _softplus(dt_ref[...] + bias_ref[d])
    dta = dt * a_ref[d]
    la = jnp.dot(tri, dta, precision=HIGHEST, preferred_element_type=F32)
    total = jnp.sum(dta, axis=0, keepdims=True)
    la_t = la.T
    dt_t = dt.T
    ws_t = (jnp.exp(total - la) * dt).T
    e_tot = jnp.exp(total)
    lane = lax.broadcasted_iota(jnp.int32, (CHUNK, LANES), 1)
    lo_half = lane < SSD_HEAD_DIM
    hg = SSD_HEADS // SSD_GROUPS
    for g in range(SSD_GROUPS):
        bg32 = b_ref[:, g * SSD_STATE:(g + 1) * SSD_STATE]
        cg = c_ref[:, g * SSD_STATE:(g + 1) * SSD_STATE].astype(BF16)
        cb = _dot_nt(cg, bg32.astype(BF16))
        bg_t = bg32.T
        for pr in range(hg // 2):
            h0 = g * hg + 2 * pr
            col0 = slice((h0 * SSD_HEAD_DIM), (h0 + 2) * SSD_HEAD_DIM)
            xpb = x_ref[:, col0].astype(BF16)
            ys, sts, la_cols = [], [], []
            for h in (h0, h0 + 1):
                la_col = jnp.broadcast_to(la[:, h:h + 1], (CHUNK, LANES))
                la_cols.append(la_col)
                w = jnp.exp(jnp.where(mask, la_col - la_t[h:h + 1, :], -jnp.inf)) * cb * dt_t[h:h + 1, :]
                ys.append(_dot(w.astype(BF16), xpb))
                sts.append(_dot((bg_t * ws_t[h:h + 1, :]).astype(BF16), xpb))
            st = st_ref[d, g, :, pr * LANES:(pr + 1) * LANES]
            e_pair = jnp.exp(jnp.where(lo_half, la_cols[0], la_cols[1]))
            y_ref[:, col0] = jnp.where(lo_half, ys[0], ys[1]) + e_pair * _dot(cg, st.astype(BF16))
            et_pair = jnp.where(lo_half[0:1], e_tot[:, h0:h0 + 1], e_tot[:, h0 + 1:h0 + 2])
            st_ref[d, g, :, pr * LANES:(pr + 1) * LANES] = et_pair * st + jnp.where(lo_half, sts[0], sts[1])


def _ssd_scan(xbc, dt_raw, dt_bias_pad, a_pad, dims):
    bsz, seq, ctx = dims
    r = xbc.shape[0]
    nsteps = (seq + ctx) // CHUNK
    blk = functools.partial(_chunk_block, bsz=bsz, seq=seq, ctx=ctx)
    xw = SSD_INNER
    def chunk_specs(d):
        return [pl.BlockSpec((CHUNK, xw), lambda b, s: (blk(b, d, s), 0)),
                pl.BlockSpec((CHUNK, SSD_BC), lambda b, s: (blk(b, d, s), xw // SSD_BC)),
                pl.BlockSpec((CHUNK, SSD_BC), lambda b, s: (blk(b, d, s), xw // SSD_BC + 1)),
                pl.BlockSpec((CHUNK, LANES), lambda b, s: (blk(b, d, s), d))]

    const = pl.BlockSpec((2, 1, LANES), lambda b, s: (0, 0, 0))
    y_shape = jax.ShapeDtypeStruct((r, xw), F32)
    return pl.pallas_call(
        _ssd_scan_kernel,
        grid=(bsz, nsteps),
        in_specs=chunk_specs(0) + chunk_specs(1) + [const, const],
        out_specs=[pl.BlockSpec((CHUNK, xw), lambda b, s: (blk(b, 0, s), 0)),
                   pl.BlockSpec((CHUNK, xw), lambda b, s: (blk(b, 1, s), 0))],
        out_shape=[y_shape, y_shape],
        scratch_shapes=[pltpu.VMEM((2, SSD_GROUPS, SSD_STATE, xw // SSD_GROUPS), F32)],
        compiler_params=_params(("arbitrary", "arbitrary")),
        name="ssd_scan",
    )(xbc, xbc, xbc, dt_raw, xbc, xbc, xbc, dt_raw, dt_bias_pad, a_pad)


def _ssd_out_kernel(yf_ref, yb_ref, xc_ref, z_ref, dexp_ref, nw_ref, w_ref, xs_ref, gate_ref, o_ref):
    y = yf_ref[...] + yb_ref[...] + dexp_ref[...] * xc_ref[...]
    y = y * _silu(z_ref[...])
    a = (_rms(y) * nw_ref[...]).astype(BF16)
    o_ref[...] = xs_ref[...] + gate_ref[...] * _dot(a, w_ref[...])


def _ssd_out(y_f, y_b, xbc, t_main, d_exp, norm_w, w_out_bf16, xs, mods_l, dims):
    bsz, seq, ctx = dims
    r, d = xs.shape
    tm = ROW_TILE
    lt = seq // tm
    xw = SSD_INNER
    return pl.pallas_call(
        _ssd_out_kernel,
        grid=(r // tm,),
        in_specs=[pl.BlockSpec((tm, xw), lambda i: (i, 0)),
                  pl.BlockSpec((tm, xw), lambda i: (i, 0)),
                  pl.BlockSpec((tm, xw), lambda i: (i, 0)),
                  pl.BlockSpec((tm, xw), lambda i: (i, 0)),
                  pl.BlockSpec((1, xw), lambda i: (0, 0)),
                  pl.BlockSpec((1, xw), lambda i: (0, 0)),
                  pl.BlockSpec((xw, d), lambda i: (0, 0)),
                  pl.BlockSpec((tm, d), lambda i: (i, 0)),
                  _mod_spec(2, lt, bsz, d)],
        out_specs=pl.BlockSpec((tm, d), lambda i: (i, 0)),
        out_shape=jax.ShapeDtypeStruct((r, d), F32),
        compiler_params=_params(("arbitrary",)),
        name="ssd_gated_norm_outproj",
    )(y_f, y_b, xbc, t_main, d_exp, norm_w.reshape(1, xw), w_out_bf16, xs, mods_l)


def _ssd_layer(xs, mods_l, g1, w_in, conv_w, conv_b, dt_bias, a_log, d_skip, norm_w, w_out, dims):
    d = xs.shape[1]
    main = SSD_INNER + SSD_CONV_DIM
    w_main = w_in[:, :main].astype(BF16)
    w_dt = jnp.zeros((d, 2 * LANES), F32)
    w_dt = w_dt.at[:, :SSD_HEADS].set(w_in[:, main:main + SSD_HEADS])
    w_dt = w_dt.at[:, LANES:LANES + SSD_HEADS].set(w_in[:, main + SSD_HEADS:]).astype(BF16)
    t_z, xbc, dt_raw = _ssd_inproj(xs, g1, mods_l, w_main, w_dt, conv_w, conv_b, dims)
    pad = jnp.zeros((2, 1, LANES - SSD_HEADS), F32)
    bias_pad = jnp.concatenate([dt_bias.astype(F32).reshape(2, 1, SSD_HEADS), pad], axis=-1)
    a_pad = jnp.concatenate([-jnp.exp(a_log.astype(F32)).reshape(2, 1, SSD_HEADS), pad], axis=-1)
    y_f, y_b = _ssd_scan(xbc, dt_raw, bias_pad, a_pad, dims)
    d_exp = jnp.repeat(d_skip.astype(F32), SSD_HEAD_DIM).reshape(1, SSD_INNER)
    return _ssd_out(y_f, y_b, xbc, t_z, d_exp, norm_w, w_out.astype(BF16), xs, mods_l, dims)


def _qkv_rope_kernel(x_ref, g_ref, sh_ref, sc_ref, w_ref, cos_ref, sin_ref, q_ref, k_ref, v_ref, *, n_lat_tiles):
    i = pl.program_id(0)
    hb = _norm_mod(x_ref[...], g_ref[...], sh_ref[...], sc_ref[...]).astype(BF16)
    d = x_ref.shape[1]
    is_ctx = i >= n_lat_tiles
    cos = jnp.where(is_ctx, 1.0, cos_ref[...])
    sin = jnp.where(is_ctx, 0.0, sin_ref[...])
    nrep = d // LANES
    cos = jnp.concatenate([cos] * nrep, axis=1)
    sin = jnp.concatenate([sin] * nrep, axis=1)

    def mm(c):
        return _dot(hb, w_ref[:, c * d:(c + 1) * d])

    q = mm(0) * cos + mm(3) * sin
    q_ref[...] = (q * (DIFF_HEAD_DIM ** -0.5 * LOG2_E)).astype(BF16)
    k_ref[...] = (mm(1) * cos + mm(4) * sin).astype(BF16)
    v_ref[...] = mm(2).astype(BF16)


def _rope_tables(seq):
    rows = seq // GRID_W
    row = jnp.repeat(jnp.arange(rows, dtype=F32), GRID_W)
    col = jnp.tile(jnp.arange(GRID_W, dtype=F32), rows)
    inv = ROPE_BASE ** (-jnp.arange(ROPE_Q, dtype=F32) / ROPE_Q)
    ang_r = row[:, None] * inv
    ang_c = col[:, None] * inv
    cos = jnp.concatenate([jnp.cos(ang_r), jnp.cos(ang_r), jnp.cos(ang_c), jnp.cos(ang_c)], axis=1)
    sin = jnp.concatenate([-jnp.sin(ang_r), jnp.sin(ang_r), -jnp.sin(ang_c), jnp.sin(ang_c)], axis=1)
    return jnp.tile(cos, (1, 2)), jnp.tile(sin, (1, 2))


def _rope_partner_cols(d):
    col = np.arange(d)
    within = col % (2 * ROPE_Q)
    return np.where(within < ROPE_Q, col + ROPE_Q, col - ROPE_Q)


def _qkv_rope(xs, g, mods_l, w_qkv, dims):
    bsz, seq, ctx = dims
    r, d = xs.shape
    tm = _wide_tile(dims)
    lt = seq // tm
    perm = _rope_partner_cols(d)
    wq, wk, wv = w_qkv[:, :d], w_qkv[:, d:2 * d], w_qkv[:, 2 * d:]
    w_all = jnp.concatenate([wq, wk, wv, wq[:, perm], wk[:, perm]], axis=1).astype(BF16)
    cos, sin = _rope_tables(seq)
    out = jax.ShapeDtypeStruct((r, d), BF16)
    row_spec = pl.BlockSpec((tm, d), lambda i: (i, 0))
    tab_spec = pl.BlockSpec((tm, LANES), lambda i: (i % lt, 0))
    return pl.pallas_call(
        functools.partial(_qkv_rope_kernel, n_lat_tiles=bsz * lt),
        grid=(r // tm,),
        in_specs=[row_spec, pl.BlockSpec((1, d), lambda i: (0, 0)),
                  _mod_spec(0, lt, bsz, d), _mod_spec(1, lt, bsz, d),
                  pl.BlockSpec((d, 5 * d), lambda i: (0, 0)), tab_spec, tab_spec],
        out_specs=[row_spec, row_spec, row_spec],
        out_shape=[out, out, out],
        compiler_params=_params(("arbitrary",)),
        name="norm_mod_qkv_rope",
    )(xs, g.reshape(1, d), mods_l, mods_l, w_all, cos, sin)


def _lane_fold(x, op):
    parts = [x[:, t * LANES:(t + 1) * LANES] for t in range(x.shape[1] // LANES)]
    return functools.reduce(op, parts)


def _attn_kernel(lam_ref, q_ref, kc_ref, vc_ref, kx_ref, vx_ref, nw_ref, o_ref,
                 sc_ref, sx_ref, acc_ref, m_ref, l_ref, *, n_lat_q, tk, unroll, lambda_init):
    i = pl.program_id(2)
    q = q_ref[...]
    tq = q.shape[0]
    lane = lax.broadcasted_iota(jnp.int32, q.shape, 1)
    zero = jnp.zeros_like(q)
    qq = jnp.concatenate([jnp.where(lane < DIFF_HEAD_DIM, q, zero),
                          jnp.where(lane >= DIFF_HEAD_DIM, q, zero)], axis=0)
    is_latent = i < n_lat_q
    n_x = kx_ref.shape[0] // tk

    s = _dot_nt(qq, kc_ref[...])
    sc_ref[...] = s
    m_ref[...] = _lane_fold(s, jnp.maximum)

    @pl.when(is_latent)
    def _():
        def p1(j, m_):
            kj = kx_ref[pl.ds(pl.multiple_of(j * tk, tk), tk), :]
            sj = _dot_nt(qq, kj)
            sx_ref[j] = sj
            return jnp.maximum(m_, _lane_fold(sj, jnp.maximum))

        m_ref[...] = lax.fori_loop(0, n_x, p1, m_ref[...], unroll=unroll)

    mrow = jnp.max(m_ref[...], axis=1, keepdims=True)

    p = jnp.exp2(sc_ref[...] - mrow)
    acc_ref[...] = _dot(p.astype(BF16), vc_ref[...])
    l_ref[...] = _lane_fold(p, jnp.add)

    @pl.when(is_latent)
    def _():
        def p2(j, l_):
            vj = vx_ref[pl.ds(pl.multiple_of(j * tk, tk), tk), :]
            pj = jnp.exp2(sx_ref[j] - mrow)
            acc_ref[...] += _dot(pj.astype(BF16), vj)
            return l_ + _lane_fold(pj, jnp.add)

        l_ref[...] = lax.fori_loop(0, n_x, p2, l_ref[...], unroll=unroll)

    on = acc_ref[...] / jnp.sum(l_ref[...], axis=1, keepdims=True)
    o = on[:tq] - lam_ref[0] * on[tq:]
    o = _rms(o) * nw_ref[...] * (1.0 - lambda_init)
    o_ref[...] = o.astype(o_ref.dtype)


def _diff_attention(q, k, v, lam_full, subln_w, lambda_init, dims):
    bsz, seq, ctx = dims
    r, d = q.shape
    tq = ATTN_Q_TILE
    tk = min(ATTN_K_TILE, seq)
    n_lat_q = seq // tq
    n_ctx_q = ctx // tq
    nq = n_lat_q + n_ctx_q

    def q_idx(b, h, i):
        return (jnp.where(i < n_lat_q, b * n_lat_q + i, bsz * n_lat_q + b * n_ctx_q + (i - n_lat_q)), h)

    ctx_spec = pl.BlockSpec((ctx, LANES), lambda b, h, i: (bsz * seq // ctx + b, h))
    lat_spec = pl.BlockSpec((seq, LANES), lambda b, h, i: (b, h))
    return pl.pallas_call(
        functools.partial(_attn_kernel, n_lat_q=n_lat_q, tk=tk, unroll=min(ATTN_UNROLL, seq // tk),
                          lambda_init=lambda_init),
        grid=(bsz, DIFF_HEADS, nq),
        in_specs=[pl.BlockSpec(memory_space=pltpu.SMEM),
                  pl.BlockSpec((tq, LANES), q_idx),
                  ctx_spec, ctx_spec, lat_spec, lat_spec,
                  pl.BlockSpec((1, LANES), lambda b, h, i: (0, 0))],
        out_specs=pl.BlockSpec((tq, LANES), q_idx),
        out_shape=jax.ShapeDtypeStruct((r, d), BF16),
        scratch_shapes=[pltpu.VMEM((2 * tq, ctx), F32),
                        pltpu.VMEM((seq // tk, 2 * tq, tk), F32),
                        pltpu.VMEM((2 * tq, LANES), F32),
                        pltpu.VMEM((2 * tq, LANES), F32),
                        pltpu.VMEM((2 * tq, LANES), F32)],
        compiler_params=_params(("arbitrary", "arbitrary", "arbitrary")),
        name="diff_attention",
    )(lam_full, q, k, v, k, v, subln_w.reshape(1, LANES))


def _proj_res_kernel(a_ref, w_ref, xs_ref, gate_ref, o_ref):
    o_ref[...] = xs_ref[...] + gate_ref[...] * _dot(a_ref[...], w_ref[...])


def _proj_res(a_bf16, w_bf16, xs, mods_l, dims):
    bsz, seq, ctx = dims
    r, d = xs.shape
    kdim = a_bf16.shape[1]
    tm = _wide_tile(dims)
    lt = seq // tm
    return pl.pallas_call(
        _proj_res_kernel,
        grid=(r // tm,),
        in_specs=[pl.BlockSpec((tm, kdim), lambda i: (i, 0)),
                  pl.BlockSpec((kdim, d), lambda i: (0, 0)),
                  pl.BlockSpec((tm, d), lambda i: (i, 0)),
                  _mod_spec(2, lt, bsz, d)],
        out_specs=pl.BlockSpec((tm, d), lambda i: (i, 0)),
        out_shape=jax.ShapeDtypeStruct((r, d), F32),
        compiler_params=_params(("arbitrary",)),
        name="outproj_gate_residual",
    )(a_bf16, w_bf16, xs, mods_l)


def _diff_layer(xs, mods_l, g1, w_qkv, lam, subln_w, w_out, lambda_init, dims):
    q, k, v = _qkv_rope(xs, g1, mods_l, w_qkv, dims)
    lam32 = lam.astype(F32)
    lam_full = (jnp.exp(jnp.sum(lam32[0] * lam32[1])) - jnp.exp(jnp.sum(lam32[2] * lam32[3]))
                + lambda_init).reshape(1)
    o = _diff_attention(q, k, v, lam_full, subln_w, lambda_init, dims)
    return _proj_res(o, w_out.astype(BF16), xs, mods_l, dims)


def _mlstm_scan_kernel(q_ref, k_ref, v_ref, g_ref, bias_ref, h_ref, c_ref, n_ref, m_ref):
    d = pl.program_id(1)
    s = pl.program_id(2)

    @pl.when(s == 0)
    def _():
        c_ref[...] = jnp.zeros_like(c_ref)
        n_ref[...] = jnp.zeros_like(n_ref)
        m_ref[...] = jnp.zeros_like(m_ref)

    mask = _dir_masks(d)
    tri = jnp.where(mask, 1.0, 0.0).astype(F32)
    gates = g_ref[...] + bias_ref[...]
    ig = gates[:, 0:ML_HEADS]
    lf = _log_sigmoid(gates)
    bcum = jnp.dot(tri, lf, precision=HIGHEST, preferred_element_type=F32)
    btot = jnp.sum(lf, axis=0, keepdims=True)
    bcum_t = bcum.T
    gates_t = gates.T
    row_last = jnp.where(d == 0, CHUNK - 1, 0)
    rsel = lax.broadcasted_iota(jnp.int32, (CHUNK, 1), 0) == row_last
    for h in range(ML_HEADS):
        fcol = ML_HEADS + h
        m_prev = m_ref[h:h + 1, 0:1]
        bcol = bcum[:, fcol:fcol + 1]
        brow = bcum_t[fcol:fcol + 1, :]
        irow = gates_t[h:h + 1, :]
        icol = ig[:, h:h + 1]
        gcol = bcol + m_prev
        dmat = jnp.where(mask, bcol - brow + irow, -jnp.inf)
        mt = jnp.maximum(gcol, jnp.max(dmat, axis=1, keepdims=True))
        q32 = q_ref[:, h * ML_QK_DIM:(h + 1) * ML_QK_DIM] * (ML_QK_DIM ** -0.5)
        qh = q32.astype(BF16)
        kh32 = k_ref[:, h * ML_QK_DIM:(h + 1) * ML_QK_DIM]
        kh = kh32.astype(BF16)
        vh = v_ref[:, h * ML_V_DIM:(h + 1) * ML_V_DIM].astype(BF16)
        sm = _dot_nt(qh, kh) * jnp.exp(dmat - mt)
        inter = jnp.exp(gcol - mt)
        cst = c_ref[h]
        nst = n_ref[h:h + 1, :]
        num = _dot(sm.astype(BF16), vh) + inter * _dot(qh, cst.astype(BF16))
        qn = jnp.sum(q32 * nst, axis=1, keepdims=True)
        den = jnp.sum(sm, axis=1, keepdims=True) + inter * qn
        h_ref[:, h * ML_V_DIM:(h + 1) * ML_V_DIM] = num / jnp.maximum(jnp.abs(den), jnp.exp(-mt))
        m_new = jnp.sum(jnp.where(rsel, mt, 0.0), axis=0, keepdims=True)
        btot_h = btot[:, fcol:fcol + 1]
        wk = jnp.exp(btot_h - bcol + icol - m_new)
        cscale = jnp.exp(btot_h + m_prev - m_new)
        kw = kh32 * wk
        c_ref[h] = cscale * cst + _dot_tn(kw.astype(BF16), vh)
        n_ref[h:h + 1, :] = cscale * nst + jnp.sum(kw, axis=0, keepdims=True)
        m_ref[h:h + 1, :] = jnp.broadcast_to(m_new, (1, LANES))


def _mlstm_scan(t_main, gates_raw, bias_pad, dims):
    bsz, seq, ctx = dims
    r = t_main.shape[0]
    nsteps = (seq + ctx) // CHUNK
    blk = functools.partial(_chunk_block, bsz=bsz, seq=seq, ctx=ctx)
    return pl.pallas_call(
        _mlstm_scan_kernel,
        grid=(bsz, 2, nsteps),
        in_specs=[pl.BlockSpec((CHUNK, ML_QK), lambda b, d, s: (blk(b, d, s), 0)),
                  pl.BlockSpec((CHUNK, ML_QK), lambda b, d, s: (blk(b, d, s), 1)),
                  pl.BlockSpec((CHUNK, ML_V), lambda b, d, s: (blk(b, d, s), 1)),
                  pl.BlockSpec((CHUNK, LANES), lambda b, d, s: (blk(b, d, s), d)),
                  pl.BlockSpec((None, 1, LANES), lambda b, d, s: (d, 0, 0))],
        out_specs=pl.BlockSpec((None, CHUNK, ML_V), lambda b, d, s: (d, blk(b, d, s), 0)),
        out_shape=jax.ShapeDtypeStruct((2, r, ML_V), F32),
        scratch_shapes=[pltpu.VMEM((ML_HEADS, ML_QK_DIM, ML_V_DIM), F32),
                        pltpu.VMEM((8, ML_QK_DIM), F32),
                        pltpu.VMEM((8, LANES), F32)],
        compiler_params=_params(("arbitrary", "arbitrary", "arbitrary")),
        name="mlstm_scan",
    )(t_main, t_main, t_main, gates_raw, bias_pad)


def _mlstm_out_kernel(hf_ref, hb_ref, o_ref_in, nw_ref, w_ref, xs_ref, gate_ref, out_ref):
    u = None
    for h in range(ML_HEADS):
        cs = slice(h * ML_V_DIM, (h + 1) * ML_V_DIM)
        a = _sigmoid(o_ref_in[:, cs]) * (hf_ref[:, cs] + hb_ref[:, cs])
        a = (_rms(a) * nw_ref[...]).astype(BF16)
        part = _dot(a, w_ref[cs, :])
        u = part if u is None else u + part
    out_ref[...] = xs_ref[...] + gate_ref[...] * u


def _mlstm_out(h2, t_main, norm_w, w_out_bf16, xs, mods_l, dims):
    bsz, seq, ctx = dims
    r, d = xs.shape
    tm = _wide_tile(dims)
    lt = seq // tm
    o_blk = (2 * ML_QK + ML_V) // ML_V
    return pl.pallas_call(
        _mlstm_out_kernel,
        grid=(r // tm,),
        in_specs=[pl.BlockSpec((None, tm, ML_V), lambda i: (0, i, 0)),
                  pl.BlockSpec((None, tm, ML_V), lambda i: (1, i, 0)),
                  pl.BlockSpec((tm, ML_V), lambda i: (i, o_blk)),
                  pl.BlockSpec((1, ML_V_DIM), lambda i: (0, 0)),
                  pl.BlockSpec((ML_V, d), lambda i: (0, 0)),
                  pl.BlockSpec((tm, d), lambda i: (i, 0)),
                  _mod_spec(2, lt, bsz, d)],
        out_specs=pl.BlockSpec((tm, d), lambda i: (i, 0)),
        out_shape=jax.ShapeDtypeStruct((r, d), F32),
        compiler_params=_params(("arbitrary",)),
        name="mlstm_norm_outproj",
    )(h2, h2, t_main, norm_w.reshape(1, ML_V_DIM), w_out_bf16, xs, mods_l)


def _mlstm_layer(xs, mods_l, g1, w_in, b_gates, norm_w, w_out, dims):
    d = xs.shape[1]
    main = 2 * ML_QK + 2 * ML_V
    w_main = w_in[:, :main].astype(BF16)
    wg = w_in[:, main:].reshape(d, 4, ML_HEADS)
    w_g = jnp.zeros((d, 2 * LANES), F32)
    bias = jnp.zeros((2, 1, LANES), F32)
    for dr in range(2):
        w_g = w_g.at[:, dr * LANES:dr * LANES + 2 * ML_HEADS].set(
            wg[:, 2 * dr:2 * dr + 2].reshape(d, 2 * ML_HEADS))
        bias = bias.at[dr, 0, :2 * ML_HEADS].set(b_gates.astype(F32)[2 * dr:2 * dr + 2].reshape(2 * ML_HEADS))
    t_main, gates_raw = _inproj(xs, g1, mods_l, w_main, w_g.astype(BF16), dims)
    h2 = _mlstm_scan(t_main, gates_raw, bias, dims)
    return _mlstm_out(h2, t_main, norm_w, w_out.astype(BF16), xs, mods_l, dims)


def _router_kernel(x_ref, g_ref, sh_ref, sc_ref, wr_ref, br_ref, h_ref, idx_ref, gate_ref, rank_ref, cnt_ref,
                   carry_ref):
    i = pl.program_id(0)

    @pl.when(i == 0)
    def _():
        carry_ref[...] = jnp.zeros_like(carry_ref)

    h = _norm_mod(x_ref[...], g_ref[...], sh_ref[...], sc_ref[...])
    tm, d = h.shape
    h_ref[...] = _pack_bf16_pairs(h)
    logits = lax.dot_general(wr_ref[...], h, (((1,), (1,)), ((), ())), precision=HIGHEST,
                             preferred_element_type=F32) + br_ref[...]
    eidx = lax.broadcasted_iota(jnp.int32, logits.shape, 0)
    work = logits
    vals, idxs = [], []
    picked = jnp.zeros(logits.shape, F32)
    for _ in range(TOP_K):
        mx = jnp.max(work, axis=0, keepdims=True)
        ix = jnp.min(jnp.where(work == mx, eidx, N_EXPERTS), axis=0, keepdims=True)
        sel = eidx == ix
        vals.append(mx)
        idxs.append(ix)
        picked = jnp.where(sel, 1.0, picked)
        work = jnp.where(sel, -jnp.inf, work)
    es = [jnp.exp(v - vals[0]) for v in vals]
    tot = es[0] + es[1] + es[2] + es[3]
    jj = lax.broadcasted_iota(jnp.int32, (tm, tm), 0)
    ii = lax.broadcasted_iota(jnp.int32, (tm, tm), 1)
    upper = jnp.where(jj <= ii, 1.0, 0.0).astype(BF16)
    incl = _dot(picked.astype(BF16), upper)
    carry = carry_ref[:, 0:1]
    excl = incl - picked + carry
    for k in range(TOP_K):
        idx_ref[k:k + 1, :] = idxs[k]
        gate_ref[k:k + 1, :] = es[k] / tot
        rk = jnp.sum(jnp.where(eidx == idxs[k], excl, 0.0), axis=0, keepdims=True)
        rank_ref[k:k + 1, :] = rk.astype(jnp.int32)
    new_carry = carry + jnp.sum(picked, axis=1, keepdims=True)
    carry_ref[...] = jnp.broadcast_to(new_carry, carry_ref.shape)
    cnt_ref[...] = jnp.broadcast_to(new_carry, cnt_ref.shape).astype(jnp.int32)


def _router(xs, g2, mods_l, w_router, b_router, dims):
    bsz, seq, ctx = dims
    r, d = xs.shape
    tm = _wide_tile(dims)
    lt = seq // tm
    tok_spec = pl.BlockSpec((TOP_K, tm), lambda i: (0, i))
    return pl.pallas_call(
        _router_kernel,
        grid=(r // tm,),
        in_specs=[pl.BlockSpec((tm, d), lambda i: (i, 0)),
                  pl.BlockSpec((1, d), lambda i: (0, 0)),
                  _mod_spec(3, lt, bsz, d), _mod_spec(4, lt, bsz, d),
                  pl.BlockSpec((N_EXPERTS, d), lambda i: (0, 0)),
                  pl.BlockSpec((N_EXPERTS, 1), lambda i: (0, 0))],
        out_specs=[pl.BlockSpec((tm, d // 2), lambda i: (i, 0)), tok_spec, tok_spec, tok_spec,
                   pl.BlockSpec((N_EXPERTS, LANES), lambda i: (0, 0))],
        out_shape=[jax.ShapeDtypeStruct((r, d // 2), jnp.int32),
                   jax.ShapeDtypeStruct((TOP_K, r), jnp.int32),
                   jax.ShapeDtypeStruct((TOP_K, r), F32),
                   jax.ShapeDtypeStruct((TOP_K, r), jnp.int32),
                   jax.ShapeDtypeStruct((N_EXPERTS, LANES), jnp.int32)],
        scratch_shapes=[pltpu.VMEM((N_EXPERTS, LANES), F32)],
        compiler_params=_params(("arbitrary",)),
        name="norm_mod_router_top4",
    )(xs, g2.reshape(1, d), mods_l, mods_l, w_router.T, b_router.reshape(N_EXPERTS, 1))


def _pack_bf16_pairs(x):
    w = x.shape[1]
    xr = x.astype(BF16).astype(F32)
    hi = lax.bitcast_convert_type(xr[:, :w // 2], jnp.int32)
    lo = lax.bitcast_convert_type(xr[:, w // 2:], jnp.int32)
    return hi | lax.shift_right_logical(lo, 16)


def _unpack_bf16_pairs(xp):
    hi = lax.bitcast_convert_type(xp & jnp.int32(-65536), F32)
    lo = lax.bitcast_convert_type(lax.shift_left(xp, 16), F32)
    return hi, lo


def _expert_kernel(be_ref, nb_ref, first_ref, slot_ref, next_ref, x_ref, wgu_hbm, bgu_ref, wdn_hbm, bdn_ref, y_ref,
                   wgu_buf, wdn_buf, wgu_bf, wdn_bf, sems, *, layer):
    def fetch(e, s):
        return (pltpu.make_async_copy(wgu_hbm.at[layer, e], wgu_buf.at[s], sems.at[s, 0]),
                pltpu.make_async_copy(wdn_hbm.at[layer, e], wdn_buf.at[s], sems.at[s, 1]))

    def block(i, rows):
        active = i < nb_ref[0]
        slot = slot_ref[i]
        e = be_ref[i]

        @pl.when(jnp.logical_and(active, first_ref[i] == 1))
        def _():
            @pl.when(i == 0)
            def _():
                for cp in fetch(be_ref[0], 0):
                    cp.start()

            for cp in fetch(e, slot):
                cp.wait()
            wgu_bf[...] = wgu_buf[slot].astype(BF16)
            wdn_bf[...] = wdn_buf[slot].astype(BF16)

            @pl.when(next_ref[i] >= 0)
            def _():
                for cp in fetch(next_ref[i], 1 - slot):
                    cp.start()

        @pl.when(active)
        def _():
            de = wdn_bf.shape[0]
            half = x_ref.shape[1]
            xa, xb = _unpack_bf16_pairs(x_ref[rows, :])
            gu = _dot(xa.astype(BF16), wgu_bf[:half, :]) + _dot(xb.astype(BF16), wgu_bf[half:, :]) + bgu_ref[e]
            g = jnp.minimum(gu[:, :de], SWIGLU_LIMIT)
            u = jnp.clip(gu[:, de:], -SWIGLU_LIMIT, SWIGLU_LIMIT)
            a = (u + 1.0) * g * _sigmoid(SWIGLU_ALPHA * g)
            y_ref[rows, :] = _pack_bf16_pairs(_dot(a.astype(BF16), wdn_bf[...]) + bdn_ref[e])

    step = pl.program_id(0)
    for sub in range(MOE_BLOCKS_PER_STEP):
        block(step * MOE_BLOCKS_PER_STEP + sub, slice(sub * MOE_BLOCK, (sub + 1) * MOE_BLOCK))


def _experts(xb, block_expert, nb_used, grp_first, grp_slot, grp_next, w_gu, b_gu, w_dn, b_dn, layer):
    nrows, half = xb.shape
    bm = MOE_BLOCK
    depth, ne, d, two_de = w_gu.shape
    de = two_de // 2
    nsp = 5
    rows_step = bm * MOE_BLOCKS_PER_STEP
    assert nrows % rows_step == 0
    grid_spec = pltpu.PrefetchScalarGridSpec(
        num_scalar_prefetch=nsp,
        grid=(nrows // rows_step,),
        in_specs=[pl.BlockSpec((rows_step, half), lambda i, *_: (i, 0)),
                  pl.BlockSpec(memory_space=pl.ANY),
                  pl.BlockSpec((None, ne, 1, two_de), lambda i, *_: (layer, 0, 0, 0)),
                  pl.BlockSpec(memory_space=pl.ANY),
                  pl.BlockSpec((None, ne, 1, d), lambda i, *_: (layer, 0, 0, 0))],
        out_specs=pl.BlockSpec((rows_step, d // 2), lambda i, *_: (i, 0)),
        scratch_shapes=[pltpu.VMEM((2, d, two_de), F32), pltpu.VMEM((2, de, d), F32),
                        pltpu.VMEM((d, two_de), BF16), pltpu.VMEM((de, d), BF16),
                        pltpu.SemaphoreType.DMA((2, 2))],
    )
    return pl.pallas_call(
        functools.partial(_expert_kernel, layer=layer),
        grid_spec=grid_spec,
        out_shape=jax.ShapeDtypeStruct((nrows, d // 2), jnp.int32),
        compiler_params=_params(("arbitrary",)),
        name="moe_expert_ffn",
    )(block_expert, nb_used, grp_first, grp_slot, grp_next, xb, w_gu, b_gu.reshape(depth, ne, 1, two_de),
      w_dn, b_dn.reshape(depth, ne, 1, d))


def _sc_row_tokens(dest, n_rows):
    top_k, r = dest.shape
    lanes = SC_LANES
    chunk = r // 2
    assert n_rows % lanes == 0 and r % 2 == 0 and chunk % lanes == 0
    fill_mask = (1 << (r.bit_length() - 1)) - 1
    mesh = plsc.VectorSubcoreMesh(core_axis_name="c", subcore_axis_name="s",
                                  num_cores=SC_CORES, num_subcores=SC_SUBCORES)

    def body(dest_hbm, out_hbm, dest_v, table_v):
        wid = lax.axis_index("s") * SC_CORES + lax.axis_index("c")

        @pl.when(wid == 0)
        def _():
            lane_id = lax.iota(jnp.int32, lanes)

            @pl.loop(0, n_rows // lanes, unroll=8)
            def _(c):
                table_v[pl.ds(c * lanes, lanes)] = (lane_id + c * lanes) & fill_mask

            for k in range(top_k):
                for half in range(2):
                    t0 = half * chunk
                    pltpu.sync_copy(dest_hbm.at[pl.ds(k * r + t0, chunk)], dest_v)

                    @pl.loop(0, chunk // lanes, unroll=8)
                    def _(v):
                        d = dest_v[pl.ds(v * lanes, lanes)]
                        plsc.store_scatter(table_v, [d], lane_id + (t0 + v * lanes))

            pltpu.sync_copy(table_v, out_hbm)

    cp = pltpu.CompilerParams(needs_layout_passes=False)
    return pl.kernel(
        body,
        out_type=jax.ShapeDtypeStruct((n_rows,), jnp.int32),
        mesh=mesh,
        scratch_types=[pltpu.VMEM((chunk,), jnp.int32), pltpu.VMEM((n_rows,), jnp.int32)],
        compiler_params=cp,
        name="sc_row_tokens",
    )(dest.reshape(top_k * r))


def _sc_gather_rows(table, idx):
    n = idx.shape[0]
    width = table.shape[1]
    nw = SC_CORES * SC_SUBCORES
    nb = SC_GATHER_ROWS
    per_w = n // nw
    assert n % nw == 0 and per_w % nb == 0, (n, nw, nb)
    steps = per_w // nb
    mesh = plsc.VectorSubcoreMesh(core_axis_name="c", subcore_axis_name="s",
                                  num_cores=SC_CORES, num_subcores=SC_SUBCORES)

    def body(table_hbm, idx_hbm, out_hbm, idx_a, idx_b, rows_a, rows_b, sem_a, sem_b):
        wid = lax.axis_index("s") * SC_CORES + lax.axis_index("c")
        base = wid * per_w
        slots = ((idx_a, rows_a, sem_a), (idx_b, rows_b, sem_b))

        def gather(j, slot):
            idx_v, rows_v, sem = slots[slot]
            return pltpu.make_async_copy(table_hbm.at[idx_v], rows_v, sem)

        def start(j, slot):
            off = pl.multiple_of(base + j * nb, 8)
            pltpu.sync_copy(idx_hbm.at[pl.ds(off, nb)], slots[slot][0])
            gather(j, slot).start()

        def finish(j, slot):
            off = pl.multiple_of(base + j * nb, 8)
            gather(j, slot).wait()
            pltpu.sync_copy(slots[slot][1], out_hbm.at[pl.ds(off, nb)])

        start(0, 0)

        @pl.loop(0, steps // 2)
        def _(p):
            j = 2 * p
            start(j + 1, 1)
            finish(j, 0)
            if steps % 2 == 1:
                start(j + 2, 0)
            else:
                @pl.when(j + 2 < steps)
                def _():
                    start(j + 2, 0)
            finish(j + 1, 1)

        if steps % 2 == 1:
            finish(steps - 1, 0)

    return pl.kernel(
        body,
        out_type=jax.ShapeDtypeStruct((n, width), table.dtype),
        mesh=mesh,
        scratch_types=[pltpu.VMEM((nb,), jnp.int32), pltpu.VMEM((nb,), jnp.int32),
                       pltpu.VMEM((nb, width), table.dtype), pltpu.VMEM((nb, width), table.dtype),
                       pltpu.SemaphoreType.DMA, pltpu.SemaphoreType.DMA],
        name="sc_gather_rows",
    )(table, idx)


def _combine_kernel(xs_ref, yg_ref, gt_ref, gate_ref, fg_ref, o_ref, *, final):
    f_hi = f_lo = None
    for k in range(TOP_K):
        y_hi, y_lo = _unpack_bf16_pairs(yg_ref[k])
        gk = gt_ref[:, k:k + 1]
        f_hi = y_hi * gk if f_hi is None else f_hi + y_hi * gk
        f_lo = y_lo * gk if f_lo is None else f_lo + y_lo * gk
    out = xs_ref[...] + gate_ref[...] * jnp.concatenate([f_hi, f_lo], axis=1)
    if final:
        out = _rms(out) * fg_ref[...]
    o_ref[...] = out


def _combine(xs, yg, gate_t, mods_l, final_g, final, dims):
    bsz, seq, ctx = dims
    r, d = xs.shape
    tm = _wide_tile(dims)
    lt = seq // tm
    rows_out = bsz * seq if final else r
    return pl.pallas_call(
        functools.partial(_combine_kernel, final=final),
        grid=(rows_out // tm,),
        in_specs=[pl.BlockSpec((tm, d), lambda i: (i, 0)),
                  pl.BlockSpec((TOP_K, tm, d // 2), lambda i: (0, i, 0)),
                  pl.BlockSpec((tm, TOP_K), lambda i: (i, 0)),
                  _mod_spec(5, lt, bsz, d),
                  pl.BlockSpec((1, d), lambda i: (0, 0))],
        out_specs=pl.BlockSpec((tm, d), lambda i: (i, 0)),
        out_shape=jax.ShapeDtypeStruct((rows_out, d), F32),
        compiler_params=_params(("arbitrary",)),
        name="moe_combine_residual",
    )(xs, yg, gate_t, mods_l, final_g.reshape(1, d))


def _moe_layer(xs, mods_l, g2, w_router, b_router, w_gu, b_gu, w_dn, b_dn, final_g, layer, final, dims):
    r, d = xs.shape
    bm = MOE_BLOCK
    hp, top_idx, gate, rank, cnt = _router(xs, g2, mods_l, w_router, b_router, dims)
    counts = cnt[:, 0]
    padded = (counts + bm - 1) // bm * bm
    pad_ends = jnp.cumsum(padded)
    pad_starts = pad_ends - padded
    eids = jnp.arange(N_EXPERTS, dtype=jnp.int32)
    onehot = top_idx[None] == eids[:, None, None]
    dest = jnp.sum(jnp.where(onehot, pad_starts[:, None, None], 0), axis=0) + rank
    n_blocks = -(-(r * TOP_K + N_EXPERTS * (bm - 1)) // bm)
    n_blocks = -(-n_blocks // MOE_BLOCKS_PER_STEP) * MOE_BLOCKS_PER_STEP
    blk_start = jnp.arange(n_blocks, dtype=jnp.int32) * bm
    block_expert = jnp.minimum(jnp.sum((pad_ends[None, :] <= blk_start[:, None]).astype(jnp.int32), axis=1),
                               N_EXPERTS - 1)
    nb_total = pad_ends[-1] // bm
    nb_used = nb_total.astype(jnp.int32).reshape(1)
    blk = jnp.arange(n_blocks, dtype=jnp.int32)
    prev_e = jnp.concatenate([jnp.full((1,), -1, jnp.int32), block_expert[:-1]])
    grp_first = ((block_expert != prev_e) & (blk < nb_total)).astype(jnp.int32)
    grp_slot = (jnp.cumsum(grp_first) - 1) % 2
    later = (eids[None, :] > eids[:, None]) & (counts[None, :] > 0)
    next_of_e = jnp.min(jnp.where(later, eids[None, :], N_EXPERTS), axis=1)
    next_of_e = jnp.where(next_of_e == N_EXPERTS, -1, next_of_e)
    grp_next = jnp.sum(jnp.where(block_expert[:, None] == eids[None, :], next_of_e[None, :], 0), axis=1)
    row_tok = _sc_row_tokens(dest, n_blocks * bm)
    xb = _sc_gather_rows(hp, row_tok)
    yb = _experts(xb, block_expert, nb_used, grp_first, grp_slot.astype(jnp.int32), grp_next.astype(jnp.int32),
                  w_gu, b_gu, w_dn, b_dn, layer)
    yg = _sc_gather_rows(yb, dest.reshape(-1)).reshape(TOP_K, r, d // 2)
    return _combine(xs, yg, gate.T, mods_l, final_g, final, dims)


def kernel(x, c, ctx, c_ctx, ada_w, ada_b, norm1_g, norm2_g, ssd_w_in, ssd_conv_w, ssd_conv_b, ssd_dt_bias,
           ssd_a_log, ssd_d, ssd_norm_w, ssd_w_out, diff_w_qkv, diff_lam, diff_subln_w, diff_w_out, ml_w_in,
           ml_b_gates, ml_norm_w, ml_w_out, moe_w_router, moe_b_router, moe_w_gu, moe_b_gu, moe_w_dn, moe_b_dn,
           final_g):
    bsz, seq, d = x.shape
    n_ctx = ctx.shape[1]
    depth = ada_w.shape[0]
    dims = (bsz, seq, n_ctx)
    n_lat = bsz * seq
    xs = jnp.concatenate([x.reshape(n_lat, d), ctx.reshape(bsz * n_ctx, d)], axis=0)
    cond_rows = jnp.zeros((8, d), F32).at[:bsz].set(c).at[bsz].set(c_ctx)
    mods = _mods(cond_rows, ada_w, ada_b)
    mods = mods[:, :, :bsz + 1].reshape(depth, N_MOD, bsz + 1, 1, d)
    for i in range(depth):
        mods_l = mods[i]
        kind, j = i % 3, i // 3
        if kind == 0:
            xs = _ssd_layer(xs, mods_l, norm1_g[i], ssd_w_in[j], ssd_conv_w[j], ssd_conv_b[j], ssd_dt_bias[j],
                            ssd_a_log[j], ssd_d[j], ssd_norm_w[j], ssd_w_out[j], dims)
        elif kind == 1:
            lambda_init = 0.8 - 0.6 * math.exp(-0.3 * i)
            xs = _diff_layer(xs, mods_l, norm1_g[i], diff_w_qkv[j], diff_lam[j], diff_subln_w[j], diff_w_out[j],
                             lambda_init, dims)
        else:
            xs = _mlstm_layer(xs, mods_l, norm1_g[i], ml_w_in[j], ml_b_gates[j], ml_norm_w[j], ml_w_out[j], dims)
        xs = _moe_layer(xs, mods_l, norm2_g[i], moe_w_router[i], moe_b_router[i], moe_w_gu, moe_b_gu,
                        moe_w_dn, moe_b_dn, final_g, i, i == depth - 1, dims)
    return xs.reshape(bsz, seq, d)
```

```python
import functools
import math

import numpy as np
import jax
import jax.numpy as jnp
from jax import lax
from jax.experimental import pallas as pl
from jax.experimental.pallas import tpu as pltpu
from jax.experimental.pallas import tpu_sc as plsc

F32 = jnp.float32
BF16 = jnp.bfloat16
HIGHEST = lax.Precision.HIGHEST

GRID_W = 64
RMS_EPS = 1e-6
N_MOD = 6
SSD_HEAD_DIM = 64
SSD_HEADS = 32
SSD_GROUPS = 4
SSD_STATE = 128
SSD_INNER = SSD_HEADS * SSD_HEAD_DIM
SSD_BC = SSD_GROUPS * SSD_STATE
SSD_CONV_DIM = SSD_INNER + 2 * SSD_BC
DIFF_HEADS = 8
DIFF_HEAD_DIM = 64
DIFF_V_DIM = 128
ROPE_BASE = 10000.0
ROPE_Q = DIFF_HEAD_DIM // 4
ML_HEADS = 4
ML_QK_DIM = 128
ML_V_DIM = 256
ML_QK = ML_HEADS * ML_QK_DIM
ML_V = ML_HEADS * ML_V_DIM
N_EXPERTS = 32
TOP_K = 4
SWIGLU_LIMIT = 7.0
SWIGLU_ALPHA = 1.702
LOG2_E = 1.4426950408889634

LANES = 128
CHUNK = 128
ROW_TILE = 256
ROW_TILE_WIDE = 512
MOE_BLOCK = 256
MOE_BLOCKS_PER_STEP = 2
ATTN_Q_TILE = 256
ATTN_K_TILE = 512
ATTN_UNROLL = 16
SC_CORES = 2
SC_SUBCORES = 16
SC_LANES = 16
SC_GATHER_ROWS = 64
VMEM_LIMIT = 56 * 1024 * 1024


def _wide_tile(dims):
    bsz, seq, ctx = dims
    return min(ROW_TILE_WIDE, seq, bsz * ctx)


def _params(sem):
    return pltpu.CompilerParams(dimension_semantics=sem, vmem_limit_bytes=VMEM_LIMIT)


def _dot(a, b):
    return jnp.dot(a, b, preferred_element_type=F32)


def _dot_nt(a, b):
    return lax.dot_general(a, b, (((1,), (1,)), ((), ())), preferred_element_type=F32)


def _dot_tn(a, b):
    return lax.dot_general(a, b, (((0,), (0,)), ((), ())), preferred_element_type=F32)


def _rms(x):
    return x * lax.rsqrt(jnp.mean(x * x, axis=-1, keepdims=True) + RMS_EPS)


def _sigmoid(x):
    return 1.0 / (1.0 + jnp.exp(-x))


def _silu(x):
    return x * _sigmoid(x)


def _softplus(x):
    return jnp.maximum(x, 0.0) + jnp.log(1.0 + jnp.exp(-jnp.abs(x)))


def _log_sigmoid(x):
    return -_softplus(-x)


def _mod_spec(which, lat_tiles_per_batch, bsz, d):
    return pl.BlockSpec((None, None, 1, d),
                        lambda i: (which, jnp.minimum(i // lat_tiles_per_batch, bsz), 0, 0))


def _mods_kernel(c_ref, w_ref, b_ref, o_ref):
    c = c_ref[...]
    cond = _silu(c)
    o_ref[...] = jnp.dot(cond, w_ref[...], precision=HIGHEST, preferred_element_type=F32) + b_ref[...]


def _mods(cond_rows, ada_w, ada_b):
    depth, d, _ = ada_w.shape
    nr = cond_rows.shape[0]
    return pl.pallas_call(
        _mods_kernel,
        grid=(depth, N_MOD),
        in_specs=[pl.BlockSpec((nr, d), lambda l, j: (0, 0)),
                  pl.BlockSpec((None, d, d), lambda l, j: (l, 0, j)),
                  pl.BlockSpec((None, 1, d), lambda l, j: (l, 0, j))],
        out_specs=pl.BlockSpec((None, None, nr, d), lambda l, j: (l, j, 0, 0)),
        out_shape=jax.ShapeDtypeStruct((depth, N_MOD, nr, d), F32),
        compiler_params=_params(("arbitrary", "arbitrary")),
        name="adaln_mods",
    )(cond_rows, ada_w, ada_b.reshape(depth, 1, N_MOD * d))


def _norm_mod(x, g, sh, sc):
    return (_rms(x) * g) * (1.0 + sc) + sh


def _inproj_kernel(x_ref, g_ref, sh_ref, sc_ref, w_ref, w2_ref, o_ref, o2_ref, *, n_chunk):
    hb = _norm_mod(x_ref[...], g_ref[...], sh_ref[...], sc_ref[...]).astype(BF16)
    n = o_ref.shape[1]
    for n0 in range(0, n, n_chunk):
        o_ref[:, n0:n0 + n_chunk] = _dot(hb, w_ref[:, n0:n0 + n_chunk])
    o2_ref[...] = _dot(hb, w2_ref[...])


def _col_chunk(n):
    for c in (512, 384, 256, 128):
        if n % c == 0:
            return c
    raise ValueError(n)


def _inproj(xs, g, mods_l, w_bf16, w2_bf16, dims):
    bsz, seq, ctx = dims
    r, d = xs.shape
    n = w_bf16.shape[1]
    n2 = w2_bf16.shape[1]
    tm = _wide_tile(dims)
    lt = seq // tm
    return pl.pallas_call(
        functools.partial(_inproj_kernel, n_chunk=_col_chunk(n)),
        grid=(r // tm,),
        in_specs=[pl.BlockSpec((tm, d), lambda i: (i, 0)),
                  pl.BlockSpec((1, d), lambda i: (0, 0)),
                  _mod_spec(0, lt, bsz, d), _mod_spec(1, lt, bsz, d),
                  pl.BlockSpec((d, n), lambda i: (0, 0)),
                  pl.BlockSpec((d, n2), lambda i: (0, 0))],
        out_specs=[pl.BlockSpec((tm, n), lambda i: (i, 0)), pl.BlockSpec((tm, n2), lambda i: (i, 0))],
        out_shape=[jax.ShapeDtypeStruct((r, n), F32), jax.ShapeDtypeStruct((r, n2), F32)],
        compiler_params=_params(("arbitrary",)),
        name="norm_mod_inproj",
    )(xs, g.reshape(1, d), mods_l, mods_l, w_bf16, w2_bf16)


def _ssd_inproj_kernel(first_ref, last_ref, x_ref, xp_ref, xn_ref, g_ref, sh_ref, sc_ref, w_ref, wdt_ref, cw_ref,
                       cb_ref, z_ref, xbc_ref, dt_ref, *, n_chunk):
    i = pl.program_id(0)
    g, sh, sc = g_ref[...], sh_ref[...], sc_ref[...]
    hb = _norm_mod(x_ref[...], g, sh, sc).astype(BF16)
    halo = jnp.concatenate([xp_ref[...], xn_ref[...]], axis=0)
    hb_ext = jnp.concatenate([hb, _norm_mod(halo, g, sh, sc).astype(BF16)], axis=0)
    tm = hb.shape[0]
    nz = z_ref.shape[1]
    for n0 in range(0, nz, n_chunk):
        z_ref[:, n0:n0 + n_chunk] = _dot(hb, w_ref[:, n0:n0 + n_chunk])
    dt_ref[...] = _dot(hb, wdt_ref[...])
    row = lax.broadcasted_iota(jnp.int32, (tm, n_chunk), 0)
    keep_prev = first_ref[i] == 0
    keep_next = last_ref[i] == 0
    for n0 in range(0, xbc_ref.shape[1], n_chunk):
        cs = slice(n0, n0 + n_chunk)
        t = _dot(hb_ext, w_ref[:, nz + n0:nz + n0 + n_chunk])
        cur = t[:tm]
        prev_row = jnp.where(keep_prev, t[tm + 7:tm + 8], 0.0)
        next_row = jnp.where(keep_next, t[tm + 8:tm + 9], 0.0)
        down = jnp.where(row == 0, prev_row, pltpu.roll(cur, 1, 0))
        up = jnp.where(row == tm - 1, next_row, pltpu.roll(cur, tm - 1, 0))
        y = cw_ref[0:1, cs] * down + cw_ref[1:2, cs] * cur + cw_ref[2:3, cs] * up + cb_ref[:, cs]
        xbc_ref[:, cs] = _silu(y).astype(xbc_ref.dtype)


def _ssd_inproj(xs, g, mods_l, w_main, w_dt, conv_w, conv_b, dims):
    bsz, seq, ctx = dims
    r, d = xs.shape
    tm = ROW_TILE
    lt = seq // tm
    n2 = w_dt.shape[1]
    starts = [b * seq for b in range(bsz)] + [bsz * seq + b * ctx for b in range(bsz)]
    ends = [s + (seq if k < bsz else ctx) for k, s in enumerate(starts)]
    first = np.array([1 if (i * tm) in starts else 0 for i in range(r // tm)], np.int32)
    last = np.array([1 if ((i + 1) * tm) in ends else 0 for i in range(r // tm)], np.int32)
    sub = tm // 8
    nblk8 = r // 8

    def mod_spec(which):
        return pl.BlockSpec((None, None, 1, d), lambda i, f, l: (which, jnp.minimum(i // lt, bsz), 0, 0))

    def const(shape):
        return pl.BlockSpec(shape, lambda i, f, l: (0, 0))

    grid_spec = pltpu.PrefetchScalarGridSpec(
        num_scalar_prefetch=2,
        grid=(r // tm,),
        in_specs=[pl.BlockSpec((tm, d), lambda i, f, l: (i, 0)),
                  pl.BlockSpec((8, d), lambda i, f, l: (jnp.maximum(i * sub - 1, 0), 0)),
                  pl.BlockSpec((8, d), lambda i, f, l: (jnp.minimum((i + 1) * sub, nblk8 - 1), 0)),
                  const((1, d)), mod_spec(0), mod_spec(1),
                  const((d, SSD_INNER + SSD_CONV_DIM)), const((d, n2)),
                  const((3, SSD_CONV_DIM)), const((1, SSD_CONV_DIM))],
        out_specs=[pl.BlockSpec((tm, SSD_INNER), lambda i, f, l: (i, 0)),
                   pl.BlockSpec((tm, SSD_CONV_DIM), lambda i, f, l: (i, 0)),
                   pl.BlockSpec((tm, n2), lambda i, f, l: (i, 0))],
    )
    return pl.pallas_call(
        functools.partial(_ssd_inproj_kernel, n_chunk=512),
        grid_spec=grid_spec,
        out_shape=[jax.ShapeDtypeStruct((r, SSD_INNER), F32),
                   jax.ShapeDtypeStruct((r, SSD_CONV_DIM), BF16),
                   jax.ShapeDtypeStruct((r, n2), F32)],
        compiler_params=_params(("arbitrary",)),
        name="ssd_inproj_conv_silu",
    )(jnp.asarray(first), jnp.asarray(last), xs, xs, xs, g.reshape(1, d), mods_l, mods_l, w_main, w_dt,
      conv_w, conv_b.reshape(1, -1))


def _chunk_block(b, d, s, bsz, seq, ctx):
    nc_ctx = ctx // CHUNK
    nc_lat = seq // CHUNK
    in_ctx = s < nc_ctx
    pc = jnp.where(d == 0, s, nc_ctx - 1 - s)
    pls = s - nc_ctx
    plat = jnp.where(d == 0, pls, nc_lat - 1 - pls)
    ctx_blk = bsz * nc_lat + b * nc_ctx + pc
    lat_blk = b * nc_lat + plat
    return jnp.where(in_ctx, ctx_blk, lat_blk)


def _dir_masks(d):
    ii = lax.broadcasted_iota(jnp.int32, (CHUNK, CHUNK), 0)
    jj = lax.broadcasted_iota(jnp.int32, (CHUNK, CHUNK), 1)
    sign = 1 - 2 * d
    return (ii - jj) * sign >= 0


def _ssd_scan_kernel(xf_ref, bf_ref, cf_ref, dtf_ref, xb_ref, bb_ref, cb_ref, dtb_ref, bias_ref, a_ref,
                     yf_ref, yb_ref, st_ref):
    s = pl.program_id(1)

    @pl.when(s == 0)
    def _():
        st_ref[...] = jnp.zeros_like(st_ref)

    _ssd_chunk(0, xf_ref, bf_ref, cf_ref, dtf_ref, bias_ref, a_ref, yf_ref, st_ref)
    _ssd_chunk(1, xb_ref, bb_ref, cb_ref, dtb_ref, bias_ref, a_ref, yb_ref, st_ref)


def _ssd_chunk(d, x_ref, b_ref, c_ref, dt_ref, bias_ref, a_ref, y_ref, st_ref):
    mask = _dir_masks(d)
    tri = jnp.where(mask, 1.0, 0.0).astype(F32)
    dt = _softplus(dt_ref[...] + bias_ref[d])
    dta = dt * a_ref[d]
    la = jnp.dot(tri, dta, precision=HIGHEST, preferred_element_type=F32)
    total = jnp.sum(dta, axis=0, keepdims=True)
    la_t = la.T
    dt_t = dt.T
    ws_t = (jnp.exp(total - la) * dt).T
    e_tot = jnp.exp(total)
    lane = lax.broadcasted_iota(jnp.int32, (CHUNK, LANES), 1)
    lo_half = lane < SSD_HEAD_DIM
    hg = SSD_HEADS // SSD_GROUPS
    for g in range(SSD_GROUPS):
        bg = b_ref[:, g * SSD_STATE:(g + 1) * SSD_STATE]
        cg = c_ref[:, g * SSD_STATE:(g + 1) * SSD_STATE]
        cb = _dot_nt(cg, bg)
        bg_t = bg.astype(F32).T
        for pr in range(hg // 2):
            h0 = g * hg + 2 * pr
            col0 = slice((h0 * SSD_HEAD_DIM), (h0 + 2) * SSD_HEAD_DIM)
            xpb = x_ref[:, col0]
            ys, sts, la_cols = [], [], []
            for h in (h0, h0 + 1):
                la_col = jnp.broadcast_to(la[:, h:h + 1], (CHUNK, LANES))
                la_cols.append(la_col)
                w = jnp.exp(jnp.where(mask, la_col - la_t[h:h + 1, :], -jnp.inf)) * cb * dt_t[h:h + 1, :]
                ys.append(_dot(w.astype(BF16), xpb))
                sts.append(_dot((bg_t * ws_t[h:h + 1, :]).astype(BF16), xpb))
            st = st_ref[d, g, :, pr * LANES:(pr + 1) * LANES]
            e_pair = jnp.exp(jnp.where(lo_half, la_cols[0], la_cols[1]))
            y_pair = jnp.where(lo_half, ys[0], ys[1]) + e_pair * _dot(cg, st.astype(BF16))
            y_ref[:, col0] = y_pair.astype(y_ref.dtype)
            et_pair = jnp.where(lo_half[0:1], e_tot[:, h0:h0 + 1], e_tot[:, h0 + 1:h0 + 2])
            st_ref[d, g, :, pr * LANES:(pr + 1) * LANES] = et_pair * st + jnp.where(lo_half, sts[0], sts[1])


def _ssd_scan(xbc, dt_raw, dt_bias_pad, a_pad, dims):
    bsz, seq, ctx = dims
    r = xbc.shape[0]
    nsteps = (seq + ctx) // CHUNK
    blk = functools.partial(_chunk_block, bsz=bsz, seq=seq, ctx=ctx)
    xw = SSD_INNER
    def chunk_specs(d):
        return [pl.BlockSpec((CHUNK, xw), lambda b, s: (blk(b, d, s), 0)),
                pl.BlockSpec((CHUNK, SSD_BC), lambda b, s: (blk(b, d, s), xw // SSD_BC)),
                pl.BlockSpec((CHUNK, SSD_BC), lambda b, s: (blk(b, d, s), xw // SSD_BC + 1)),
                pl.BlockSpec((CHUNK, LANES), lambda b, s: (blk(b, d, s), d))]

    const = pl.BlockSpec((2, 1, LANES), lambda b, s: (0, 0, 0))
    y_shape = jax.ShapeDtypeStruct((r, xw), BF16)
    return pl.pallas_call(
        _ssd_scan_kernel,
        grid=(bsz, nsteps),
        in_specs=chunk_specs(0) + chunk_specs(1) + [const, const],
        out_specs=[pl.BlockSpec((CHUNK, xw), lambda b, s: (blk(b, 0, s), 0)),
                   pl.BlockSpec((CHUNK, xw), lambda b, s: (blk(b, 1, s), 0))],
        out_shape=[y_shape, y_shape],
        scratch_shapes=[pltpu.VMEM((2, SSD_GROUPS, SSD_STATE, xw // SSD_GROUPS), F32)],
        compiler_params=_params(("arbitrary", "arbitrary")),
        name="ssd_scan",
    )(xbc, xbc, xbc, dt_raw, xbc, xbc, xbc, dt_raw, dt_bias_pad, a_pad)


def _ssd_out_kernel(yf_ref, yb_ref, xc_ref, z_ref, dexp_ref, nw_ref, w_ref, xs_ref, gate_ref, o_ref):
    y = yf_ref[...].astype(F32) + yb_ref[...].astype(F32) + dexp_ref[...] * xc_ref[...].astype(F32)
    y = y * _silu(z_ref[...])
    a = (_rms(y) * nw_ref[...]).astype(BF16)
    o_ref[...] = xs_ref[...] + gate_ref[...] * _dot(a, w_ref[...])


def _ssd_out(y_f, y_b, xbc, t_main, d_exp, norm_w, w_out_bf16, xs, mods_l, dims):
    bsz, seq, ctx = dims
    r, d = xs.shape
    tm = ROW_TILE
    lt = seq // tm
    xw = SSD_INNER
    return pl.pallas_call(
        _ssd_out_kernel,
        grid=(r // tm,),
        in_specs=[pl.BlockSpec((tm, xw), lambda i: (i, 0)),
                  pl.BlockSpec((tm, xw), lambda i: (i, 0)),
                  pl.BlockSpec((tm, xw), lambda i: (i, 0)),
                  pl.BlockSpec((tm, xw), lambda i: (i, 0)),
                  pl.BlockSpec((1, xw), lambda i: (0, 0)),
                  pl.BlockSpec((1, xw), lambda i: (0, 0)),
                  pl.BlockSpec((xw, d), lambda i: (0, 0)),
                  pl.BlockSpec((tm, d), lambda i: (i, 0)),
                  _mod_spec(2, lt, bsz, d)],
        out_specs=pl.BlockSpec((tm, d), lambda i: (i, 0)),
        out_shape=jax.ShapeDtypeStruct((r, d), F32),
        compiler_params=_params(("arbitrary",)),
        name="ssd_gated_norm_outproj",
    )(y_f, y_b, xbc, t_main, d_exp, norm_w.reshape(1, xw), w_out_bf16, xs, mods_l)


def _ssd_layer(xs, mods_l, g1, w_in, conv_w, conv_b, dt_bias, a_log, d_skip, norm_w, w_out, dims):
    d = xs.shape[1]
    main = SSD_INNER + SSD_CONV_DIM
    w_main = w_in[:, :main].astype(BF16)
    w_dt = jnp.zeros((d, 2 * LANES), F32)
    w_dt = w_dt.at[:, :SSD_HEADS].set(w_in[:, main:main + SSD_HEADS])
    w_dt = w_dt.at[:, LANES:LANES + SSD_HEADS].set(w_in[:, main + SSD_HEADS:]).astype(BF16)
    t_z, xbc, dt_raw = _ssd_inproj(xs, g1, mods_l, w_main, w_dt, conv_w, conv_b, dims)
    pad = jnp.zeros((2, 1, LANES - SSD_HEADS), F32)
    bias_pad = jnp.concatenate([dt_bias.astype(F32).reshape(2, 1, SSD_HEADS), pad], axis=-1)
    a_pad = jnp.concatenate([-jnp.exp(a_log.astype(F32)).reshape(2, 1, SSD_HEADS), pad], axis=-1)
    y_f, y_b = _ssd_scan(xbc, dt_raw, bias_pad, a_pad, dims)
    d_exp = jnp.repeat(d_skip.astype(F32), SSD_HEAD_DIM).reshape(1, SSD_INNER)
    return _ssd_out(y_f, y_b, xbc, t_z, d_exp, norm_w, w_out.astype(BF16), xs, mods_l, dims)


def _qkv_rope_kernel(x_ref, g_ref, sh_ref, sc_ref, w_ref, cos_ref, sin_ref, q_ref, k_ref, v_ref, *, n_lat_tiles):
    i = pl.program_id(0)
    hb = _norm_mod(x_ref[...], g_ref[...], sh_ref[...], sc_ref[...]).astype(BF16)
    d = x_ref.shape[1]
    is_ctx = i >= n_lat_tiles
    cos = jnp.where(is_ctx, 1.0, cos_ref[...])
    sin = jnp.where(is_ctx, 0.0, sin_ref[...])
    nrep = d // LANES
    cos = jnp.concatenate([cos] * nrep, axis=1)
    sin = jnp.concatenate([sin] * nrep, axis=1)

    def mm(c):
        return _dot(hb, w_ref[:, c * d:(c + 1) * d])

    q = mm(0) * cos + mm(3) * sin
    q_ref[...] = (q * (DIFF_HEAD_DIM ** -0.5 * LOG2_E)).astype(BF16)
    k_ref[...] = (mm(1) * cos + mm(4) * sin).astype(BF16)
    v_ref[...] = mm(2).astype(BF16)


def _rope_tables(seq):
    rows = seq // GRID_W
    row = jnp.repeat(jnp.arange(rows, dtype=F32), GRID_W)
    col = jnp.tile(jnp.arange(GRID_W, dtype=F32), rows)
    inv = ROPE_BASE ** (-jnp.arange(ROPE_Q, dtype=F32) / ROPE_Q)
    ang_r = row[:, None] * inv
    ang_c = col[:, None] * inv
    cos = jnp.concatenate([jnp.cos(ang_r), jnp.cos(ang_r), jnp.cos(ang_c), jnp.cos(ang_c)], axis=1)
    sin = jnp.concatenate([-jnp.sin(ang_r), jnp.sin(ang_r), -jnp.sin(ang_c), jnp.sin(ang_c)], axis=1)
    return jnp.tile(cos, (1, 2)), jnp.tile(sin, (1, 2))


def _rope_partner_cols(d):
    col = np.arange(d)
    within = col % (2 * ROPE_Q)
    return np.where(within < ROPE_Q, col + ROPE_Q, col - ROPE_Q)


def _qkv_rope(xs, g, mods_l, w_qkv, dims):
    bsz, seq, ctx = dims
    r, d = xs.shape
    tm = _wide_tile(dims)
    lt = seq // tm
    perm = _rope_partner_cols(d)
    wq, wk, wv = w_qkv[:, :d], w_qkv[:, d:2 * d], w_qkv[:, 2 * d:]
    w_all = jnp.concatenate([wq, wk, wv, wq[:, perm], wk[:, perm]], axis=1).astype(BF16)
    cos, sin = _rope_tables(seq)
    out = jax.ShapeDtypeStruct((r, d), BF16)
    row_spec = pl.BlockSpec((tm, d), lambda i: (i, 0))
    tab_spec = pl.BlockSpec((tm, LANES), lambda i: (i % lt, 0))
    return pl.pallas_call(
        functools.partial(_qkv_rope_kernel, n_lat_tiles=bsz * lt),
        grid=(r // tm,),
        in_specs=[row_spec, pl.BlockSpec((1, d), lambda i: (0, 0)),
                  _mod_spec(0, lt, bsz, d), _mod_spec(1, lt, bsz, d),
                  pl.BlockSpec((d, 5 * d), lambda i: (0, 0)), tab_spec, tab_spec],
        out_specs=[row_spec, row_spec, row_spec],
        out_shape=[out, out, out],
        compiler_params=_params(("arbitrary",)),
        name="norm_mod_qkv_rope",
    )(xs, g.reshape(1, d), mods_l, mods_l, w_all, cos, sin)


def _lane_fold(x, op):
    parts = [x[:, t * LANES:(t + 1) * LANES] for t in range(x.shape[1] // LANES)]
    return functools.reduce(op, parts)


def _attn_kernel(lam_ref, q_ref, kc_ref, vc_ref, kx_ref, vx_ref, nw_ref, o_ref,
                 sc_ref, sx_ref, acc_ref, m_ref, l_ref, *, n_lat_q, tk, unroll, lambda_init):
    i = pl.program_id(2)
    q = q_ref[...]
    tq = q.shape[0]
    lane = lax.broadcasted_iota(jnp.int32, q.shape, 1)
    zero = jnp.zeros_like(q)
    qq = jnp.concatenate([jnp.where(lane < DIFF_HEAD_DIM, q, zero),
                          jnp.where(lane >= DIFF_HEAD_DIM, q, zero)], axis=0)
    is_latent = i < n_lat_q
    n_x = kx_ref.shape[0] // tk

    s = _dot_nt(qq, kc_ref[...])
    sc_ref[...] = s
    m_ref[...] = _lane_fold(s, jnp.maximum)

    @pl.when(is_latent)
    def _():
        def p1(j, m_):
            kj = kx_ref[pl.ds(pl.multiple_of(j * tk, tk), tk), :]
            sj = _dot_nt(qq, kj)
            sx_ref[j] = sj
            return jnp.maximum(m_, _lane_fold(sj, jnp.maximum))

        m_ref[...] = lax.fori_loop(0, n_x, p1, m_ref[...], unroll=unroll)

    mrow = jnp.max(m_ref[...], axis=1, keepdims=True)

    p = jnp.exp2(sc_ref[...] - mrow)
    acc_ref[...] = _dot(p.astype(BF16), vc_ref[...])
    l_ref[...] = _lane_fold(p, jnp.add)

    @pl.when(is_latent)
    def _():
        def p2(j, l_):
            vj = vx_ref[pl.ds(pl.multiple_of(j * tk, tk), tk), :]
            pj = jnp.exp2(sx_ref[j] - mrow)
            acc_ref[...] += _dot(pj.astype(BF16), vj)
            return l_ + _lane_fold(pj, jnp.add)

        l_ref[...] = lax.fori_loop(0, n_x, p2, l_ref[...], unroll=unroll)

    on = acc_ref[...] / jnp.sum(l_ref[...], axis=1, keepdims=True)
    o = on[:tq] - lam_ref[0] * on[tq:]
    o = _rms(o) * nw_ref[...] * (1.0 - lambda_init)
    o_ref[...] = o.astype(o_ref.dtype)


def _diff_attention(q, k, v, lam_full, subln_w, lambda_init, dims):
    bsz, seq, ctx = dims
    r, d = q.shape
    tq = ATTN_Q_TILE
    tk = min(ATTN_K_TILE, seq)
    n_lat_q = seq // tq
    n_ctx_q = ctx // tq
    nq = n_lat_q + n_ctx_q

    def q_idx(b, h, i):
        return (jnp.where(i < n_lat_q, b * n_lat_q + i, bsz * n_lat_q + b * n_ctx_q + (i - n_lat_q)), h)

    ctx_spec = pl.BlockSpec((ctx, LANES), lambda b, h, i: (bsz * seq // ctx + b, h))
    lat_spec = pl.BlockSpec((seq, LANES), lambda b, h, i: (b, h))
    return pl.pallas_call(
        functools.partial(_attn_kernel, n_lat_q=n_lat_q, tk=tk, unroll=min(ATTN_UNROLL, seq // tk),
                          lambda_init=lambda_init),
        grid=(bsz, DIFF_HEADS, nq),
        in_specs=[pl.BlockSpec(memory_space=pltpu.SMEM),
                  pl.BlockSpec((tq, LANES), q_idx),
                  ctx_spec, ctx_spec, lat_spec, lat_spec,
                  pl.BlockSpec((1, LANES), lambda b, h, i: (0, 0))],
        out_specs=pl.BlockSpec((tq, LANES), q_idx),
        out_shape=jax.ShapeDtypeStruct((r, d), BF16),
        scratch_shapes=[pltpu.VMEM((2 * tq, ctx), F32),
                        pltpu.VMEM((seq // tk, 2 * tq, tk), F32),
                        pltpu.VMEM((2 * tq, LANES), F32),
                        pltpu.VMEM((2 * tq, LANES), F32),
                        pltpu.VMEM((2 * tq, LANES), F32)],
        compiler_params=_params(("arbitrary", "arbitrary", "arbitrary")),
        name="diff_attention",
    )(lam_full, q, k, v, k, v, subln_w.reshape(1, LANES))


def _proj_res_kernel(a_ref, w_ref, xs_ref, gate_ref, o_ref):
    o_ref[...] = xs_ref[...] + gate_ref[...] * _dot(a_ref[...], w_ref[...])


def _proj_res(a_bf16, w_bf16, xs, mods_l, dims):
    bsz, seq, ctx = dims
    r, d = xs.shape
    kdim = a_bf16.shape[1]
    tm = _wide_tile(dims)
    lt = seq // tm
    return pl.pallas_call(
        _proj_res_kernel,
        grid=(r // tm,),
        in_specs=[pl.BlockSpec((tm, kdim), lambda i: (i, 0)),
                  pl.BlockSpec((kdim, d), lambda i: (0, 0)),
                  pl.BlockSpec((tm, d), lambda i: (i, 0)),
                  _mod_spec(2, lt, bsz, d)],
        out_specs=pl.BlockSpec((tm, d), lambda i: (i, 0)),
        out_shape=jax.ShapeDtypeStruct((r, d), F32),
        compiler_params=_params(("arbitrary",)),
        name="outproj_gate_residual",
    )(a_bf16, w_bf16, xs, mods_l)


def _diff_layer(xs, mods_l, g1, w_qkv, lam, subln_w, w_out, lambda_init, dims):
    q, k, v = _qkv_rope(xs, g1, mods_l, w_qkv, dims)
    lam32 = lam.astype(F32)
    lam_full = (jnp.exp(jnp.sum(lam32[0] * lam32[1])) - jnp.exp(jnp.sum(lam32[2] * lam32[3]))
                + lambda_init).reshape(1)
    o = _diff_attention(q, k, v, lam_full, subln_w, lambda_init, dims)
    return _proj_res(o, w_out.astype(BF16), xs, mods_l, dims)


def _mlstm_scan_kernel(q_ref, k_ref, v_ref, g_ref, bias_ref, h_ref, c_ref, n_ref, m_ref):
    d = pl.program_id(1)
    s = pl.program_id(2)

    @pl.when(s == 0)
    def _():
        c_ref[...] = jnp.zeros_like(c_ref)
        n_ref[...] = jnp.zeros_like(n_ref)
        m_ref[...] = jnp.zeros_like(m_ref)

    mask = _dir_masks(d)
    tri = jnp.where(mask, 1.0, 0.0).astype(F32)
    gates = g_ref[...] + bias_ref[...]
    ig = gates[:, 0:ML_HEADS]
    lf = _log_sigmoid(gates)
    bcum = jnp.dot(tri, lf, precision=HIGHEST, preferred_element_type=F32)
    btot = jnp.sum(lf, axis=0, keepdims=True)
    bcum_t = bcum.T
    gates_t = gates.T
    row_last = jnp.where(d == 0, CHUNK - 1, 0)
    rsel = lax.broadcasted_iota(jnp.int32, (CHUNK, 1), 0) == row_last
    for h in range(ML_HEADS):
        fcol = ML_HEADS + h
        m_prev = m_ref[h:h + 1, 0:1]
        bcol = bcum[:, fcol:fcol + 1]
        brow = bcum_t[fcol:fcol + 1, :]
        irow = gates_t[h:h + 1, :]
        icol = ig[:, h:h + 1]
        gcol = bcol + m_prev
        dmat = jnp.where(mask, bcol - brow + irow, -jnp.inf)
        mt = jnp.maximum(gcol, jnp.max(dmat, axis=1, keepdims=True))
        q32 = q_ref[:, h * ML_QK_DIM:(h + 1) * ML_QK_DIM] * (ML_QK_DIM ** -0.5)
        qh = q32.astype(BF16)
        kh32 = k_ref[:, h * ML_QK_DIM:(h + 1) * ML_QK_DIM]
        kh = kh32.astype(BF16)
        vh = v_ref[:, h * ML_V_DIM:(h + 1) * ML_V_DIM].astype(BF16)
        sm = _dot_nt(qh, kh) * jnp.exp(dmat - mt)
        inter = jnp.exp(gcol - mt)
        cst = c_ref[h]
        nst = n_ref[h:h + 1, :]
        num = _dot(sm.astype(BF16), vh) + inter * _dot(qh, cst.astype(BF16))
        qn = jnp.sum(q32 * nst, axis=1, keepdims=True)
        den = jnp.sum(sm, axis=1, keepdims=True) + inter * qn
        h_ref[:, h * ML_V_DIM:(h + 1) * ML_V_DIM] = num / jnp.maximum(jnp.abs(den), jnp.exp(-mt))
        m_new = jnp.sum(jnp.where(rsel, mt, 0.0), axis=0, keepdims=True)
        btot_h = btot[:, fcol:fcol + 1]
        wk = jnp.exp(btot_h - bcol + icol - m_new)
        cscale = jnp.exp(btot_h + m_prev - m_new)
        kw = kh32 * wk
        c_ref[h] = cscale * cst + _dot_tn(kw.astype(BF16), vh)
        n_ref[h:h + 1, :] = cscale * nst + jnp.sum(kw, axis=0, keepdims=True)
        m_ref[h:h + 1, :] = jnp.broadcast_to(m_new, (1, LANES))


def _mlstm_scan(t_main, gates_raw, bias_pad, dims):
    bsz, seq, ctx = dims
    r = t_main.shape[0]
    nsteps = (seq + ctx) // CHUNK
    blk = functools.partial(_chunk_block, bsz=bsz, seq=seq, ctx=ctx)
    return pl.pallas_call(
        _mlstm_scan_kernel,
        grid=(bsz, 2, nsteps),
        in_specs=[pl.BlockSpec((CHUNK, ML_QK), lambda b, d, s: (blk(b, d, s), 0)),
                  pl.BlockSpec((CHUNK, ML_QK), lambda b, d, s: (blk(b, d, s), 1)),
                  pl.BlockSpec((CHUNK, ML_V), lambda b, d, s: (blk(b, d, s), 1)),
                  pl.BlockSpec((CHUNK, LANES), lambda b, d, s: (blk(b, d, s), d)),
                  pl.BlockSpec((None, 1, LANES), lambda b, d, s: (d, 0, 0))],
        out_specs=pl.BlockSpec((None, CHUNK, ML_V), lambda b, d, s: (d, blk(b, d, s), 0)),
        out_shape=jax.ShapeDtypeStruct((2, r, ML_V), F32),
        scratch_shapes=[pltpu.VMEM((ML_HEADS, ML_QK_DIM, ML_V_DIM), F32),
                        pltpu.VMEM((8, ML_QK_DIM), F32),
                        pltpu.VMEM((8, LANES), F32)],
        compiler_params=_params(("arbitrary", "arbitrary", "arbitrary")),
        name="mlstm_scan",
    )(t_main, t_main, t_main, gates_raw, bias_pad)


def _mlstm_out_kernel(hf_ref, hb_ref, o_ref_in, nw_ref, w_ref, xs_ref, gate_ref, out_ref):
    u = None
    for h in range(ML_HEADS):
        cs = slice(h * ML_V_DIM, (h + 1) * ML_V_DIM)
        a = _sigmoid(o_ref_in[:, cs]) * (hf_ref[:, cs] + hb_ref[:, cs])
        a = (_rms(a) * nw_ref[...]).astype(BF16)
        part = _dot(a, w_ref[cs, :])
        u = part if u is None else u + part
    out_ref[...] = xs_ref[...] + gate_ref[...] * u


def _mlstm_out(h2, t_main, norm_w, w_out_bf16, xs, mods_l, dims):
    bsz, seq, ctx = dims
    r, d = xs.shape
    tm = _wide_tile(dims)
    lt = seq // tm
    o_blk = (2 * ML_QK + ML_V) // ML_V
    return pl.pallas_call(
        _mlstm_out_kernel,
        grid=(r // tm,),
        in_specs=[pl.BlockSpec((None, tm, ML_V), lambda i: (0, i, 0)),
                  pl.BlockSpec((None, tm, ML_V), lambda i: (1, i, 0)),
                  pl.BlockSpec((tm, ML_V), lambda i: (i, o_blk)),
                  pl.BlockSpec((1, ML_V_DIM), lambda i: (0, 0)),
                  pl.BlockSpec((ML_V, d), lambda i: (0, 0)),
                  pl.BlockSpec((tm, d), lambda i: (i, 0)),
                  _mod_spec(2, lt, bsz, d)],
        out_specs=pl.BlockSpec((tm, d), lambda i: (i, 0)),
        out_shape=jax.ShapeDtypeStruct((r, d), F32),
        compiler_params=_params(("arbitrary",)),
        name="mlstm_norm_outproj",
    )(h2, h2, t_main, norm_w.reshape(1, ML_V_DIM), w_out_bf16, xs, mods_l)


def _mlstm_layer(xs, mods_l, g1, w_in, b_gates, norm_w, w_out, dims):
    d = xs.shape[1]
    main = 2 * ML_QK + 2 * ML_V
    w_main = w_in[:, :main].astype(BF16)
    wg = w_in[:, main:].reshape(d, 4, ML_HEADS)
    w_g = jnp.zeros((d, 2 * LANES), F32)
    bias = jnp.zeros((2, 1, LANES), F32)
    for dr in range(2):
        w_g = w_g.at[:, dr * LANES:dr * LANES + 2 * ML_HEADS].set(
            wg[:, 2 * dr:2 * dr + 2].reshape(d, 2 * ML_HEADS))
        bias = bias.at[dr, 0, :2 * ML_HEADS].set(b_gates.astype(F32)[2 * dr:2 * dr + 2].reshape(2 * ML_HEADS))
    t_main, gates_raw = _inproj(xs, g1, mods_l, w_main, w_g.astype(BF16), dims)
    h2 = _mlstm_scan(t_main, gates_raw, bias, dims)
    return _mlstm_out(h2, t_main, norm_w, w_out.astype(BF16), xs, mods_l, dims)


def _router_kernel(x_ref, g_ref, sh_ref, sc_ref, wr_ref, br_ref, h_ref, idx_ref, gate_ref, rank_ref, cnt_ref,
                   carry_ref):
    i = pl.program_id(0)

    @pl.when(i == 0)
    def _():
        carry_ref[...] = jnp.zeros_like(carry_ref)

    h = _norm_mod(x_ref[...], g_ref[...], sh_ref[...], sc_ref[...])
    tm, d = h.shape
    h_ref[...] = _pack_bf16_pairs(h)
    logits = lax.dot_general(wr_ref[...], h, (((1,), (1,)), ((), ())), precision=HIGHEST,
                             preferred_element_type=F32) + br_ref[...]
    eidx = lax.broadcasted_iota(jnp.int32, logits.shape, 0)
    work = logits
    vals, idxs = [], []
    picked = jnp.zeros(logits.shape, F32)
    for _ in range(TOP_K):
        mx = jnp.max(work, axis=0, keepdims=True)
        ix = jnp.min(jnp.where(work == mx, eidx, N_EXPERTS), axis=0, keepdims=True)
        sel = eidx == ix
        vals.append(mx)
        idxs.append(ix)
        picked = jnp.where(sel, 1.0, picked)
        work = jnp.where(sel, -jnp.inf, work)
    es = [jnp.exp(v - vals[0]) for v in vals]
    tot = es[0] + es[1] + es[2] + es[3]
    jj = lax.broadcasted_iota(jnp.int32, (tm, tm), 0)
    ii = lax.broadcasted_iota(jnp.int32, (tm, tm), 1)
    upper = jnp.where(jj <= ii, 1.0, 0.0).astype(BF16)
    incl = _dot(picked.astype(BF16), upper)
    carry = carry_ref[:, 0:1]
    excl = incl - picked + carry
    for k in range(TOP_K):
        idx_ref[k:k + 1, :] = idxs[k]
        gate_ref[k:k + 1, :] = es[k] / tot
        rk = jnp.sum(jnp.where(eidx == idxs[k], excl, 0.0), axis=0, keepdims=True)
        rank_ref[k:k + 1, :] = rk.astype(jnp.int32)
    new_carry = carry + jnp.sum(picked, axis=1, keepdims=True)
    carry_ref[...] = jnp.broadcast_to(new_carry, carry_ref.shape)
    cnt_ref[...] = jnp.broadcast_to(new_carry, cnt_ref.shape).astype(jnp.int32)


def _router(xs, g2, mods_l, w_router, b_router, dims):
    bsz, seq, ctx = dims
    r, d = xs.shape
    tm = _wide_tile(dims)
    lt = seq // tm
    tok_spec = pl.BlockSpec((TOP_K, tm), lambda i: (0, i))
    return pl.pallas_call(
        _router_kernel,
        grid=(r // tm,),
        in_specs=[pl.BlockSpec((tm, d), lambda i: (i, 0)),
                  pl.BlockSpec((1, d), lambda i: (0, 0)),
                  _mod_spec(3, lt, bsz, d), _mod_spec(4, lt, bsz, d),
                  pl.BlockSpec((N_EXPERTS, d), lambda i: (0, 0)),
                  pl.BlockSpec((N_EXPERTS, 1), lambda i: (0, 0))],
        out_specs=[pl.BlockSpec((tm, d // 2), lambda i: (i, 0)), tok_spec, tok_spec, tok_spec,
                   pl.BlockSpec((N_EXPERTS, LANES), lambda i: (0, 0))],
        out_shape=[jax.ShapeDtypeStruct((r, d // 2), jnp.int32),
                   jax.ShapeDtypeStruct((TOP_K, r), jnp.int32),
                   jax.ShapeDtypeStruct((TOP_K, r), F32),
                   jax.ShapeDtypeStruct((TOP_K, r), jnp.int32),
                   jax.ShapeDtypeStruct((N_EXPERTS, LANES), jnp.int32)],
        scratch_shapes=[pltpu.VMEM((N_EXPERTS, LANES), F32)],
        compiler_params=_params(("arbitrary",)),
        name="norm_mod_router_top4",
    )(xs, g2.reshape(1, d), mods_l, mods_l, w_router.T, b_router.reshape(N_EXPERTS, 1))


def _pack_bf16_pairs(x):
    w = x.shape[1]
    xr = x.astype(BF16).astype(F32)
    hi = lax.bitcast_convert_type(xr[:, :w // 2], jnp.int32)
    lo = lax.bitcast_convert_type(xr[:, w // 2:], jnp.int32)
    return hi | lax.shift_right_logical(lo, 16)


def _unpack_bf16_pairs(xp):
    hi = lax.bitcast_convert_type(xp & jnp.int32(-65536), F32)
    lo = lax.bitcast_convert_type(lax.shift_left(xp, 16), F32)
    return hi, lo


def _expert_kernel(be_ref, nb_ref, first_ref, slot_ref, next_ref, x_ref, wgu_hbm, bgu_ref, wdn_hbm, bdn_ref, y_ref,
                   wgu_buf, wdn_buf, wgu_bf, wdn_bf, sems, *, layer):
    def fetch(e, s):
        return (pltpu.make_async_copy(wgu_hbm.at[layer, e], wgu_buf.at[s], sems.at[s, 0]),
                pltpu.make_async_copy(wdn_hbm.at[layer, e], wdn_buf.at[s], sems.at[s, 1]))

    def block(i, rows):
        active = i < nb_ref[0]
        slot = slot_ref[i]
        e = be_ref[i]

        @pl.when(jnp.logical_and(active, first_ref[i] == 1))
        def _():
            @pl.when(i == 0)
            def _():
                for cp in fetch(be_ref[0], 0):
                    cp.start()

            for cp in fetch(e, slot):
                cp.wait()
            wgu_bf[...] = wgu_buf[slot].astype(BF16)
            wdn_bf[...] = wdn_buf[slot].astype(BF16)

            @pl.when(next_ref[i] >= 0)
            def _():
                for cp in fetch(next_ref[i], 1 - slot):
                    cp.start()

        @pl.when(active)
        def _():
            de = wdn_bf.shape[0]
            half = x_ref.shape[1]
            xa, xb = _unpack_bf16_pairs(x_ref[rows, :])
            gu = _dot(xa.astype(BF16), wgu_bf[:half, :]) + _dot(xb.astype(BF16), wgu_bf[half:, :]) + bgu_ref[e]
            g = jnp.minimum(gu[:, :de], SWIGLU_LIMIT)
            u = jnp.clip(gu[:, de:], -SWIGLU_LIMIT, SWIGLU_LIMIT)
            a = (u + 1.0) * g * _sigmoid(SWIGLU_ALPHA * g)
            y_ref[rows, :] = _pack_bf16_pairs(_dot(a.astype(BF16), wdn_bf[...]) + bdn_ref[e])

    step = pl.program_id(0)
    for sub in range(MOE_BLOCKS_PER_STEP):
        block(step * MOE_BLOCKS_PER_STEP + sub, slice(sub * MOE_BLOCK, (sub + 1) * MOE_BLOCK))


def _experts(xb, block_expert, nb_used, grp_first, grp_slot, grp_next, w_gu, b_gu, w_dn, b_dn, layer):
    nrows, half = xb.shape
    bm = MOE_BLOCK
    depth, ne, d, two_de = w_gu.shape
    de = two_de // 2
    nsp = 5
    rows_step = bm * MOE_BLOCKS_PER_STEP
    assert nrows % rows_step == 0
    grid_spec = pltpu.PrefetchScalarGridSpec(
        num_scalar_prefetch=nsp,
        grid=(nrows // rows_step,),
        in_specs=[pl.BlockSpec((rows_step, half), lambda i, *_: (i, 0)),
                  pl.BlockSpec(memory_space=pl.ANY),
                  pl.BlockSpec((None, ne, 1, two_de), lambda i, *_: (layer, 0, 0, 0)),
                  pl.BlockSpec(memory_space=pl.ANY),
                  pl.BlockSpec((None, ne, 1, d), lambda i, *_: (layer, 0, 0, 0))],
        out_specs=pl.BlockSpec((rows_step, d // 2), lambda i, *_: (i, 0)),
        scratch_shapes=[pltpu.VMEM((2, d, two_de), F32), pltpu.VMEM((2, de, d), F32),
                        pltpu.VMEM((d, two_de), BF16), pltpu.VMEM((de, d), BF16),
                        pltpu.SemaphoreType.DMA((2, 2))],
    )
    return pl.pallas_call(
        functools.partial(_expert_kernel, layer=layer),
        grid_spec=grid_spec,
        out_shape=jax.ShapeDtypeStruct((nrows, d // 2), jnp.int32),
        compiler_params=_params(("arbitrary",)),
        name="moe_expert_ffn",
    )(block_expert, nb_used, grp_first, grp_slot, grp_next, xb, w_gu, b_gu.reshape(depth, ne, 1, two_de),
      w_dn, b_dn.reshape(depth, ne, 1, d))


def _sc_row_tokens(dest, n_rows):
    top_k, r = dest.shape
    lanes = SC_LANES
    chunk = r // 2
    assert n_rows % lanes == 0 and r % 2 == 0 and chunk % lanes == 0
    fill_mask = (1 << (r.bit_length() - 1)) - 1
    mesh = plsc.VectorSubcoreMesh(core_axis_name="c", subcore_axis_name="s",
                                  num_cores=SC_CORES, num_subcores=SC_SUBCORES)

    def body(dest_hbm, out_hbm, dest_v, table_v):
        wid = lax.axis_index("s") * SC_CORES + lax.axis_index("c")

        @pl.when(wid == 0)
        def _():
            lane_id = lax.iota(jnp.int32, lanes)

            @pl.loop(0, n_rows // lanes, unroll=8)
            def _(c):
                table_v[pl.ds(c * lanes, lanes)] = (lane_id + c * lanes) & fill_mask

            for k in range(top_k):
                for half in range(2):
                    t0 = half * chunk
                    pltpu.sync_copy(dest_hbm.at[pl.ds(k * r + t0, chunk)], dest_v)

                    @pl.loop(0, chunk // lanes, unroll=8)
                    def _(v):
                        d = dest_v[pl.ds(v * lanes, lanes)]
                        plsc.store_scatter(table_v, [d], lane_id + (t0 + v * lanes))

            pltpu.sync_copy(table_v, out_hbm)

    cp = pltpu.CompilerParams(needs_layout_passes=False)
    return pl.kernel(
        body,
        out_type=jax.ShapeDtypeStruct((n_rows,), jnp.int32),
        mesh=mesh,
        scratch_types=[pltpu.VMEM((chunk,), jnp.int32), pltpu.VMEM((n_rows,), jnp.int32)],
        compiler_params=cp,
        name="sc_row_tokens",
    )(dest.reshape(top_k * r))


def _sc_gather_rows(table, idx):
    n = idx.shape[0]
    width = table.shape[1]
    nw = SC_CORES * SC_SUBCORES
    nb = SC_GATHER_ROWS
    per_w = n // nw
    assert n % nw == 0 and per_w % nb == 0, (n, nw, nb)
    steps = per_w // nb
    mesh = plsc.VectorSubcoreMesh(core_axis_name="c", subcore_axis_name="s",
                                  num_cores=SC_CORES, num_subcores=SC_SUBCORES)

    def body(table_hbm, idx_hbm, out_hbm, idx_a, idx_b, rows_a, rows_b, sem_a, sem_b):
        wid = lax.axis_index("s") * SC_CORES + lax.axis_index("c")
        base = wid * per_w
        slots = ((idx_a, rows_a, sem_a), (idx_b, rows_b, sem_b))

        def gather(j, slot):
            idx_v, rows_v, sem = slots[slot]
            return pltpu.make_async_copy(table_hbm.at[idx_v], rows_v, sem)

        def start(j, slot):
            off = pl.multiple_of(base + j * nb, 8)
            pltpu.sync_copy(idx_hbm.at[pl.ds(off, nb)], slots[slot][0])
            gather(j, slot).start()

        def finish(j, slot):
            off = pl.multiple_of(base + j * nb, 8)
            gather(j, slot).wait()
            pltpu.sync_copy(slots[slot][1], out_hbm.at[pl.ds(off, nb)])

        start(0, 0)

        @pl.loop(0, steps // 2)
        def _(p):
            j = 2 * p
            start(j + 1, 1)
            finish(j, 0)
            if steps % 2 == 1:
                start(j + 2, 0)
            else:
                @pl.when(j + 2 < steps)
                def _():
                    start(j + 2, 0)
            finish(j + 1, 1)

        if steps % 2 == 1:
            finish(steps - 1, 0)

    return pl.kernel(
        body,
        out_type=jax.ShapeDtypeStruct((n, width), table.dtype),
        mesh=mesh,
        scratch_types=[pltpu.VMEM((nb,), jnp.int32), pltpu.VMEM((nb,), jnp.int32),
                       pltpu.VMEM((nb, width), table.dtype), pltpu.VMEM((nb, width), table.dtype),
                       pltpu.SemaphoreType.DMA, pltpu.SemaphoreType.DMA],
        name="sc_gather_rows",
    )(table, idx)


def _combine_kernel(xs_ref, yg_ref, gt_ref, gate_ref, fg_ref, o_ref, *, final):
    f_hi = f_lo = None
    for k in range(TOP_K):
        y_hi, y_lo = _unpack_bf16_pairs(yg_ref[k])
        gk = gt_ref[:, k:k + 1]
        f_hi = y_hi * gk if f_hi is None else f_hi + y_hi * gk
        f_lo = y_lo * gk if f_lo is None else f_lo + y_lo * gk
    out = xs_ref[...] + gate_ref[...] * jnp.concatenate([f_hi, f_lo], axis=1)
    if final:
        out = _rms(out) * fg_ref[...]
    o_ref[...] = out


def _combine(xs, yg, gate_t, mods_l, final_g, final, dims):
    bsz, seq, ctx = dims
    r, d = xs.shape
    tm = _wide_tile(dims)
    lt = seq // tm
    rows_out = bsz * seq if final else r
    return pl.pallas_call(
        functools.partial(_combine_kernel, final=final),
        grid=(rows_out // tm,),
        in_specs=[pl.BlockSpec((tm, d), lambda i: (i, 0)),
                  pl.BlockSpec((TOP_K, tm, d // 2), lambda i: (0, i, 0)),
                  pl.BlockSpec((tm, TOP_K), lambda i: (i, 0)),
                  _mod_spec(5, lt, bsz, d),
                  pl.BlockSpec((1, d), lambda i: (0, 0))],
        out_specs=pl.BlockSpec((tm, d), lambda i: (i, 0)),
        out_shape=jax.ShapeDtypeStruct((rows_out, d), F32),
        compiler_params=_params(("arbitrary",)),
        name="moe_combine_residual",
    )(xs, yg, gate_t, mods_l, final_g.reshape(1, d))


def _moe_layer(xs, mods_l, g2, w_router, b_router, w_gu, b_gu, w_dn, b_dn, final_g, layer, final, dims):
    r, d = xs.shape
    bm = MOE_BLOCK
    hp, top_idx, gate, rank, cnt = _router(xs, g2, mods_l, w_router, b_router, dims)
    counts = cnt[:, 0]
    padded = (counts + bm - 1) // bm * bm
    pad_ends = jnp.cumsum(padded)
    pad_starts = pad_ends - padded
    eids = jnp.arange(N_EXPERTS, dtype=jnp.int32)
    onehot = top_idx[None] == eids[:, None, None]
    dest = jnp.sum(jnp.where(onehot, pad_starts[:, None, None], 0), axis=0) + rank
    n_blocks = -(-(r * TOP_K + N_EXPERTS * (bm - 1)) // bm)
    n_blocks = -(-n_blocks // MOE_BLOCKS_PER_STEP) * MOE_BLOCKS_PER_STEP
    blk_start = jnp.arange(n_blocks, dtype=jnp.int32) * bm
    block_expert = jnp.minimum(jnp.sum((pad_ends[None, :] <= blk_start[:, None]).astype(jnp.int32), axis=1),
                               N_EXPERTS - 1)
    nb_total = pad_ends[-1] // bm
    nb_used = nb_total.astype(jnp.int32).reshape(1)
    blk = jnp.arange(n_blocks, dtype=jnp.int32)
    prev_e = jnp.concatenate([jnp.full((1,), -1, jnp.int32), block_expert[:-1]])
    grp_first = ((block_expert != prev_e) & (blk < nb_total)).astype(jnp.int32)
    grp_slot = (jnp.cumsum(grp_first) - 1) % 2
    later = (eids[None, :] > eids[:, None]) & (counts[None, :] > 0)
    next_of_e = jnp.min(jnp.where(later, eids[None, :], N_EXPERTS), axis=1)
    next_of_e = jnp.where(next_of_e == N_EXPERTS, -1, next_of_e)
    grp_next = jnp.sum(jnp.where(block_expert[:, None] == eids[None, :], next_of_e[None, :], 0), axis=1)
    row_tok = _sc_row_tokens(dest, n_blocks * bm)
    xb = _sc_gather_rows(hp, row_tok)
    yb = _experts(xb, block_expert, nb_used, grp_first, grp_slot.astype(jnp.int32), grp_next.astype(jnp.int32),
                  w_gu, b_gu, w_dn, b_dn, layer)
    yg = _sc_gather_rows(yb, dest.reshape(-1)).reshape(TOP_K, r, d // 2)
    return _combine(xs, yg, gate.T, mods_l, final_g, final, dims)


def kernel(x, c, ctx, c_ctx, ada_w, ada_b, norm1_g, norm2_g, ssd_w_in, ssd_conv_w, ssd_conv_b, ssd_dt_bias,
           ssd_a_log, ssd_d, ssd_norm_w, ssd_w_out, diff_w_qkv, diff_lam, diff_subln_w, diff_w_out, ml_w_in,
           ml_b_gates, ml_norm_w, ml_w_out, moe_w_router, moe_b_router, moe_w_gu, moe_b_gu, moe_w_dn, moe_b_dn,
           final_g):
    bsz, seq, d = x.shape
    n_ctx = ctx.shape[1]
    depth = ada_w.shape[0]
    dims = (bsz, seq, n_ctx)
    n_lat = bsz * seq
    xs = jnp.concatenate([x.reshape(n_lat, d), ctx.reshape(bsz * n_ctx, d)], axis=0)
    cond_rows = jnp.zeros((8, d), F32).at[:bsz].set(c).at[bsz].set(c_ctx)
    mods = _mods(cond_rows, ada_w, ada_b)
    mods = mods[:, :, :bsz + 1].reshape(depth, N_MOD, bsz + 1, 1, d)
    for i in range(depth):
        mods_l = mods[i]
        kind, j = i % 3, i // 3
        if kind == 0:
            xs = _ssd_layer(xs, mods_l, norm1_g[i], ssd_w_in[j], ssd_conv_w[j], ssd_conv_b[j], ssd_dt_bias[j],
                            ssd_a_log[j], ssd_d[j], ssd_norm_w[j], ssd_w_out[j], dims)
        elif kind == 1:
            lambda_init = 0.8 - 0.6 * math.exp(-0.3 * i)
            xs = _diff_layer(xs, mods_l, norm1_g[i], diff_w_qkv[j], diff_lam[j], diff_subln_w[j], diff_w_out[j],
                             lambda_init, dims)
        else:
            xs = _mlstm_layer(xs, mods_l, norm1_g[i], ml_w_in[j], ml_b_gates[j], ml_norm_w[j], ml_w_out[j], dims)
        xs = _moe_layer(xs, mods_l, norm2_g[i], moe_w_router[i], moe_b_router[i], moe_w_gu, moe_b_gu,
                        moe_w_dn, moe_b_dn, final_g, i, i == depth - 1, dims)
    return xs.reshape(bsz, seq, d)
```

```python
import functools
import math

import numpy as np
import jax
import jax.numpy as jnp
from jax import lax
from jax.experimental import pallas as pl
from jax.experimental.pallas import tpu as pltpu
from jax.experimental.pallas import tpu_sc as plsc

F32 = jnp.float32
BF16 = jnp.bfloat16
HIGHEST = lax.Precision.HIGHEST

GRID_W = 64
RMS_EPS = 1e-6
N_MOD = 6
SSD_HEAD_DIM = 64
SSD_HEADS = 32
SSD_GROUPS = 4
SSD_STATE = 128
SSD_INNER = SSD_HEADS * SSD_HEAD_DIM
SSD_BC = SSD_GROUPS * SSD_STATE
SSD_CONV_DIM = SSD_INNER + 2 * SSD_BC
DIFF_HEADS = 8
DIFF_HEAD_DIM = 64
DIFF_V_DIM = 128
ROPE_BASE = 10000.0
ROPE_Q = DIFF_HEAD_DIM // 4
ML_HEADS = 4
ML_QK_DIM = 128
ML_V_DIM = 256
ML_QK = ML_HEADS * ML_QK_DIM
ML_V = ML_HEADS * ML_V_DIM
N_EXPERTS = 32
TOP_K = 4
SWIGLU_LIMIT = 7.0
SWIGLU_ALPHA = 1.702
LOG2_E = 1.4426950408889634

LANES = 128
CHUNK = 128
ROW_TILE = 256
ROW_TILE_WIDE = 512
MOE_BLOCK = 256
MOE_BLOCKS_PER_STEP = 2
ATTN_Q_TILE = 512
ATTN_K_TILE = 512
ATTN_UNROLL = 16
SC_CORES = 2
SC_SUBCORES = 16
SC_LANES = 16
SC_GATHER_ROWS = 64
VMEM_LIMIT = 56 * 1024 * 1024


def _wide_tile(dims):
    bsz, seq, ctx = dims
    return min(ROW_TILE_WIDE, seq, bsz * ctx)


def _params(sem):
    return pltpu.CompilerParams(dimension_semantics=sem, vmem_limit_bytes=VMEM_LIMIT)


def _dot(a, b):
    return jnp.dot(a, b, preferred_element_type=F32)


def _dot_nt(a, b):
    return lax.dot_general(a, b, (((1,), (1,)), ((), ())), preferred_element_type=F32)


def _dot_tn(a, b):
    return lax.dot_general(a, b, (((0,), (0,)), ((), ())), preferred_element_type=F32)


def _rms(x):
    return x * lax.rsqrt(jnp.mean(x * x, axis=-1, keepdims=True) + RMS_EPS)


def _sigmoid(x):
    return 1.0 / (1.0 + jnp.exp(-x))


def _silu(x):
    return x * _sigmoid(x)


def _softplus(x):
    return jnp.maximum(x, 0.0) + jnp.log(1.0 + jnp.exp(-jnp.abs(x)))


def _log_sigmoid(x):
    return -_softplus(-x)


def _mod_spec(which, lat_tiles_per_batch, bsz, d):
    return pl.BlockSpec((None, None, 1, d),
                        lambda i: (which, jnp.minimum(i // lat_tiles_per_batch, bsz), 0, 0))


def _mods_kernel(c_ref, w_ref, b_ref, o_ref):
    c = c_ref[...]
    cond = _silu(c)
    o_ref[...] = jnp.dot(cond, w_ref[...], precision=HIGHEST, preferred_element_type=F32) + b_ref[...]


def _mods(cond_rows, ada_w, ada_b):
    depth, d, _ = ada_w.shape
    nr = cond_rows.shape[0]
    return pl.pallas_call(
        _mods_kernel,
        grid=(depth, N_MOD),
        in_specs=[pl.BlockSpec((nr, d), lambda l, j: (0, 0)),
                  pl.BlockSpec((None, d, d), lambda l, j: (l, 0, j)),
                  pl.BlockSpec((None, 1, d), lambda l, j: (l, 0, j))],
        out_specs=pl.BlockSpec((None, None, nr, d), lambda l, j: (l, j, 0, 0)),
        out_shape=jax.ShapeDtypeStruct((depth, N_MOD, nr, d), F32),
        compiler_params=_params(("arbitrary", "arbitrary")),
        name="adaln_mods",
    )(cond_rows, ada_w, ada_b.reshape(depth, 1, N_MOD * d))


def _norm_mod(x, g, sh, sc):
    return (_rms(x) * g) * (1.0 + sc) + sh


def _inproj_kernel(x_ref, g_ref, sh_ref, sc_ref, w_ref, w2_ref, o_ref, o2_ref, *, n_chunk):
    hb = _norm_mod(x_ref[...], g_ref[...], sh_ref[...], sc_ref[...]).astype(BF16)
    n = o_ref.shape[1]
    for n0 in range(0, n, n_chunk):
        o_ref[:, n0:n0 + n_chunk] = _dot(hb, w_ref[:, n0:n0 + n_chunk])
    o2_ref[...] = _dot(hb, w2_ref[...])


def _col_chunk(n):
    for c in (512, 384, 256, 128):
        if n % c == 0:
            return c
    raise ValueError(n)


def _inproj(xs, g, mods_l, w_bf16, w2_bf16, dims):
    bsz, seq, ctx = dims
    r, d = xs.shape
    n = w_bf16.shape[1]
    n2 = w2_bf16.shape[1]
    tm = _wide_tile(dims)
    lt = seq // tm
    return pl.pallas_call(
        functools.partial(_inproj_kernel, n_chunk=_col_chunk(n)),
        grid=(r // tm,),
        in_specs=[pl.BlockSpec((tm, d), lambda i: (i, 0)),
                  pl.BlockSpec((1, d), lambda i: (0, 0)),
                  _mod_spec(0, lt, bsz, d), _mod_spec(1, lt, bsz, d),
                  pl.BlockSpec((d, n), lambda i: (0, 0)),
                  pl.BlockSpec((d, n2), lambda i: (0, 0))],
        out_specs=[pl.BlockSpec((tm, n), lambda i: (i, 0)), pl.BlockSpec((tm, n2), lambda i: (i, 0))],
        out_shape=[jax.ShapeDtypeStruct((r, n), F32), jax.ShapeDtypeStruct((r, n2), F32)],
        compiler_params=_params(("arbitrary",)),
        name="norm_mod_inproj",
    )(xs, g.reshape(1, d), mods_l, mods_l, w_bf16, w2_bf16)


def _ssd_inproj_kernel(first_ref, last_ref, x_ref, xp_ref, xn_ref, g_ref, sh_ref, sc_ref, w_ref, wdt_ref, cw_ref,
                       cb_ref, z_ref, xbc_ref, dt_ref, *, n_chunk):
    i = pl.program_id(0)
    g, sh, sc = g_ref[...], sh_ref[...], sc_ref[...]
    hb = _norm_mod(x_ref[...], g, sh, sc).astype(BF16)
    halo = jnp.concatenate([xp_ref[...], xn_ref[...]], axis=0)
    hb_ext = jnp.concatenate([hb, _norm_mod(halo, g, sh, sc).astype(BF16)], axis=0)
    tm = hb.shape[0]
    nz = z_ref.shape[1]
    for n0 in range(0, nz, n_chunk):
        z_ref[:, n0:n0 + n_chunk] = _dot(hb, w_ref[:, n0:n0 + n_chunk])
    dt_ref[...] = _dot(hb, wdt_ref[...])
    row = lax.broadcasted_iota(jnp.int32, (tm, n_chunk), 0)
    keep_prev = first_ref[i] == 0
    keep_next = last_ref[i] == 0
    for n0 in range(0, xbc_ref.shape[1], n_chunk):
        cs = slice(n0, n0 + n_chunk)
        t = _dot(hb_ext, w_ref[:, nz + n0:nz + n0 + n_chunk])
        cur = t[:tm]
        prev_row = jnp.where(keep_prev, t[tm + 7:tm + 8], 0.0)
        next_row = jnp.where(keep_next, t[tm + 8:tm + 9], 0.0)
        down = jnp.where(row == 0, prev_row, pltpu.roll(cur, 1, 0))
        up = jnp.where(row == tm - 1, next_row, pltpu.roll(cur, tm - 1, 0))
        y = cw_ref[0:1, cs] * down + cw_ref[1:2, cs] * cur + cw_ref[2:3, cs] * up + cb_ref[:, cs]
        xbc_ref[:, cs] = _silu(y).astype(xbc_ref.dtype)


def _ssd_inproj(xs, g, mods_l, w_main, w_dt, conv_w, conv_b, dims):
    bsz, seq, ctx = dims
    r, d = xs.shape
    tm = ROW_TILE
    lt = seq // tm
    n2 = w_dt.shape[1]
    starts = [b * seq for b in range(bsz)] + [bsz * seq + b * ctx for b in range(bsz)]
    ends = [s + (seq if k < bsz else ctx) for k, s in enumerate(starts)]
    first = np.array([1 if (i * tm) in starts else 0 for i in range(r // tm)], np.int32)
    last = np.array([1 if ((i + 1) * tm) in ends else 0 for i in range(r // tm)], np.int32)
    sub = tm // 8
    nblk8 = r // 8

    def mod_spec(which):
        return pl.BlockSpec((None, None, 1, d), lambda i, f, l: (which, jnp.minimum(i // lt, bsz), 0, 0))

    def const(shape):
        return pl.BlockSpec(shape, lambda i, f, l: (0, 0))

    grid_spec = pltpu.PrefetchScalarGridSpec(
        num_scalar_prefetch=2,
        grid=(r // tm,),
        in_specs=[pl.BlockSpec((tm, d), lambda i, f, l: (i, 0)),
                  pl.BlockSpec((8, d), lambda i, f, l: (jnp.maximum(i * sub - 1, 0), 0)),
                  pl.BlockSpec((8, d), lambda i, f, l: (jnp.minimum((i + 1) * sub, nblk8 - 1), 0)),
                  const((1, d)), mod_spec(0), mod_spec(1),
                  const((d, SSD_INNER + SSD_CONV_DIM)), const((d, n2)),
                  const((3, SSD_CONV_DIM)), const((1, SSD_CONV_DIM))],
        out_specs=[pl.BlockSpec((tm, SSD_INNER), lambda i, f, l: (i, 0)),
                   pl.BlockSpec((tm, SSD_CONV_DIM), lambda i, f, l: (i, 0)),
                   pl.BlockSpec((tm, n2), lambda i, f, l: (i, 0))],
    )
    return pl.pallas_call(
        functools.partial(_ssd_inproj_kernel, n_chunk=512),
        grid_spec=grid_spec,
        out_shape=[jax.ShapeDtypeStruct((r, SSD_INNER), F32),
                   jax.ShapeDtypeStruct((r, SSD_CONV_DIM), BF16),
                   jax.ShapeDtypeStruct((r, n2), F32)],
        compiler_params=_params(("arbitrary",)),
        name="ssd_inproj_conv_silu",
    )(jnp.asarray(first), jnp.asarray(last), xs, xs, xs, g.reshape(1, d), mods_l, mods_l, w_main, w_dt,
      conv_w, conv_b.reshape(1, -1))


def _chunk_block(b, d, s, bsz, seq, ctx):
    nc_ctx = ctx // CHUNK
    nc_lat = seq // CHUNK
    in_ctx = s < nc_ctx
    pc = jnp.where(d == 0, s, nc_ctx - 1 - s)
    pls = s - nc_ctx
    plat = jnp.where(d == 0, pls, nc_lat - 1 - pls)
    ctx_blk = bsz * nc_lat + b * nc_ctx + pc
    lat_blk = b * nc_lat + plat
    return jnp.where(in_ctx, ctx_blk, lat_blk)


def _dir_masks(d):
    ii = lax.broadcasted_iota(jnp.int32, (CHUNK, CHUNK), 0)
    jj = lax.broadcasted_iota(jnp.int32, (CHUNK, CHUNK), 1)
    sign = 1 - 2 * d
    return (ii - jj) * sign >= 0


def _ssd_scan_kernel(xf_ref, bf_ref, cf_ref, dtf_ref, xb_ref, bb_ref, cb_ref, dtb_ref, bias_ref, a_ref,
                     yf_ref, yb_ref, st_ref):
    s = pl.program_id(1)

    @pl.when(s == 0)
    def _():
        st_ref[...] = jnp.zeros_like(st_ref)

    _ssd_chunk(0, xf_ref, bf_ref, cf_ref, dtf_ref, bias_ref, a_ref, yf_ref, st_ref)
    _ssd_chunk(1, xb_ref, bb_ref, cb_ref, dtb_ref, bias_ref, a_ref, yb_ref, st_ref)


def _ssd_chunk(d, x_ref, b_ref, c_ref, dt_ref, bias_ref, a_ref, y_ref, st_ref):
    mask = _dir_masks(d)
    tri = jnp.where(mask, 1.0, 0.0).astype(F32)
    dt = _softplus(dt_ref[...] + bias_ref[d])
    dta = dt * a_ref[d]
    la = jnp.dot(tri, dta, precision=HIGHEST, preferred_element_type=F32)
    total = jnp.sum(dta, axis=0, keepdims=True)
    la_t = la.T
    dt_t = dt.T
    ws_t = (jnp.exp(total - la) * dt).T
    e_tot = jnp.exp(total)
    lane = lax.broadcasted_iota(jnp.int32, (CHUNK, LANES), 1)
    lo_half = lane < SSD_HEAD_DIM
    hg = SSD_HEADS // SSD_GROUPS
    for g in range(SSD_GROUPS):
        bg = b_ref[:, g * SSD_STATE:(g + 1) * SSD_STATE]
        cg = c_ref[:, g * SSD_STATE:(g + 1) * SSD_STATE]
        cb = _dot_nt(cg, bg)
        bg_t = bg.astype(F32).T
        for pr in range(hg // 2):
            h0 = g * hg + 2 * pr
            col0 = slice((h0 * SSD_HEAD_DIM), (h0 + 2) * SSD_HEAD_DIM)
            xpb = x_ref[:, col0]
            ys, sts, la_cols = [], [], []
            for h in (h0, h0 + 1):
                la_col = jnp.broadcast_to(la[:, h:h + 1], (CHUNK, LANES))
                la_cols.append(la_col)
                w = jnp.exp(jnp.where(mask, la_col - la_t[h:h + 1, :], -jnp.inf)) * cb * dt_t[h:h + 1, :]
                ys.append(_dot(w.astype(BF16), xpb))
                sts.append(_dot((bg_t * ws_t[h:h + 1, :]).astype(BF16), xpb))
            st = st_ref[d, g, :, pr * LANES:(pr + 1) * LANES]
            e_pair = jnp.exp(jnp.where(lo_half, la_cols[0], la_cols[1]))
            y_pair = jnp.where(lo_half, ys[0], ys[1]) + e_pair * _dot(cg, st.astype(BF16))
            y_ref[:, col0] = y_pair.astype(y_ref.dtype)
            et_pair = jnp.where(lo_half[0:1], e_tot[:, h0:h0 + 1], e_tot[:, h0 + 1:h0 + 2])
            st_ref[d, g, :, pr * LANES:(pr + 1) * LANES] = et_pair * st + jnp.where(lo_half, sts[0], sts[1])


def _ssd_scan(xbc, dt_raw, dt_bias_pad, a_pad, dims):
    bsz, seq, ctx = dims
    r = xbc.shape[0]
    nsteps = (seq + ctx) // CHUNK
    blk = functools.partial(_chunk_block, bsz=bsz, seq=seq, ctx=ctx)
    xw = SSD_INNER
    def chunk_specs(d):
        return [pl.BlockSpec((CHUNK, xw), lambda b, s: (blk(b, d, s), 0)),
                pl.BlockSpec((CHUNK, SSD_BC), lambda b, s: (blk(b, d, s), xw // SSD_BC)),
                pl.BlockSpec((CHUNK, SSD_BC), lambda b, s: (blk(b, d, s), xw // SSD_BC + 1)),
                pl.BlockSpec((CHUNK, LANES), lambda b, s: (blk(b, d, s), d))]

    const = pl.BlockSpec((2, 1, LANES), lambda b, s: (0, 0, 0))
    y_shape = jax.ShapeDtypeStruct((r, xw), BF16)
    return pl.pallas_call(
        _ssd_scan_kernel,
        grid=(bsz, nsteps),
        in_specs=chunk_specs(0) + chunk_specs(1) + [const, const],
        out_specs=[pl.BlockSpec((CHUNK, xw), lambda b, s: (blk(b, 0, s), 0)),
                   pl.BlockSpec((CHUNK, xw), lambda b, s: (blk(b, 1, s), 0))],
        out_shape=[y_shape, y_shape],
        scratch_shapes=[pltpu.VMEM((2, SSD_GROUPS, SSD_STATE, xw // SSD_GROUPS), F32)],
        compiler_params=_params(("arbitrary", "arbitrary")),
        name="ssd_scan",
    )(xbc, xbc, xbc, dt_raw, xbc, xbc, xbc, dt_raw, dt_bias_pad, a_pad)


def _ssd_out_kernel(yf_ref, yb_ref, xc_ref, z_ref, dexp_ref, nw_ref, w_ref, xs_ref, gate_ref, o_ref):
    y = yf_ref[...].astype(F32) + yb_ref[...].astype(F32) + dexp_ref[...] * xc_ref[...].astype(F32)
    y = y * _silu(z_ref[...])
    a = (_rms(y) * nw_ref[...]).astype(BF16)
    o_ref[...] = xs_ref[...] + gate_ref[...] * _dot(a, w_ref[...])


def _ssd_out(y_f, y_b, xbc, t_main, d_exp, norm_w, w_out_bf16, xs, mods_l, dims):
    bsz, seq, ctx = dims
    r, d = xs.shape
    tm = ROW_TILE
    lt = seq // tm
    xw = SSD_INNER
    return pl.pallas_call(
        _ssd_out_kernel,
        grid=(r // tm,),
        in_specs=[pl.BlockSpec((tm, xw), lambda i: (i, 0)),
                  pl.BlockSpec((tm, xw), lambda i: (i, 0)),
                  pl.BlockSpec((tm, xw), lambda i: (i, 0)),
                  pl.BlockSpec((tm, xw), lambda i: (i, 0)),
                  pl.BlockSpec((1, xw), lambda i: (0, 0)),
                  pl.BlockSpec((1, xw), lambda i: (0, 0)),
                  pl.BlockSpec((xw, d), lambda i: (0, 0)),
                  pl.BlockSpec((tm, d), lambda i: (i, 0)),
                  _mod_spec(2, lt, bsz, d)],
        out_specs=pl.BlockSpec((tm, d), lambda i: (i, 0)),
        out_shape=jax.ShapeDtypeStruct((r, d), F32),
        compiler_params=_params(("arbitrary",)),
        name="ssd_gated_norm_outproj",
    )(y_f, y_b, xbc, t_main, d_exp, norm_w.reshape(1, xw), w_out_bf16, xs, mods_l)


def _ssd_layer(xs, mods_l, g1, w_in, conv_w, conv_b, dt_bias, a_log, d_skip, norm_w, w_out, dims):
    d = xs.shape[1]
    main = SSD_INNER + SSD_CONV_DIM
    w_main = w_in[:, :main].astype(BF16)
    w_dt = jnp.zeros((d, 2 * LANES), F32)
    w_dt = w_dt.at[:, :SSD_HEADS].set(w_in[:, main:main + SSD_HEADS])
    w_dt = w_dt.at[:, LANES:LANES + SSD_HEADS].set(w_in[:, main + SSD_HEADS:]).astype(BF16)
    t_z, xbc, dt_raw = _ssd_inproj(xs, g1, mods_l, w_main, w_dt, conv_w, conv_b, dims)
    pad = jnp.zeros((2, 1, LANES - SSD_HEADS), F32)
    bias_pad = jnp.concatenate([dt_bias.astype(F32).reshape(2, 1, SSD_HEADS), pad], axis=-1)
    a_pad = jnp.concatenate([-jnp.exp(a_log.astype(F32)).reshape(2, 1, SSD_HEADS), pad], axis=-1)
    y_f, y_b = _ssd_scan(xbc, dt_raw, bias_pad, a_pad, dims)
    d_exp = jnp.repeat(d_skip.astype(F32), SSD_HEAD_DIM).reshape(1, SSD_INNER)
    return _ssd_out(y_f, y_b, xbc, t_z, d_exp, norm_w, w_out.astype(BF16), xs, mods_l, dims)


def _qkv_rope_kernel(x_ref, g_ref, sh_ref, sc_ref, w_ref, cos_ref, sin_ref, q_ref, k_ref, v_ref, *, n_lat_tiles):
    i = pl.program_id(0)
    hb = _norm_mod(x_ref[...], g_ref[...], sh_ref[...], sc_ref[...]).astype(BF16)
    d = x_ref.shape[1]
    is_ctx = i >= n_lat_tiles
    cos = jnp.where(is_ctx, 1.0, cos_ref[...])
    sin = jnp.where(is_ctx, 0.0, sin_ref[...])
    nrep = d // LANES
    cos = jnp.concatenate([cos] * nrep, axis=1)
    sin = jnp.concatenate([sin] * nrep, axis=1)

    def mm(c):
        return _dot(hb, w_ref[:, c * d:(c + 1) * d])

    q = mm(0) * cos + mm(3) * sin
    q_ref[...] = (q * (DIFF_HEAD_DIM ** -0.5 * LOG2_E)).astype(BF16)
    k_ref[...] = (mm(1) * cos + mm(4) * sin).astype(BF16)
    v_ref[...] = mm(2).astype(BF16)


def _rope_tables(seq):
    rows = seq // GRID_W
    row = jnp.repeat(jnp.arange(rows, dtype=F32), GRID_W)
    col = jnp.tile(jnp.arange(GRID_W, dtype=F32), rows)
    inv = ROPE_BASE ** (-jnp.arange(ROPE_Q, dtype=F32) / ROPE_Q)
    ang_r = row[:, None] * inv
    ang_c = col[:, None] * inv
    cos = jnp.concatenate([jnp.cos(ang_r), jnp.cos(ang_r), jnp.cos(ang_c), jnp.cos(ang_c)], axis=1)
    sin = jnp.concatenate([-jnp.sin(ang_r), jnp.sin(ang_r), -jnp.sin(ang_c), jnp.sin(ang_c)], axis=1)
    return jnp.tile(cos, (1, 2)), jnp.tile(sin, (1, 2))


def _rope_partner_cols(d):
    col = np.arange(d)
    within = col % (2 * ROPE_Q)
    return np.where(within < ROPE_Q, col + ROPE_Q, col - ROPE_Q)


def _qkv_rope(xs, g, mods_l, w_qkv, dims):
    bsz, seq, ctx = dims
    r, d = xs.shape
    tm = _wide_tile(dims)
    lt = seq // tm
    perm = _rope_partner_cols(d)
    wq, wk, wv = w_qkv[:, :d], w_qkv[:, d:2 * d], w_qkv[:, 2 * d:]
    w_all = jnp.concatenate([wq, wk, wv, wq[:, perm], wk[:, perm]], axis=1).astype(BF16)
    cos, sin = _rope_tables(seq)
    out = jax.ShapeDtypeStruct((r, d), BF16)
    row_spec = pl.BlockSpec((tm, d), lambda i: (i, 0))
    tab_spec = pl.BlockSpec((tm, LANES), lambda i: (i % lt, 0))
    return pl.pallas_call(
        functools.partial(_qkv_rope_kernel, n_lat_tiles=bsz * lt),
        grid=(r // tm,),
        in_specs=[row_spec, pl.BlockSpec((1, d), lambda i: (0, 0)),
                  _mod_spec(0, lt, bsz, d), _mod_spec(1, lt, bsz, d),
                  pl.BlockSpec((d, 5 * d), lambda i: (0, 0)), tab_spec, tab_spec],
        out_specs=[row_spec, row_spec, row_spec],
        out_shape=[out, out, out],
        compiler_params=_params(("arbitrary",)),
        name="norm_mod_qkv_rope",
    )(xs, g.reshape(1, d), mods_l, mods_l, w_all, cos, sin)


def _lane_fold(x, op):
    parts = [x[:, t * LANES:(t + 1) * LANES] for t in range(x.shape[1] // LANES)]
    return functools.reduce(op, parts)


def _attn_kernel(*refs, tk, unroll, with_latent, lambda_init):
    if with_latent:
        lam_ref, q_ref, kc_ref, vc_ref, kx_ref, vx_ref, nw_ref, o_ref, sc_ref, sx_ref, acc_ref, m_ref, l_ref = refs
        n_x = kx_ref.shape[0] // tk
    else:
        lam_ref, q_ref, kc_ref, vc_ref, nw_ref, _, o_ref, sc_ref, acc_ref, m_ref, l_ref = refs
    q = q_ref[...]
    tq = q.shape[0]
    lane = lax.broadcasted_iota(jnp.int32, q.shape, 1)
    zero = jnp.zeros_like(q)
    qq = jnp.concatenate([jnp.where(lane < DIFF_HEAD_DIM, q, zero),
                          jnp.where(lane >= DIFF_HEAD_DIM, q, zero)], axis=0)

    s = _dot_nt(qq, kc_ref[...])
    sc_ref[...] = s
    m_ref[...] = _lane_fold(s, jnp.maximum)

    always = pl.program_id(2) >= 0

    if with_latent:
        @pl.when(always)
        def _():
            def p1(j, m_):
                kj = kx_ref[pl.ds(pl.multiple_of(j * tk, tk), tk), :]
                sj = _dot_nt(qq, kj)
                sx_ref[j] = sj
                return jnp.maximum(m_, _lane_fold(sj, jnp.maximum))

            m_ref[...] = lax.fori_loop(0, n_x, p1, m_ref[...], unroll=unroll)

    mrow = jnp.max(m_ref[...], axis=1, keepdims=True)

    p = jnp.exp2(sc_ref[...] - mrow)
    acc_ref[...] = _dot(p.astype(BF16), vc_ref[...])
    l_ref[...] = _lane_fold(p, jnp.add)

    if with_latent:
        @pl.when(always)
        def _():
            def p2(j, l_):
                vj = vx_ref[pl.ds(pl.multiple_of(j * tk, tk), tk), :]
                pj = jnp.exp2(sx_ref[j] - mrow)
                acc_ref[...] += _dot(pj.astype(BF16), vj)
                return l_ + _lane_fold(pj, jnp.add)

            l_ref[...] = lax.fori_loop(0, n_x, p2, l_ref[...], unroll=unroll)

    on = acc_ref[...] / jnp.sum(l_ref[...], axis=1, keepdims=True)
    o = on[:tq] - lam_ref[0] * on[tq:]
    o = _rms(o) * nw_ref[...] * (1.0 - lambda_init)
    o_ref[...] = o.astype(o_ref.dtype)


def _diff_attention(q, k, v, lam_full, subln_w, lambda_init, dims):
    bsz, seq, ctx = dims
    r, d = q.shape
    tq = min(ATTN_Q_TILE, seq)
    tk = min(ATTN_K_TILE, seq)
    n_lat_q = seq // tq
    smem = pl.BlockSpec(memory_space=pltpu.SMEM)
    nw_spec = pl.BlockSpec((1, LANES), lambda b, h, i: (0, 0))
    ctx_spec = pl.BlockSpec((ctx, LANES), lambda b, h, i: (bsz * seq // ctx + b, h))
    lat_spec = pl.BlockSpec((seq, LANES), lambda b, h, i: (b, h))
    q_lat = pl.BlockSpec((tq, LANES), lambda b, h, i: (b * n_lat_q + i, h))

    def stats(rows):
        return pltpu.VMEM((2 * rows, LANES), F32)

    o_lat = pl.pallas_call(
        functools.partial(_attn_kernel, tk=tk, unroll=min(ATTN_UNROLL, seq // tk), with_latent=True,
                          lambda_init=lambda_init),
        grid=(bsz, DIFF_HEADS, n_lat_q),
        in_specs=[smem, q_lat, ctx_spec, ctx_spec, lat_spec, lat_spec, nw_spec],
        out_specs=q_lat,
        out_shape=jax.ShapeDtypeStruct((r, d), BF16),
        scratch_shapes=[pltpu.VMEM((2 * tq, ctx), F32), pltpu.VMEM((seq // tk, 2 * tq, tk), F32),
                        stats(tq), stats(tq), stats(tq)],
        compiler_params=_params(("arbitrary", "arbitrary", "arbitrary")),
        name="diff_attention",
    )(lam_full, q, k, v, k, v, subln_w.reshape(1, LANES))
    tc = min(tq, ctx)
    n_ctx_q = ctx // tc
    q_ctx = pl.BlockSpec((tc, LANES), lambda b, h, i: (bsz * seq // tc + b * n_ctx_q + i, h))
    return pl.pallas_call(
        functools.partial(_attn_kernel, tk=tk, unroll=1, with_latent=False, lambda_init=lambda_init),
        grid=(bsz, DIFF_HEADS, n_ctx_q),
        in_specs=[smem, q_ctx, ctx_spec, ctx_spec, nw_spec, pl.BlockSpec(memory_space=pl.ANY)],
        out_specs=q_ctx,
        out_shape=jax.ShapeDtypeStruct((r, d), BF16),
        scratch_shapes=[pltpu.VMEM((2 * tc, ctx), F32), stats(tc), stats(tc), stats(tc)],
        input_output_aliases={5: 0},
        compiler_params=_params(("arbitrary", "arbitrary", "arbitrary")),
        name="diff_attention_ctx",
    )(lam_full, q, k, v, subln_w.reshape(1, LANES), o_lat)


def _proj_res_kernel(a_ref, w_ref, xs_ref, gate_ref, o_ref):
    o_ref[...] = xs_ref[...] + gate_ref[...] * _dot(a_ref[...], w_ref[...])


def _proj_res(a_bf16, w_bf16, xs, mods_l, dims):
    bsz, seq, ctx = dims
    r, d = xs.shape
    kdim = a_bf16.shape[1]
    tm = _wide_tile(dims)
    lt = seq // tm
    return pl.pallas_call(
        _proj_res_kernel,
        grid=(r // tm,),
        in_specs=[pl.BlockSpec((tm, kdim), lambda i: (i, 0)),
                  pl.BlockSpec((kdim, d), lambda i: (0, 0)),
                  pl.BlockSpec((tm, d), lambda i: (i, 0)),
                  _mod_spec(2, lt, bsz, d)],
        out_specs=pl.BlockSpec((tm, d), lambda i: (i, 0)),
        out_shape=jax.ShapeDtypeStruct((r, d), F32),
        compiler_params=_params(("arbitrary",)),
        name="outproj_gate_residual",
    )(a_bf16, w_bf16, xs, mods_l)


def _diff_layer(xs, mods_l, g1, w_qkv, lam, subln_w, w_out, lambda_init, dims):
    q, k, v = _qkv_rope(xs, g1, mods_l, w_qkv, dims)
    lam32 = lam.astype(F32)
    lam_full = (jnp.exp(jnp.sum(lam32[0] * lam32[1])) - jnp.exp(jnp.sum(lam32[2] * lam32[3]))
                + lambda_init).reshape(1)
    o = _diff_attention(q, k, v, lam_full, subln_w, lambda_init, dims)
    return _proj_res(o, w_out.astype(BF16), xs, mods_l, dims)


def _mlstm_scan_kernel(q_ref, k_ref, v_ref, g_ref, bias_ref, h_ref, c_ref, n_ref, m_ref):
    d = pl.program_id(1)
    s = pl.program_id(2)

    @pl.when(s == 0)
    def _():
        c_ref[...] = jnp.zeros_like(c_ref)
        n_ref[...] = jnp.zeros_like(n_ref)
        m_ref[...] = jnp.zeros_like(m_ref)

    mask = _dir_masks(d)
    tri = jnp.where(mask, 1.0, 0.0).astype(F32)
    gates = g_ref[...] + bias_ref[...]
    ig = gates[:, 0:ML_HEADS]
    lf = _log_sigmoid(gates)
    bcum = jnp.dot(tri, lf, precision=HIGHEST, preferred_element_type=F32)
    btot = jnp.sum(lf, axis=0, keepdims=True)
    bcum_t = bcum.T
    gates_t = gates.T
    row_last = jnp.where(d == 0, CHUNK - 1, 0)
    rsel = lax.broadcasted_iota(jnp.int32, (CHUNK, 1), 0) == row_last
    for h in range(ML_HEADS):
        fcol = ML_HEADS + h
        m_prev = m_ref[h:h + 1, 0:1]
        bcol = bcum[:, fcol:fcol + 1]
        brow = bcum_t[fcol:fcol + 1, :]
        irow = gates_t[h:h + 1, :]
        icol = ig[:, h:h + 1]
        gcol = bcol + m_prev
        dmat = jnp.where(mask, bcol - brow + irow, -jnp.inf)
        mt = jnp.maximum(gcol, jnp.max(dmat, axis=1, keepdims=True))
        q32 = q_ref[:, h * ML_QK_DIM:(h + 1) * ML_QK_DIM] * (ML_QK_DIM ** -0.5)
        qh = q32.astype(BF16)
        kh32 = k_ref[:, h * ML_QK_DIM:(h + 1) * ML_QK_DIM]
        kh = kh32.astype(BF16)
        vh = v_ref[:, h * ML_V_DIM:(h + 1) * ML_V_DIM].astype(BF16)
        sm = _dot_nt(qh, kh) * jnp.exp(dmat - mt)
        inter = jnp.exp(gcol - mt)
        cst = c_ref[h]
        nst = n_ref[h:h + 1, :]
        num = _dot(sm.astype(BF16), vh) + inter * _dot(qh, cst.astype(BF16))
        qn = jnp.sum(q32 * nst, axis=1, keepdims=True)
        den = jnp.sum(sm, axis=1, keepdims=True) + inter * qn
        h_ref[:, h * ML_V_DIM:(h + 1) * ML_V_DIM] = num / jnp.maximum(jnp.abs(den), jnp.exp(-mt))
        m_new = jnp.sum(jnp.where(rsel, mt, 0.0), axis=0, keepdims=True)
        btot_h = btot[:, fcol:fcol + 1]
        wk = jnp.exp(btot_h - bcol + icol - m_new)
        cscale = jnp.exp(btot_h + m_prev - m_new)
        kw = kh32 * wk
        c_ref[h] = cscale * cst + _dot_tn(kw.astype(BF16), vh)
        n_ref[h:h + 1, :] = cscale * nst + jnp.sum(kw, axis=0, keepdims=True)
        m_ref[h:h + 1, :] = jnp.broadcast_to(m_new, (1, LANES))


def _mlstm_scan(t_main, gates_raw, bias_pad, dims):
    bsz, seq, ctx = dims
    r = t_main.shape[0]
    nsteps = (seq + ctx) // CHUNK
    blk = functools.partial(_chunk_block, bsz=bsz, seq=seq, ctx=ctx)
    return pl.pallas_call(
        _mlstm_scan_kernel,
        grid=(bsz, 2, nsteps),
        in_specs=[pl.BlockSpec((CHUNK, ML_QK), lambda b, d, s: (blk(b, d, s), 0)),
                  pl.BlockSpec((CHUNK, ML_QK), lambda b, d, s: (blk(b, d, s), 1)),
                  pl.BlockSpec((CHUNK, ML_V), lambda b, d, s: (blk(b, d, s), 1)),
                  pl.BlockSpec((CHUNK, LANES), lambda b, d, s: (blk(b, d, s), d)),
                  pl.BlockSpec((None, 1, LANES), lambda b, d, s: (d, 0, 0))],
        out_specs=pl.BlockSpec((None, CHUNK, ML_V), lambda b, d, s: (d, blk(b, d, s), 0)),
        out_shape=jax.ShapeDtypeStruct((2, r, ML_V), F32),
        scratch_shapes=[pltpu.VMEM((ML_HEADS, ML_QK_DIM, ML_V_DIM), F32),
                        pltpu.VMEM((8, ML_QK_DIM), F32),
                        pltpu.VMEM((8, LANES), F32)],
        compiler_params=_params(("arbitrary", "arbitrary", "arbitrary")),
        name="mlstm_scan",
    )(t_main, t_main, t_main, gates_raw, bias_pad)


def _mlstm_out_kernel(hf_ref, hb_ref, o_ref_in, nw_ref, w_ref, xs_ref, gate_ref, out_ref):
    u = None
    for h in range(ML_HEADS):
        cs = slice(h * ML_V_DIM, (h + 1) * ML_V_DIM)
        a = _sigmoid(o_ref_in[:, cs]) * (hf_ref[:, cs] + hb_ref[:, cs])
        a = (_rms(a) * nw_ref[...]).astype(BF16)
        part = _dot(a, w_ref[cs, :])
        u = part if u is None else u + part
    out_ref[...] = xs_ref[...] + gate_ref[...] * u


def _mlstm_out(h2, t_main, norm_w, w_out_bf16, xs, mods_l, dims):
    bsz, seq, ctx = dims
    r, d = xs.shape
    tm = _wide_tile(dims)
    lt = seq // tm
    o_blk = (2 * ML_QK + ML_V) // ML_V
    return pl.pallas_call(
        _mlstm_out_kernel,
        grid=(r // tm,),
        in_specs=[pl.BlockSpec((None, tm, ML_V), lambda i: (0, i, 0)),
                  pl.BlockSpec((None, tm, ML_V), lambda i: (1, i, 0)),
                  pl.BlockSpec((tm, ML_V), lambda i: (i, o_blk)),
                  pl.BlockSpec((1, ML_V_DIM), lambda i: (0, 0)),
                  pl.BlockSpec((ML_V, d), lambda i: (0, 0)),
                  pl.BlockSpec((tm, d), lambda i: (i, 0)),
                  _mod_spec(2, lt, bsz, d)],
        out_specs=pl.BlockSpec((tm, d), lambda i: (i, 0)),
        out_shape=jax.ShapeDtypeStruct((r, d), F32),
        compiler_params=_params(("arbitrary",)),
        name="mlstm_norm_outproj",
    )(h2, h2, t_main, norm_w.reshape(1, ML_V_DIM), w_out_bf16, xs, mods_l)


def _mlstm_layer(xs, mods_l, g1, w_in, b_gates, norm_w, w_out, dims):
    d = xs.shape[1]
    main = 2 * ML_QK + 2 * ML_V
    w_main = w_in[:, :main].astype(BF16)
    wg = w_in[:, main:].reshape(d, 4, ML_HEADS)
    w_g = jnp.zeros((d, 2 * LANES), F32)
    bias = jnp.zeros((2, 1, LANES), F32)
    for dr in range(2):
        w_g = w_g.at[:, dr * LANES:dr * LANES + 2 * ML_HEADS].set(
            wg[:, 2 * dr:2 * dr + 2].reshape(d, 2 * ML_HEADS))
        bias = bias.at[dr, 0, :2 * ML_HEADS].set(b_gates.astype(F32)[2 * dr:2 * dr + 2].reshape(2 * ML_HEADS))
    t_main, gates_raw = _inproj(xs, g1, mods_l, w_main, w_g.astype(BF16), dims)
    h2 = _mlstm_scan(t_main, gates_raw, bias, dims)
    return _mlstm_out(h2, t_main, norm_w, w_out.astype(BF16), xs, mods_l, dims)


def _router_kernel(x_ref, g_ref, sh_ref, sc_ref, wr_ref, br_ref, h_ref, idx_ref, gate_ref, rank_ref, cnt_ref,
                   carry_ref):
    i = pl.program_id(0)

    @pl.when(i == 0)
    def _():
        carry_ref[...] = jnp.zeros_like(carry_ref)

    h = _norm_mod(x_ref[...], g_ref[...], sh_ref[...], sc_ref[...])
    tm, d = h.shape
    h_ref[...] = _pack_bf16_pairs(h)
    logits = lax.dot_general(wr_ref[...], h, (((1,), (1,)), ((), ())), precision=HIGHEST,
                             preferred_element_type=F32) + br_ref[...]
    eidx = lax.broadcasted_iota(jnp.int32, logits.shape, 0)
    work = logits
    vals, idxs = [], []
    picked = jnp.zeros(logits.shape, F32)
    for _ in range(TOP_K):
        mx = jnp.max(work, axis=0, keepdims=True)
        ix = jnp.min(jnp.where(work == mx, eidx, N_EXPERTS), axis=0, keepdims=True)
        sel = eidx == ix
        vals.append(mx)
        idxs.append(ix)
        picked = jnp.where(sel, 1.0, picked)
        work = jnp.where(sel, -jnp.inf, work)
    es = [jnp.exp(v - vals[0]) for v in vals]
    tot = es[0] + es[1] + es[2] + es[3]
    jj = lax.broadcasted_iota(jnp.int32, (tm, tm), 0)
    ii = lax.broadcasted_iota(jnp.int32, (tm, tm), 1)
    upper = jnp.where(jj <= ii, 1.0, 0.0).astype(BF16)
    incl = _dot(picked.astype(BF16), upper)
    carry = carry_ref[:, 0:1]
    excl = incl - picked + carry
    for k in range(TOP_K):
        idx_ref[k:k + 1, :] = idxs[k]
        gate_ref[k:k + 1, :] = es[k] / tot
        rk = jnp.sum(jnp.where(eidx == idxs[k], excl, 0.0), axis=0, keepdims=True)
        rank_ref[k:k + 1, :] = rk.astype(jnp.int32)
    new_carry = carry + jnp.sum(picked, axis=1, keepdims=True)
    carry_ref[...] = jnp.broadcast_to(new_carry, carry_ref.shape)
    cnt_ref[...] = jnp.broadcast_to(new_carry, cnt_ref.shape).astype(jnp.int32)


def _router(xs, g2, mods_l, w_router, b_router, dims):
    bsz, seq, ctx = dims
    r, d = xs.shape
    tm = _wide_tile(dims)
    lt = seq // tm
    tok_spec = pl.BlockSpec((TOP_K, tm), lambda i: (0, i))
    return pl.pallas_call(
        _router_kernel,
        grid=(r // tm,),
        in_specs=[pl.BlockSpec((tm, d), lambda i: (i, 0)),
                  pl.BlockSpec((1, d), lambda i: (0, 0)),
                  _mod_spec(3, lt, bsz, d), _mod_spec(4, lt, bsz, d),
                  pl.BlockSpec((N_EXPERTS, d), lambda i: (0, 0)),
                  pl.BlockSpec((N_EXPERTS, 1), lambda i: (0, 0))],
        out_specs=[pl.BlockSpec((tm, d // 2), lambda i: (i, 0)), tok_spec, tok_spec, tok_spec,
                   pl.BlockSpec((N_EXPERTS, LANES), lambda i: (0, 0))],
        out_shape=[jax.ShapeDtypeStruct((r, d // 2), jnp.int32),
                   jax.ShapeDtypeStruct((TOP_K, r), jnp.int32),
                   jax.ShapeDtypeStruct((TOP_K, r), F32),
                   jax.ShapeDtypeStruct((TOP_K, r), jnp.int32),
                   jax.ShapeDtypeStruct((N_EXPERTS, LANES), jnp.int32)],
        scratch_shapes=[pltpu.VMEM((N_EXPERTS, LANES), F32)],
        compiler_params=_params(("arbitrary",)),
        name="norm_mod_router_top4",
    )(xs, g2.reshape(1, d), mods_l, mods_l, w_router.T, b_router.reshape(N_EXPERTS, 1))


def _pack_bf16_pairs(x):
    w = x.shape[1]
    xr = x.astype(BF16).astype(F32)
    hi = lax.bitcast_convert_type(xr[:, :w // 2], jnp.int32)
    lo = lax.bitcast_convert_type(xr[:, w // 2:], jnp.int32)
    return hi | lax.shift_right_logical(lo, 16)


def _unpack_bf16_pairs(xp):
    hi = lax.bitcast_convert_type(xp & jnp.int32(-65536), F32)
    lo = lax.bitcast_convert_type(lax.shift_left(xp, 16), F32)
    return hi, lo


def _expert_kernel(be_ref, nb_ref, first_ref, slot_ref, next_ref, x_ref, wgu_hbm, bgu_ref, wdn_hbm, bdn_ref, y_ref,
                   wgu_buf, wdn_buf, wgu_bf, wdn_bf, sems, *, layer):
    def fetch(e, s):
        return (pltpu.make_async_copy(wgu_hbm.at[layer, e], wgu_buf.at[s], sems.at[s, 0]),
                pltpu.make_async_copy(wdn_hbm.at[layer, e], wdn_buf.at[s], sems.at[s, 1]))

    def block(i, rows):
        active = i < nb_ref[0]
        slot = slot_ref[i]
        e = be_ref[i]

        @pl.when(jnp.logical_and(active, first_ref[i] == 1))
        def _():
            @pl.when(i == 0)
            def _():
                for cp in fetch(be_ref[0], 0):
                    cp.start()

            for cp in fetch(e, slot):
                cp.wait()
            wgu_bf[...] = wgu_buf[slot].astype(BF16)
            wdn_bf[...] = wdn_buf[slot].astype(BF16)

            @pl.when(next_ref[i] >= 0)
            def _():
                for cp in fetch(next_ref[i], 1 - slot):
                    cp.start()

        @pl.when(active)
        def _():
            de = wdn_bf.shape[0]
            half = x_ref.shape[1]
            xa, xb = _unpack_bf16_pairs(x_ref[rows, :])
            gu = _dot(xa.astype(BF16), wgu_bf[:half, :]) + _dot(xb.astype(BF16), wgu_bf[half:, :]) + bgu_ref[e]
            g = jnp.minimum(gu[:, :de], SWIGLU_LIMIT)
            u = jnp.clip(gu[:, de:], -SWIGLU_LIMIT, SWIGLU_LIMIT)
            a = (u + 1.0) * g * _sigmoid(SWIGLU_ALPHA * g)
            y_ref[rows, :] = _pack_bf16_pairs(_dot(a.astype(BF16), wdn_bf[...]) + bdn_ref[e])

    step = pl.program_id(0)
    for sub in range(MOE_BLOCKS_PER_STEP):
        block(step * MOE_BLOCKS_PER_STEP + sub, slice(sub * MOE_BLOCK, (sub + 1) * MOE_BLOCK))


def _experts(xb, block_expert, nb_used, grp_first, grp_slot, grp_next, w_gu, b_gu, w_dn, b_dn, layer):
    nrows, half = xb.shape
    bm = MOE_BLOCK
    depth, ne, d, two_de = w_gu.shape
    de = two_de // 2
    nsp = 5
    rows_step = bm * MOE_BLOCKS_PER_STEP
    assert nrows % rows_step == 0
    grid_spec = pltpu.PrefetchScalarGridSpec(
        num_scalar_prefetch=nsp,
        grid=(nrows // rows_step,),
        in_specs=[pl.BlockSpec((rows_step, half), lambda i, *_: (i, 0)),
                  pl.BlockSpec(memory_space=pl.ANY),
                  pl.BlockSpec((None, ne, 1, two_de), lambda i, *_: (layer, 0, 0, 0)),
                  pl.BlockSpec(memory_space=pl.ANY),
                  pl.BlockSpec((None, ne, 1, d), lambda i, *_: (layer, 0, 0, 0))],
        out_specs=pl.BlockSpec((rows_step, d // 2), lambda i, *_: (i, 0)),
        scratch_shapes=[pltpu.VMEM((2, d, two_de), F32), pltpu.VMEM((2, de, d), F32),
                        pltpu.VMEM((d, two_de), BF16), pltpu.VMEM((de, d), BF16),
                        pltpu.SemaphoreType.DMA((2, 2))],
    )
    return pl.pallas_call(
        functools.partial(_expert_kernel, layer=layer),
        grid_spec=grid_spec,
        out_shape=jax.ShapeDtypeStruct((nrows, d // 2), jnp.int32),
        compiler_params=_params(("arbitrary",)),
        name="moe_expert_ffn",
    )(block_expert, nb_used, grp_first, grp_slot, grp_next, xb, w_gu, b_gu.reshape(depth, ne, 1, two_de),
      w_dn, b_dn.reshape(depth, ne, 1, d))


def _sc_row_tokens(dest, n_rows):
    top_k, r = dest.shape
    lanes = SC_LANES
    chunk = r // 2
    assert n_rows % lanes == 0 and r % 2 == 0 and chunk % lanes == 0
    fill_mask = (1 << (r.bit_length() - 1)) - 1
    mesh = plsc.VectorSubcoreMesh(core_axis_name="c", subcore_axis_name="s",
                                  num_cores=SC_CORES, num_subcores=SC_SUBCORES)

    def body(dest_hbm, out_hbm, dest_v, table_v):
        wid = lax.axis_index("s") * SC_CORES + lax.axis_index("c")

        @pl.when(wid == 0)
        def _():
            lane_id = lax.iota(jnp.int32, lanes)

            @pl.loop(0, n_rows // lanes, unroll=8)
            def _(c):
                table_v[pl.ds(c * lanes, lanes)] = (lane_id + c * lanes) & fill_mask

            for k in range(top_k):
                for half in range(2):
                    t0 = half * chunk
                    pltpu.sync_copy(dest_hbm.at[pl.ds(k * r + t0, chunk)], dest_v)

                    @pl.loop(0, chunk // lanes, unroll=8)
                    def _(v):
                        d = dest_v[pl.ds(v * lanes, lanes)]
                        plsc.store_scatter(table_v, [d], lane_id + (t0 + v * lanes))

            pltpu.sync_copy(table_v, out_hbm)

    cp = pltpu.CompilerParams(needs_layout_passes=False)
    return pl.kernel(
        body,
        out_type=jax.ShapeDtypeStruct((n_rows,), jnp.int32),
        mesh=mesh,
        scratch_types=[pltpu.VMEM((chunk,), jnp.int32), pltpu.VMEM((n_rows,), jnp.int32)],
        compiler_params=cp,
        name="sc_row_tokens",
    )(dest.reshape(top_k * r))


def _sc_gather_rows(table, idx):
    n = idx.shape[0]
    width = table.shape[1]
    nw = SC_CORES * SC_SUBCORES
    nb = SC_GATHER_ROWS
    per_w = n // nw
    assert n % nw == 0 and per_w % nb == 0, (n, nw, nb)
    steps = per_w // nb
    mesh = plsc.VectorSubcoreMesh(core_axis_name="c", subcore_axis_name="s",
                                  num_cores=SC_CORES, num_subcores=SC_SUBCORES)

    def body(table_hbm, idx_hbm, out_hbm, idx_a, idx_b, rows_a, rows_b, sem_a, sem_b):
        wid = lax.axis_index("s") * SC_CORES + lax.axis_index("c")
        base = wid * per_w
        slots = ((idx_a, rows_a, sem_a), (idx_b, rows_b, sem_b))

        def gather(j, slot):
            idx_v, rows_v, sem = slots[slot]
            return pltpu.make_async_copy(table_hbm.at[idx_v], rows_v, sem)

        def start(j, slot):
            off = pl.multiple_of(base + j * nb, 8)
            pltpu.sync_copy(idx_hbm.at[pl.ds(off, nb)], slots[slot][0])
            gather(j, slot).start()

        def finish(j, slot):
            off = pl.multiple_of(base + j * nb, 8)
            gather(j, slot).wait()
            pltpu.sync_copy(slots[slot][1], out_hbm.at[pl.ds(off, nb)])

        start(0, 0)

        @pl.loop(0, steps // 2)
        def _(p):
            j = 2 * p
            start(j + 1, 1)
            finish(j, 0)
            if steps % 2 == 1:
                start(j + 2, 0)
            else:
                @pl.when(j + 2 < steps)
                def _():
                    start(j + 2, 0)
            finish(j + 1, 1)

        if steps % 2 == 1:
            finish(steps - 1, 0)

    return pl.kernel(
        body,
        out_type=jax.ShapeDtypeStruct((n, width), table.dtype),
        mesh=mesh,
        scratch_types=[pltpu.VMEM((nb,), jnp.int32), pltpu.VMEM((nb,), jnp.int32),
                       pltpu.VMEM((nb, width), table.dtype), pltpu.VMEM((nb, width), table.dtype),
                       pltpu.SemaphoreType.DMA, pltpu.SemaphoreType.DMA],
        name="sc_gather_rows",
    )(table, idx)


def _combine_kernel(xs_ref, yg_ref, gt_ref, gate_ref, fg_ref, o_ref, *, final):
    f_hi = f_lo = None
    for k in range(TOP_K):
        y_hi, y_lo = _unpack_bf16_pairs(yg_ref[k])
        gk = gt_ref[:, k:k + 1]
        f_hi = y_hi * gk if f_hi is None else f_hi + y_hi * gk
        f_lo = y_lo * gk if f_lo is None else f_lo + y_lo * gk
    out = xs_ref[...] + gate_ref[...] * jnp.concatenate([f_hi, f_lo], axis=1)
    if final:
        out = _rms(out) * fg_ref[...]
    o_ref[...] = out


def _combine(xs, yg, gate_t, mods_l, final_g, final, dims):
    bsz, seq, ctx = dims
    r, d = xs.shape
    tm = _wide_tile(dims)
    lt = seq // tm
    rows_out = bsz * seq if final else r
    return pl.pallas_call(
        functools.partial(_combine_kernel, final=final),
        grid=(rows_out // tm,),
        in_specs=[pl.BlockSpec((tm, d), lambda i: (i, 0)),
                  pl.BlockSpec((TOP_K, tm, d // 2), lambda i: (0, i, 0)),
                  pl.BlockSpec((tm, TOP_K), lambda i: (i, 0)),
                  _mod_spec(5, lt, bsz, d),
                  pl.BlockSpec((1, d), lambda i: (0, 0))],
        out_specs=pl.BlockSpec((tm, d), lambda i: (i, 0)),
        out_shape=jax.ShapeDtypeStruct((rows_out, d), F32),
        compiler_params=_params(("arbitrary",)),
        name="moe_combine_residual",
    )(xs, yg, gate_t, mods_l, final_g.reshape(1, d))


def _moe_layer(xs, mods_l, g2, w_router, b_router, w_gu, b_gu, w_dn, b_dn, final_g, layer, final, dims):
    r, d = xs.shape
    bm = MOE_BLOCK
    hp, top_idx, gate, rank, cnt = _router(xs, g2, mods_l, w_router, b_router, dims)
    counts = cnt[:, 0]
    padded = (counts + bm - 1) // bm * bm
    pad_ends = jnp.cumsum(padded)
    pad_starts = pad_ends - padded
    eids = jnp.arange(N_EXPERTS, dtype=jnp.int32)
    onehot = top_idx[None] == eids[:, None, None]
    dest = jnp.sum(jnp.where(onehot, pad_starts[:, None, None], 0), axis=0) + rank
    n_blocks = -(-(r * TOP_K + N_EXPERTS * (bm - 1)) // bm)
    n_blocks = -(-n_blocks // MOE_BLOCKS_PER_STEP) * MOE_BLOCKS_PER_STEP
    blk_start = jnp.arange(n_blocks, dtype=jnp.int32) * bm
    block_expert = jnp.minimum(jnp.sum((pad_ends[None, :] <= blk_start[:, None]).astype(jnp.int32), axis=1),
                               N_EXPERTS - 1)
    nb_total = pad_ends[-1] // bm
    nb_used = nb_total.astype(jnp.int32).reshape(1)
    blk = jnp.arange(n_blocks, dtype=jnp.int32)
    prev_e = jnp.concatenate([jnp.full((1,), -1, jnp.int32), block_expert[:-1]])
    grp_first = ((block_expert != prev_e) & (blk < nb_total)).astype(jnp.int32)
    grp_slot = (jnp.cumsum(grp_first) - 1) % 2
    later = (eids[None, :] > eids[:, None]) & (counts[None, :] > 0)
    next_of_e = jnp.min(jnp.where(later, eids[None, :], N_EXPERTS), axis=1)
    next_of_e = jnp.where(next_of_e == N_EXPERTS, -1, next_of_e)
    grp_next = jnp.sum(jnp.where(block_expert[:, None] == eids[None, :], next_of_e[None, :], 0), axis=1)
    row_tok = _sc_row_tokens(dest, n_blocks * bm)
    xb = _sc_gather_rows(hp, row_tok)
    yb = _experts(xb, block_expert, nb_used, grp_first, grp_slot.astype(jnp.int32), grp_next.astype(jnp.int32),
                  w_gu, b_gu, w_dn, b_dn, layer)
    yg = _sc_gather_rows(yb, dest.reshape(-1)).reshape(TOP_K, r, d // 2)
    return _combine(xs, yg, gate.T, mods_l, final_g, final, dims)


def kernel(x, c, ctx, c_ctx, ada_w, ada_b, norm1_g, norm2_g, ssd_w_in, ssd_conv_w, ssd_conv_b, ssd_dt_bias,
           ssd_a_log, ssd_d, ssd_norm_w, ssd_w_out, diff_w_qkv, diff_lam, diff_subln_w, diff_w_out, ml_w_in,
           ml_b_gates, ml_norm_w, ml_w_out, moe_w_router, moe_b_router, moe_w_gu, moe_b_gu, moe_w_dn, moe_b_dn,
           final_g):
    bsz, seq, d = x.shape
    n_ctx = ctx.shape[1]
    depth = ada_w.shape[0]
    dims = (bsz, seq, n_ctx)
    n_lat = bsz * seq
    xs = jnp.concatenate([x.reshape(n_lat, d), ctx.reshape(bsz * n_ctx, d)], axis=0)
    cond_rows = jnp.zeros((8, d), F32).at[:bsz].set(c).at[bsz].set(c_ctx)
    mods = _mods(cond_rows, ada_w, ada_b)
    mods = mods[:, :, :bsz + 1].reshape(depth, N_MOD, bsz + 1, 1, d)
    for i in range(depth):
        mods_l = mods[i]
        kind, j = i % 3, i // 3
        if kind == 0:
            xs = _ssd_layer(xs, mods_l, norm1_g[i], ssd_w_in[j], ssd_conv_w[j], ssd_conv_b[j], ssd_dt_bias[j],
                            ssd_a_log[j], ssd_d[j], ssd_norm_w[j], ssd_w_out[j], dims)
        elif kind == 1:
            lambda_init = 0.8 - 0.6 * math.exp(-0.3 * i)
            xs = _diff_layer(xs, mods_l, norm1_g[i], diff_w_qkv[j], diff_lam[j], diff_subln_w[j], diff_w_out[j],
                             lambda_init, dims)
        else:
            xs = _mlstm_layer(xs, mods_l, norm1_g[i], ml_w_in[j], ml_b_gates[j], ml_norm_w[j], ml_w_out[j], dims)
        xs = _moe_layer(xs, mods_l, norm2_g[i], moe_w_router[i], moe_b_router[i], moe_w_gu, moe_b_gu,
                        moe_w_dn, moe_b_dn, final_g, i, i == depth - 1, dims)
    return xs.reshape(bsz, seq, d)
```

```python
import functools
import math

import numpy as np
import jax
import jax.numpy as jnp
from jax import lax
from jax.experimental import pallas as pl
from jax.experimental.pallas import tpu as pltpu
from jax.experimental.pallas import tpu_sc as plsc

F32 = jnp.float32
BF16 = jnp.bfloat16
HIGHEST = lax.Precision.HIGHEST

GRID_W = 64
RMS_EPS = 1e-6
N_MOD = 6
SSD_HEAD_DIM = 64
SSD_HEADS = 32
SSD_GROUPS = 4
SSD_STATE = 128
SSD_INNER = SSD_HEADS * SSD_HEAD_DIM
SSD_BC = SSD_GROUPS * SSD_STATE
SSD_CONV_DIM = SSD_INNER + 2 * SSD_BC
DIFF_HEADS = 8
DIFF_HEAD_DIM = 64
DIFF_V_DIM = 128
ROPE_BASE = 10000.0
ROPE_Q = DIFF_HEAD_DIM // 4
ML_HEADS = 4
ML_QK_DIM = 128
ML_V_DIM = 256
ML_QK = ML_HEADS * ML_QK_DIM
ML_V = ML_HEADS * ML_V_DIM
N_EXPERTS = 32
TOP_K = 4
SWIGLU_LIMIT = 7.0
SWIGLU_ALPHA = 1.702
LOG2_E = 1.4426950408889634

LANES = 128
CHUNK = 128
ROW_TILE = 256
ROW_TILE_WIDE = 512
MOE_BLOCK = 256
MOE_BLOCKS_PER_STEP = 2
ATTN_Q_TILE = 512
ATTN_K_TILE = 512
ATTN_UNROLL = 16
SC_CORES = 2
SC_SUBCORES = 16
SC_LANES = 16
SC_GATHER_ROWS = 64
VMEM_LIMIT = 56 * 1024 * 1024


def _wide_tile(dims):
    bsz, seq, ctx = dims
    return min(ROW_TILE_WIDE, seq, bsz * ctx)


def _params(sem):
    return pltpu.CompilerParams(dimension_semantics=sem, vmem_limit_bytes=VMEM_LIMIT)


def _dot(a, b):
    return jnp.dot(a, b, preferred_element_type=F32)


def _dot_nt(a, b):
    return lax.dot_general(a, b, (((1,), (1,)), ((), ())), preferred_element_type=F32)


def _dot_tn(a, b):
    return lax.dot_general(a, b, (((0,), (0,)), ((), ())), preferred_element_type=F32)


def _rms(x):
    return x * lax.rsqrt(jnp.mean(x * x, axis=-1, keepdims=True) + RMS_EPS)


def _sigmoid(x):
    return 1.0 / (1.0 + jnp.exp(-x))


def _silu(x):
    return x * _sigmoid(x)


def _softplus(x):
    return jnp.maximum(x, 0.0) + jnp.log(1.0 + jnp.exp(-jnp.abs(x)))


def _log_sigmoid(x):
    return -_softplus(-x)


def _mod_spec(which, lat_tiles_per_batch, bsz, d):
    return pl.BlockSpec((None, None, 1, d),
                        lambda i: (which, jnp.minimum(i // lat_tiles_per_batch, bsz), 0, 0))


def _mods_kernel(c_ref, w_ref, b_ref, o_ref):
    c = c_ref[...]
    cond = _silu(c)
    o_ref[...] = jnp.dot(cond, w_ref[...], precision=HIGHEST, preferred_element_type=F32) + b_ref[...]


def _mods(cond_rows, ada_w, ada_b):
    depth, d, _ = ada_w.shape
    nr = cond_rows.shape[0]
    return pl.pallas_call(
        _mods_kernel,
        grid=(depth, N_MOD),
        in_specs=[pl.BlockSpec((nr, d), lambda l, j: (0, 0)),
                  pl.BlockSpec((None, d, d), lambda l, j: (l, 0, j)),
                  pl.BlockSpec((None, 1, d), lambda l, j: (l, 0, j))],
        out_specs=pl.BlockSpec((None, None, nr, d), lambda l, j: (l, j, 0, 0)),
        out_shape=jax.ShapeDtypeStruct((depth, N_MOD, nr, d), F32),
        compiler_params=_params(("arbitrary", "arbitrary")),
        name="adaln_mods",
    )(cond_rows, ada_w, ada_b.reshape(depth, 1, N_MOD * d))


def _norm_mod(x, g, sh, sc):
    return (_rms(x) * g) * (1.0 + sc) + sh


def _inproj_kernel(x_ref, g_ref, sh_ref, sc_ref, w_ref, w2_ref, o_ref, o2_ref, *, n_chunk):
    hb = _norm_mod(x_ref[...], g_ref[...], sh_ref[...], sc_ref[...]).astype(BF16)
    n = o_ref.shape[1]
    for n0 in range(0, n, n_chunk):
        o_ref[:, n0:n0 + n_chunk] = _dot(hb, w_ref[:, n0:n0 + n_chunk])
    o2_ref[...] = _dot(hb, w2_ref[...])


def _col_chunk(n):
    for c in (512, 384, 256, 128):
        if n % c == 0:
            return c
    raise ValueError(n)


def _inproj(xs, g, mods_l, w_bf16, w2_bf16, dims):
    bsz, seq, ctx = dims
    r, d = xs.shape
    n = w_bf16.shape[1]
    n2 = w2_bf16.shape[1]
    tm = _wide_tile(dims)
    lt = seq // tm
    return pl.pallas_call(
        functools.partial(_inproj_kernel, n_chunk=_col_chunk(n)),
        grid=(r // tm,),
        in_specs=[pl.BlockSpec((tm, d), lambda i: (i, 0)),
                  pl.BlockSpec((1, d), lambda i: (0, 0)),
                  _mod_spec(0, lt, bsz, d), _mod_spec(1, lt, bsz, d),
                  pl.BlockSpec((d, n), lambda i: (0, 0)),
                  pl.BlockSpec((d, n2), lambda i: (0, 0))],
        out_specs=[pl.BlockSpec((tm, n), lambda i: (i, 0)), pl.BlockSpec((tm, n2), lambda i: (i, 0))],
        out_shape=[jax.ShapeDtypeStruct((r, n), F32), jax.ShapeDtypeStruct((r, n2), F32)],
        compiler_params=_params(("arbitrary",)),
        name="norm_mod_inproj",
    )(xs, g.reshape(1, d), mods_l, mods_l, w_bf16, w2_bf16)


def _ssd_inproj_kernel(first_ref, last_ref, x_ref, xp_ref, xn_ref, g_ref, sh_ref, sc_ref, w_ref, wdt_ref, cw_ref,
                       cb_ref, z_ref, xbc_ref, dt_ref, *, n_chunk):
    i = pl.program_id(0)
    g, sh, sc = g_ref[...], sh_ref[...], sc_ref[...]
    hb = _norm_mod(x_ref[...], g, sh, sc).astype(BF16)
    halo = jnp.concatenate([xp_ref[...], xn_ref[...]], axis=0)
    hb_ext = jnp.concatenate([hb, _norm_mod(halo, g, sh, sc).astype(BF16)], axis=0)
    tm = hb.shape[0]
    nz = z_ref.shape[1]
    for n0 in range(0, nz, n_chunk):
        z_ref[:, n0:n0 + n_chunk] = _dot(hb, w_ref[:, n0:n0 + n_chunk])
    dt_ref[...] = _dot(hb, wdt_ref[...])
    row = lax.broadcasted_iota(jnp.int32, (tm, n_chunk), 0)
    keep_prev = first_ref[i] == 0
    keep_next = last_ref[i] == 0
    for n0 in range(0, xbc_ref.shape[1], n_chunk):
        cs = slice(n0, n0 + n_chunk)
        t = _dot(hb_ext, w_ref[:, nz + n0:nz + n0 + n_chunk])
        cur = t[:tm]
        prev_row = jnp.where(keep_prev, t[tm + 7:tm + 8], 0.0)
        next_row = jnp.where(keep_next, t[tm + 8:tm + 9], 0.0)
        down = jnp.where(row == 0, prev_row, pltpu.roll(cur, 1, 0))
        up = jnp.where(row == tm - 1, next_row, pltpu.roll(cur, tm - 1, 0))
        y = cw_ref[0:1, cs] * down + cw_ref[1:2, cs] * cur + cw_ref[2:3, cs] * up + cb_ref[:, cs]
        xbc_ref[:, cs] = _silu(y).astype(xbc_ref.dtype)


def _ssd_inproj(xs, g, mods_l, w_main, w_dt, conv_w, conv_b, dims):
    bsz, seq, ctx = dims
    r, d = xs.shape
    tm = ROW_TILE
    lt = seq // tm
    n2 = w_dt.shape[1]
    starts = [b * seq for b in range(bsz)] + [bsz * seq + b * ctx for b in range(bsz)]
    ends = [s + (seq if k < bsz else ctx) for k, s in enumerate(starts)]
    first = np.array([1 if (i * tm) in starts else 0 for i in range(r // tm)], np.int32)
    last = np.array([1 if ((i + 1) * tm) in ends else 0 for i in range(r // tm)], np.int32)
    sub = tm // 8
    nblk8 = r // 8

    def mod_spec(which):
        return pl.BlockSpec((None, None, 1, d), lambda i, f, l: (which, jnp.minimum(i // lt, bsz), 0, 0))

    def const(shape):
        return pl.BlockSpec(shape, lambda i, f, l: (0, 0))

    grid_spec = pltpu.PrefetchScalarGridSpec(
        num_scalar_prefetch=2,
        grid=(r // tm,),
        in_specs=[pl.BlockSpec((tm, d), lambda i, f, l: (i, 0)),
                  pl.BlockSpec((8, d), lambda i, f, l: (jnp.maximum(i * sub - 1, 0), 0)),
                  pl.BlockSpec((8, d), lambda i, f, l: (jnp.minimum((i + 1) * sub, nblk8 - 1), 0)),
                  const((1, d)), mod_spec(0), mod_spec(1),
                  const((d, SSD_INNER + SSD_CONV_DIM)), const((d, n2)),
                  const((3, SSD_CONV_DIM)), const((1, SSD_CONV_DIM))],
        out_specs=[pl.BlockSpec((tm, SSD_INNER), lambda i, f, l: (i, 0)),
                   pl.BlockSpec((tm, SSD_CONV_DIM), lambda i, f, l: (i, 0)),
                   pl.BlockSpec((tm, n2), lambda i, f, l: (i, 0))],
    )
    return pl.pallas_call(
        functools.partial(_ssd_inproj_kernel, n_chunk=512),
        grid_spec=grid_spec,
        out_shape=[jax.ShapeDtypeStruct((r, SSD_INNER), F32),
                   jax.ShapeDtypeStruct((r, SSD_CONV_DIM), BF16),
                   jax.ShapeDtypeStruct((r, n2), F32)],
        compiler_params=_params(("arbitrary",)),
        name="ssd_inproj_conv_silu",
    )(jnp.asarray(first), jnp.asarray(last), xs, xs, xs, g.reshape(1, d), mods_l, mods_l, w_main, w_dt,
      conv_w, conv_b.reshape(1, -1))


def _chunk_block(b, d, s, bsz, seq, ctx):
    nc_ctx = ctx // CHUNK
    nc_lat = seq // CHUNK
    in_ctx = s < nc_ctx
    pc = jnp.where(d == 0, s, nc_ctx - 1 - s)
    pls = s - nc_ctx
    plat = jnp.where(d == 0, pls, nc_lat - 1 - pls)
    ctx_blk = bsz * nc_lat + b * nc_ctx + pc
    lat_blk = b * nc_lat + plat
    return jnp.where(in_ctx, ctx_blk, lat_blk)


def _dir_masks(d):
    ii = lax.broadcasted_iota(jnp.int32, (CHUNK, CHUNK), 0)
    jj = lax.broadcasted_iota(jnp.int32, (CHUNK, CHUNK), 1)
    sign = 1 - 2 * d
    return (ii - jj) * sign >= 0


def _ssd_scan_kernel(xf_ref, bf_ref, cf_ref, dtf_ref, xb_ref, bb_ref, cb_ref, dtb_ref, bias_ref, a_ref,
                     yf_ref, yb_ref, st_ref):
    s = pl.program_id(1)

    @pl.when(s == 0)
    def _():
        st_ref[...] = jnp.zeros_like(st_ref)

    _ssd_chunk(0, xf_ref, bf_ref, cf_ref, dtf_ref, bias_ref, a_ref, yf_ref, st_ref)
    _ssd_chunk(1, xb_ref, bb_ref, cb_ref, dtb_ref, bias_ref, a_ref, yb_ref, st_ref)


def _ssd_chunk(d, x_ref, b_ref, c_ref, dt_ref, bias_ref, a_ref, y_ref, st_ref):
    mask = _dir_masks(d)
    tri = jnp.where(mask, 1.0, 0.0).astype(F32)
    dt = _softplus(dt_ref[...] + bias_ref[d])
    dta = dt * a_ref[d]
    la = jnp.dot(tri, dta, precision=HIGHEST, preferred_element_type=F32)
    total = jnp.sum(dta, axis=0, keepdims=True)
    la_t = la.T
    dt_t = dt.T
    ws_t = (jnp.exp(total - la) * dt).T
    e_tot = jnp.exp(total)
    lane = lax.broadcasted_iota(jnp.int32, (CHUNK, LANES), 1)
    lo_half = lane < SSD_HEAD_DIM
    hg = SSD_HEADS // SSD_GROUPS
    for g in range(SSD_GROUPS):
        bg = b_ref[:, g * SSD_STATE:(g + 1) * SSD_STATE]
        cg = c_ref[:, g * SSD_STATE:(g + 1) * SSD_STATE]
        cb = _dot_nt(cg, bg)
        bg_t = bg.astype(F32).T
        for pr in range(hg // 2):
            h0 = g * hg + 2 * pr
            col0 = slice((h0 * SSD_HEAD_DIM), (h0 + 2) * SSD_HEAD_DIM)
            xpb = x_ref[:, col0]
            ys, sts, la_cols = [], [], []
            for h in (h0, h0 + 1):
                la_col = jnp.broadcast_to(la[:, h:h + 1], (CHUNK, LANES))
                la_cols.append(la_col)
                w = jnp.exp(jnp.where(mask, la_col - la_t[h:h + 1, :], -jnp.inf)) * cb * dt_t[h:h + 1, :]
                ys.append(_dot(w.astype(BF16), xpb))
                sts.append(_dot((bg_t * ws_t[h:h + 1, :]).astype(BF16), xpb))
            st = st_ref[d, g, :, pr * LANES:(pr + 1) * LANES]
            e_pair = jnp.exp(jnp.where(lo_half, la_cols[0], la_cols[1]))
            y_pair = jnp.where(lo_half, ys[0], ys[1]) + e_pair * _dot(cg, st.astype(BF16))
            y_ref[:, col0] = y_pair.astype(y_ref.dtype)
            et_pair = jnp.where(lo_half[0:1], e_tot[:, h0:h0 + 1], e_tot[:, h0 + 1:h0 + 2])
            st_ref[d, g, :, pr * LANES:(pr + 1) * LANES] = et_pair * st + jnp.where(lo_half, sts[0], sts[1])


def _ssd_scan(xbc, dt_raw, dt_bias_pad, a_pad, dims):
    bsz, seq, ctx = dims
    r = xbc.shape[0]
    nsteps = (seq + ctx) // CHUNK
    blk = functools.partial(_chunk_block, bsz=bsz, seq=seq, ctx=ctx)
    xw = SSD_INNER
    def chunk_specs(d):
        return [pl.BlockSpec((CHUNK, xw), lambda b, s: (blk(b, d, s), 0)),
                pl.BlockSpec((CHUNK, SSD_BC), lambda b, s: (blk(b, d, s), xw // SSD_BC)),
                pl.BlockSpec((CHUNK, SSD_BC), lambda b, s: (blk(b, d, s), xw // SSD_BC + 1)),
                pl.BlockSpec((CHUNK, LANES), lambda b, s: (blk(b, d, s), d))]

    const = pl.BlockSpec((2, 1, LANES), lambda b, s: (0, 0, 0))
    y_shape = jax.ShapeDtypeStruct((r, xw), BF16)
    return pl.pallas_call(
        _ssd_scan_kernel,
        grid=(bsz, nsteps),
        in_specs=chunk_specs(0) + chunk_specs(1) + [const, const],
        out_specs=[pl.BlockSpec((CHUNK, xw), lambda b, s: (blk(b, 0, s), 0)),
                   pl.BlockSpec((CHUNK, xw), lambda b, s: (blk(b, 1, s), 0))],
        out_shape=[y_shape, y_shape],
        scratch_shapes=[pltpu.VMEM((2, SSD_GROUPS, SSD_STATE, xw // SSD_GROUPS), F32)],
        compiler_params=_params(("arbitrary", "arbitrary")),
        name="ssd_scan",
    )(xbc, xbc, xbc, dt_raw, xbc, xbc, xbc, dt_raw, dt_bias_pad, a_pad)


def _ssd_out_kernel(yf_ref, yb_ref, xc_ref, z_ref, dexp_ref, nw_ref, w_ref, xs_ref, gate_ref, o_ref):
    y = yf_ref[...].astype(F32) + yb_ref[...].astype(F32) + dexp_ref[...] * xc_ref[...].astype(F32)
    y = y * _silu(z_ref[...])
    a = (_rms(y) * nw_ref[...]).astype(BF16)
    o_ref[...] = xs_ref[...] + gate_ref[...] * _dot(a, w_ref[...])


def _ssd_out(y_f, y_b, xbc, t_main, d_exp, norm_w, w_out_bf16, xs, mods_l, dims):
    bsz, seq, ctx = dims
    r, d = xs.shape
    tm = ROW_TILE
    lt = seq // tm
    xw = SSD_INNER
    return pl.pallas_call(
        _ssd_out_kernel,
        grid=(r // tm,),
        in_specs=[pl.BlockSpec((tm, xw), lambda i: (i, 0)),
                  pl.BlockSpec((tm, xw), lambda i: (i, 0)),
                  pl.BlockSpec((tm, xw), lambda i: (i, 0)),
                  pl.BlockSpec((tm, xw), lambda i: (i, 0)),
                  pl.BlockSpec((1, xw), lambda i: (0, 0)),
                  pl.BlockSpec((1, xw), lambda i: (0, 0)),
                  pl.BlockSpec((xw, d), lambda i: (0, 0)),
                  pl.BlockSpec((tm, d), lambda i: (i, 0)),
                  _mod_spec(2, lt, bsz, d)],
        out_specs=pl.BlockSpec((tm, d), lambda i: (i, 0)),
        out_shape=jax.ShapeDtypeStruct((r, d), F32),
        compiler_params=_params(("arbitrary",)),
        name="ssd_gated_norm_outproj",
    )(y_f, y_b, xbc, t_main, d_exp, norm_w.reshape(1, xw), w_out_bf16, xs, mods_l)


def _ssd_layer(xs, mods_l, g1, w_in, conv_w, conv_b, dt_bias, a_log, d_skip, norm_w, w_out, dims):
    d = xs.shape[1]
    main = SSD_INNER + SSD_CONV_DIM
    w_main = w_in[:, :main].astype(BF16)
    w_dt = jnp.zeros((d, 2 * LANES), F32)
    w_dt = w_dt.at[:, :SSD_HEADS].set(w_in[:, main:main + SSD_HEADS])
    w_dt = w_dt.at[:, LANES:LANES + SSD_HEADS].set(w_in[:, main + SSD_HEADS:]).astype(BF16)
    t_z, xbc, dt_raw = _ssd_inproj(xs, g1, mods_l, w_main, w_dt, conv_w, conv_b, dims)
    pad = jnp.zeros((2, 1, LANES - SSD_HEADS), F32)
    bias_pad = jnp.concatenate([dt_bias.astype(F32).reshape(2, 1, SSD_HEADS), pad], axis=-1)
    a_pad = jnp.concatenate([-jnp.exp(a_log.astype(F32)).reshape(2, 1, SSD_HEADS), pad], axis=-1)
    y_f, y_b = _ssd_scan(xbc, dt_raw, bias_pad, a_pad, dims)
    d_exp = jnp.repeat(d_skip.astype(F32), SSD_HEAD_DIM).reshape(1, SSD_INNER)
    return _ssd_out(y_f, y_b, xbc, t_z, d_exp, norm_w, w_out.astype(BF16), xs, mods_l, dims)


def _qkv_rope_kernel(x_ref, g_ref, sh_ref, sc_ref, w_ref, cos_ref, sin_ref, q_ref, k_ref, v_ref, *, n_lat_tiles):
    i = pl.program_id(0)
    hb = _norm_mod(x_ref[...], g_ref[...], sh_ref[...], sc_ref[...]).astype(BF16)
    d = x_ref.shape[1]
    is_ctx = i >= n_lat_tiles
    cos = jnp.where(is_ctx, 1.0, cos_ref[...])
    sin = jnp.where(is_ctx, 0.0, sin_ref[...])
    nrep = d // LANES
    cos = jnp.concatenate([cos] * nrep, axis=1)
    sin = jnp.concatenate([sin] * nrep, axis=1)

    def mm(c):
        return _dot(hb, w_ref[:, c * d:(c + 1) * d])

    q = mm(0) * cos + mm(3) * sin
    q_ref[...] = (q * (DIFF_HEAD_DIM ** -0.5 * LOG2_E)).astype(BF16)
    k_ref[...] = (mm(1) * cos + mm(4) * sin).astype(BF16)
    v_ref[...] = mm(2).astype(BF16)


def _rope_tables(seq):
    rows = seq // GRID_W
    row = jnp.repeat(jnp.arange(rows, dtype=F32), GRID_W)
    col = jnp.tile(jnp.arange(GRID_W, dtype=F32), rows)
    inv = ROPE_BASE ** (-jnp.arange(ROPE_Q, dtype=F32) / ROPE_Q)
    ang_r = row[:, None] * inv
    ang_c = col[:, None] * inv
    cos = jnp.concatenate([jnp.cos(ang_r), jnp.cos(ang_r), jnp.cos(ang_c), jnp.cos(ang_c)], axis=1)
    sin = jnp.concatenate([-jnp.sin(ang_r), jnp.sin(ang_r), -jnp.sin(ang_c), jnp.sin(ang_c)], axis=1)
    return jnp.tile(cos, (1, 2)), jnp.tile(sin, (1, 2))


def _rope_partner_cols(d):
    col = np.arange(d)
    within = col % (2 * ROPE_Q)
    return np.where(within < ROPE_Q, col + ROPE_Q, col - ROPE_Q)


def _qkv_rope(xs, g, mods_l, w_qkv, dims):
    bsz, seq, ctx = dims
    r, d = xs.shape
    tm = _wide_tile(dims)
    lt = seq // tm
    perm = _rope_partner_cols(d)
    wq, wk, wv = w_qkv[:, :d], w_qkv[:, d:2 * d], w_qkv[:, 2 * d:]
    w_all = jnp.concatenate([wq, wk, wv, wq[:, perm], wk[:, perm]], axis=1).astype(BF16)
    cos, sin = _rope_tables(seq)
    out = jax.ShapeDtypeStruct((r, d), BF16)
    row_spec = pl.BlockSpec((tm, d), lambda i: (i, 0))
    tab_spec = pl.BlockSpec((tm, LANES), lambda i: (i % lt, 0))
    return pl.pallas_call(
        functools.partial(_qkv_rope_kernel, n_lat_tiles=bsz * lt),
        grid=(r // tm,),
        in_specs=[row_spec, pl.BlockSpec((1, d), lambda i: (0, 0)),
                  _mod_spec(0, lt, bsz, d), _mod_spec(1, lt, bsz, d),
                  pl.BlockSpec((d, 5 * d), lambda i: (0, 0)), tab_spec, tab_spec],
        out_specs=[row_spec, row_spec, row_spec],
        out_shape=[out, out, out],
        compiler_params=_params(("arbitrary",)),
        name="norm_mod_qkv_rope",
    )(xs, g.reshape(1, d), mods_l, mods_l, w_all, cos, sin)


def _lane_fold(x, op):
    parts = [x[:, t * LANES:(t + 1) * LANES] for t in range(x.shape[1] // LANES)]
    return functools.reduce(op, parts)


def _attn_kernel(*refs, tk, unroll, with_latent, lambda_init):
    if with_latent:
        lam_ref, q_ref, kc_ref, vc_ref, kx_ref, vx_ref, nw_ref, o_ref, sc_ref, sx_ref, acc_ref, m_ref, l_ref = refs
        n_x = kx_ref.shape[0] // tk
    else:
        lam_ref, q_ref, kc_ref, vc_ref, nw_ref, _, o_ref, sc_ref, acc_ref, m_ref, l_ref = refs
    q = q_ref[...]
    tq = q.shape[0]
    lane = lax.broadcasted_iota(jnp.int32, q.shape, 1)
    zero = jnp.zeros_like(q)
    qq = jnp.concatenate([jnp.where(lane < DIFF_HEAD_DIM, q, zero),
                          jnp.where(lane >= DIFF_HEAD_DIM, q, zero)], axis=0)

    s = _dot_nt(qq, kc_ref[...])
    sc_ref[...] = s
    m_ref[...] = _lane_fold(s, jnp.maximum)

    always = pl.program_id(2) >= 0

    if with_latent:
        @pl.when(always)
        def _():
            def p1(j, m_):
                kj = kx_ref[pl.ds(pl.multiple_of(j * tk, tk), tk), :]
                sj = _dot_nt(qq, kj)
                sx_ref[j] = sj
                return jnp.maximum(m_, _lane_fold(sj, jnp.maximum))

            m_ref[...] = lax.fori_loop(0, n_x, p1, m_ref[...], unroll=unroll)

    mrow = jnp.max(m_ref[...], axis=1, keepdims=True)

    p = jnp.exp2(sc_ref[...] - mrow)
    acc_ref[...] = _dot(p.astype(BF16), vc_ref[...])
    l_ref[...] = _lane_fold(p, jnp.add)

    if with_latent:
        @pl.when(always)
        def _():
            def p2(j, l_):
                vj = vx_ref[pl.ds(pl.multiple_of(j * tk, tk), tk), :]
                pj = jnp.exp2(sx_ref[j] - mrow)
                acc_ref[...] += _dot(pj.astype(BF16), vj)
                return l_ + _lane_fold(pj, jnp.add)

            l_ref[...] = lax.fori_loop(0, n_x, p2, l_ref[...], unroll=unroll)

    on = acc_ref[...] / jnp.sum(l_ref[...], axis=1, keepdims=True)
    o = on[:tq] - lam_ref[0] * on[tq:]
    o = _rms(o) * nw_ref[...] * (1.0 - lambda_init)
    o_ref[...] = o.astype(o_ref.dtype)


def _diff_attention(q, k, v, lam_full, subln_w, lambda_init, dims):
    bsz, seq, ctx = dims
    r, d = q.shape
    tq = min(ATTN_Q_TILE, seq)
    tk = min(ATTN_K_TILE, seq)
    n_lat_q = seq // tq
    smem = pl.BlockSpec(memory_space=pltpu.SMEM)
    nw_spec = pl.BlockSpec((1, LANES), lambda b, h, i: (0, 0))
    ctx_spec = pl.BlockSpec((ctx, LANES), lambda b, h, i: (bsz * seq // ctx + b, h))
    lat_spec = pl.BlockSpec((seq, LANES), lambda b, h, i: (b, h))
    q_lat = pl.BlockSpec((tq, LANES), lambda b, h, i: (b * n_lat_q + i, h))

    def stats(rows):
        return pltpu.VMEM((2 * rows, LANES), F32)

    o_lat = pl.pallas_call(
        functools.partial(_attn_kernel, tk=tk, unroll=min(ATTN_UNROLL, seq // tk), with_latent=True,
                          lambda_init=lambda_init),
        grid=(bsz, DIFF_HEADS, n_lat_q),
        in_specs=[smem, q_lat, ctx_spec, ctx_spec, lat_spec, lat_spec, nw_spec],
        out_specs=q_lat,
        out_shape=jax.ShapeDtypeStruct((r, d), BF16),
        scratch_shapes=[pltpu.VMEM((2 * tq, ctx), F32), pltpu.VMEM((seq // tk, 2 * tq, tk), F32),
                        stats(tq), stats(tq), stats(tq)],
        compiler_params=_params(("arbitrary", "arbitrary", "arbitrary")),
        name="diff_attention",
    )(lam_full, q, k, v, k, v, subln_w.reshape(1, LANES))
    tc = min(tq, ctx)
    n_ctx_q = ctx // tc
    q_ctx = pl.BlockSpec((tc, LANES), lambda b, h, i: (bsz * seq // tc + b * n_ctx_q + i, h))
    return pl.pallas_call(
        functools.partial(_attn_kernel, tk=tk, unroll=1, with_latent=False, lambda_init=lambda_init),
        grid=(bsz, DIFF_HEADS, n_ctx_q),
        in_specs=[smem, q_ctx, ctx_spec, ctx_spec, nw_spec, pl.BlockSpec(memory_space=pl.ANY)],
        out_specs=q_ctx,
        out_shape=jax.ShapeDtypeStruct((r, d), BF16),
        scratch_shapes=[pltpu.VMEM((2 * tc, ctx), F32), stats(tc), stats(tc), stats(tc)],
        input_output_aliases={5: 0},
        compiler_params=_params(("arbitrary", "arbitrary", "arbitrary")),
        name="diff_attention_ctx",
    )(lam_full, q, k, v, subln_w.reshape(1, LANES), o_lat)


def _proj_res_kernel(a_ref, w_ref, xs_ref, gate_ref, o_ref):
    o_ref[...] = xs_ref[...] + gate_ref[...] * _dot(a_ref[...], w_ref[...])


def _proj_res(a_bf16, w_bf16, xs, mods_l, dims):
    bsz, seq, ctx = dims
    r, d = xs.shape
    kdim = a_bf16.shape[1]
    tm = _wide_tile(dims)
    lt = seq // tm
    return pl.pallas_call(
        _proj_res_kernel,
        grid=(r // tm,),
        in_specs=[pl.BlockSpec((tm, kdim), lambda i: (i, 0)),
                  pl.BlockSpec((kdim, d), lambda i: (0, 0)),
                  pl.BlockSpec((tm, d), lambda i: (i, 0)),
                  _mod_spec(2, lt, bsz, d)],
        out_specs=pl.BlockSpec((tm, d), lambda i: (i, 0)),
        out_shape=jax.ShapeDtypeStruct((r, d), F32),
        compiler_params=_params(("arbitrary",)),
        name="outproj_gate_residual",
    )(a_bf16, w_bf16, xs, mods_l)


def _diff_layer(xs, mods_l, g1, w_qkv, lam, subln_w, w_out, lambda_init, dims):
    q, k, v = _qkv_rope(xs, g1, mods_l, w_qkv, dims)
    lam32 = lam.astype(F32)
    lam_full = (jnp.exp(jnp.sum(lam32[0] * lam32[1])) - jnp.exp(jnp.sum(lam32[2] * lam32[3]))
                + lambda_init).reshape(1)
    o = _diff_attention(q, k, v, lam_full, subln_w, lambda_init, dims)
    return _proj_res(o, w_out.astype(BF16), xs, mods_l, dims)


def _mlstm_scan_kernel(q_ref, k_ref, v_ref, g_ref, bias_ref, h_ref, c_ref, n_ref, m_ref):
    d = pl.program_id(1)
    s = pl.program_id(2)

    @pl.when(s == 0)
    def _():
        c_ref[...] = jnp.zeros_like(c_ref)
        n_ref[...] = jnp.zeros_like(n_ref)
        m_ref[...] = jnp.zeros_like(m_ref)

    mask = _dir_masks(d)
    tri = jnp.where(mask, 1.0, 0.0).astype(F32)
    gates = g_ref[...] + bias_ref[...]
    ig = gates[:, 0:ML_HEADS]
    lf = _log_sigmoid(gates)
    bcum = jnp.dot(tri, lf, precision=HIGHEST, preferred_element_type=F32)
    btot = jnp.sum(lf, axis=0, keepdims=True)
    bcum_t = bcum.T
    gates_t = gates.T
    row_last = jnp.where(d == 0, CHUNK - 1, 0)
    rsel = lax.broadcasted_iota(jnp.int32, (CHUNK, 1), 0) == row_last
    for h in range(ML_HEADS):
        fcol = ML_HEADS + h
        m_prev = m_ref[h:h + 1, 0:1]
        bcol = bcum[:, fcol:fcol + 1]
        brow = bcum_t[fcol:fcol + 1, :]
        irow = gates_t[h:h + 1, :]
        icol = ig[:, h:h + 1]
        gcol = bcol + m_prev
        dmat = jnp.where(mask, bcol - brow + irow, -jnp.inf)
        mt = jnp.maximum(gcol, jnp.max(dmat, axis=1, keepdims=True))
        q32 = q_ref[:, h * ML_QK_DIM:(h + 1) * ML_QK_DIM] * (ML_QK_DIM ** -0.5)
        qh = q32.astype(BF16)
        kh32 = k_ref[:, h * ML_QK_DIM:(h + 1) * ML_QK_DIM]
        kh = kh32.astype(BF16)
        vh = v_ref[:, h * ML_V_DIM:(h + 1) * ML_V_DIM].astype(BF16)
        sm = _dot_nt(qh, kh) * jnp.exp(dmat - mt)
        inter = jnp.exp(gcol - mt)
        cst = c_ref[h]
        nst = n_ref[h:h + 1, :]
        num = _dot(sm.astype(BF16), vh) + inter * _dot(qh, cst.astype(BF16))
        qn = jnp.sum(q32 * nst, axis=1, keepdims=True)
        den = jnp.sum(sm, axis=1, keepdims=True) + inter * qn
        h_ref[:, h * ML_V_DIM:(h + 1) * ML_V_DIM] = num / jnp.maximum(jnp.abs(den), jnp.exp(-mt))
        m_new = jnp.sum(jnp.where(rsel, mt, 0.0), axis=0, keepdims=True)
        btot_h = btot[:, fcol:fcol + 1]
        wk = jnp.exp(btot_h - bcol + icol - m_new)
        cscale = jnp.exp(btot_h + m_prev - m_new)
        kw = kh32 * wk
        c_ref[h] = cscale * cst + _dot_tn(kw.astype(BF16), vh)
        n_ref[h:h + 1, :] = cscale * nst + jnp.sum(kw, axis=0, keepdims=True)
        m_ref[h:h + 1, :] = jnp.broadcast_to(m_new, (1, LANES))


def _mlstm_scan(t_main, gates_raw, bias_pad, dims):
    bsz, seq, ctx = dims
    r = t_main.shape[0]
    nsteps = (seq + ctx) // CHUNK
    blk = functools.partial(_chunk_block, bsz=bsz, seq=seq, ctx=ctx)
    return pl.pallas_call(
        _mlstm_scan_kernel,
        grid=(bsz, 2, nsteps),
        in_specs=[pl.BlockSpec((CHUNK, ML_QK), lambda b, d, s: (blk(b, d, s), 0)),
                  pl.BlockSpec((CHUNK, ML_QK), lambda b, d, s: (blk(b, d, s), 1)),
                  pl.BlockSpec((CHUNK, ML_V), lambda b, d, s: (blk(b, d, s), 1)),
                  pl.BlockSpec((CHUNK, LANES), lambda b, d, s: (blk(b, d, s), d)),
                  pl.BlockSpec((None, 1, LANES), lambda b, d, s: (d, 0, 0))],
        out_specs=pl.BlockSpec((None, CHUNK, ML_V), lambda b, d, s: (d, blk(b, d, s), 0)),
        out_shape=jax.ShapeDtypeStruct((2, r, ML_V), F32),
        scratch_shapes=[pltpu.VMEM((ML_HEADS, ML_QK_DIM, ML_V_DIM), F32),
                        pltpu.VMEM((8, ML_QK_DIM), F32),
                        pltpu.VMEM((8, LANES), F32)],
        compiler_params=_params(("arbitrary", "arbitrary", "arbitrary")),
        name="mlstm_scan",
    )(t_main, t_main, t_main, gates_raw, bias_pad)


def _mlstm_out_kernel(hf_ref, hb_ref, o_ref_in, nw_ref, w_ref, xs_ref, gate_ref, out_ref):
    u = None
    for h in range(ML_HEADS):
        cs = slice(h * ML_V_DIM, (h + 1) * ML_V_DIM)
        a = _sigmoid(o_ref_in[:, cs]) * (hf_ref[:, cs] + hb_ref[:, cs])
        a = (_rms(a) * nw_ref[...]).astype(BF16)
        part = _dot(a, w_ref[cs, :])
        u = part if u is None else u + part
    out_ref[...] = xs_ref[...] + gate_ref[...] * u


def _mlstm_out(h2, t_main, norm_w, w_out_bf16, xs, mods_l, dims):
    bsz, seq, ctx = dims
    r, d = xs.shape
    tm = _wide_tile(dims)
    lt = seq // tm
    o_blk = (2 * ML_QK + ML_V) // ML_V
    return pl.pallas_call(
        _mlstm_out_kernel,
        grid=(r // tm,),
        in_specs=[pl.BlockSpec((None, tm, ML_V), lambda i: (0, i, 0)),
                  pl.BlockSpec((None, tm, ML_V), lambda i: (1, i, 0)),
                  pl.BlockSpec((tm, ML_V), lambda i: (i, o_blk)),
                  pl.BlockSpec((1, ML_V_DIM), lambda i: (0, 0)),
                  pl.BlockSpec((ML_V, d), lambda i: (0, 0)),
                  pl.BlockSpec((tm, d), lambda i: (i, 0)),
                  _mod_spec(2, lt, bsz, d)],
        out_specs=pl.BlockSpec((tm, d), lambda i: (i, 0)),
        out_shape=jax.ShapeDtypeStruct((r, d), F32),
        compiler_params=_params(("arbitrary",)),
        name="mlstm_norm_outproj",
    )(h2, h2, t_main, norm_w.reshape(1, ML_V_DIM), w_out_bf16, xs, mods_l)


def _mlstm_layer(xs, mods_l, g1, w_in, b_gates, norm_w, w_out, dims):
    d = xs.shape[1]
    main = 2 * ML_QK + 2 * ML_V
    w_main = w_in[:, :main].astype(BF16)
    wg = w_in[:, main:].reshape(d, 4, ML_HEADS)
    w_g = jnp.zeros((d, 2 * LANES), F32)
    bias = jnp.zeros((2, 1, LANES), F32)
    for dr in range(2):
        w_g = w_g.at[:, dr * LANES:dr * LANES + 2 * ML_HEADS].set(
            wg[:, 2 * dr:2 * dr + 2].reshape(d, 2 * ML_HEADS))
        bias = bias.at[dr, 0, :2 * ML_HEADS].set(b_gates.astype(F32)[2 * dr:2 * dr + 2].reshape(2 * ML_HEADS))
    t_main, gates_raw = _inproj(xs, g1, mods_l, w_main, w_g.astype(BF16), dims)
    h2 = _mlstm_scan(t_main, gates_raw, bias, dims)
    return _mlstm_out(h2, t_main, norm_w, w_out.astype(BF16), xs, mods_l, dims)


def _router_kernel(x_ref, g_ref, sh_ref, sc_ref, wr_ref, br_ref, h_ref, idx_ref, gate_ref, rank_ref, cnt_ref,
                   carry_ref):
    i = pl.program_id(0)

    @pl.when(i == 0)
    def _():
        carry_ref[...] = jnp.zeros_like(carry_ref)

    h = _norm_mod(x_ref[...], g_ref[...], sh_ref[...], sc_ref[...])
    tm, d = h.shape
    h_ref[...] = _pack_bf16_pairs(h)
    logits = lax.dot_general(wr_ref[...], h, (((1,), (1,)), ((), ())), precision=HIGHEST,
                             preferred_element_type=F32) + br_ref[...]
    eidx = lax.broadcasted_iota(jnp.int32, logits.shape, 0)
    work = logits
    vals, idxs = [], []
    picked = jnp.zeros(logits.shape, F32)
    for _ in range(TOP_K):
        mx = jnp.max(work, axis=0, keepdims=True)
        ix = jnp.min(jnp.where(work == mx, eidx, N_EXPERTS), axis=0, keepdims=True)
        sel = eidx == ix
        vals.append(mx)
        idxs.append(ix)
        picked = jnp.where(sel, 1.0, picked)
        work = jnp.where(sel, -jnp.inf, work)
    es = [jnp.exp(v - vals[0]) for v in vals]
    tot = es[0] + es[1] + es[2] + es[3]
    jj = lax.broadcasted_iota(jnp.int32, (tm, tm), 0)
    ii = lax.broadcasted_iota(jnp.int32, (tm, tm), 1)
    upper = jnp.where(jj <= ii, 1.0, 0.0).astype(BF16)
    incl = _dot(picked.astype(BF16), upper)
    carry = carry_ref[:, 0:1]
    excl = incl - picked + carry
    for k in range(TOP_K):
        idx_ref[k:k + 1, :] = idxs[k]
        gate_ref[k:k + 1, :] = es[k] / tot
        rk = jnp.sum(jnp.where(eidx == idxs[k], excl, 0.0), axis=0, keepdims=True)
        rank_ref[k:k + 1, :] = rk.astype(jnp.int32)
    new_carry = carry + jnp.sum(picked, axis=1, keepdims=True)
    carry_ref[...] = jnp.broadcast_to(new_carry, carry_ref.shape)
    cnt_ref[...] = jnp.broadcast_to(new_carry, cnt_ref.shape).astype(jnp.int32)


def _router(xs, g2, mods_l, w_router, b_router, dims):
    bsz, seq, ctx = dims
    r, d = xs.shape
    tm = _wide_tile(dims)
    lt = seq // tm
    tok_spec = pl.BlockSpec((TOP_K, tm), lambda i: (0, i))
    return pl.pallas_call(
        _router_kernel,
        grid=(r // tm,),
        in_specs=[pl.BlockSpec((tm, d), lambda i: (i, 0)),
                  pl.BlockSpec((1, d), lambda i: (0, 0)),
                  _mod_spec(3, lt, bsz, d), _mod_spec(4, lt, bsz, d),
                  pl.BlockSpec((N_EXPERTS, d), lambda i: (0, 0)),
                  pl.BlockSpec((N_EXPERTS, 1), lambda i: (0, 0))],
        out_specs=[pl.BlockSpec((tm, d // 2), lambda i: (i, 0)), tok_spec, tok_spec, tok_spec,
                   pl.BlockSpec((N_EXPERTS, LANES), lambda i: (0, 0))],
        out_shape=[jax.ShapeDtypeStruct((r, d // 2), jnp.int32),
                   jax.ShapeDtypeStruct((TOP_K, r), jnp.int32),
                   jax.ShapeDtypeStruct((TOP_K, r), F32),
                   jax.ShapeDtypeStruct((TOP_K, r), jnp.int32),
                   jax.ShapeDtypeStruct((N_EXPERTS, LANES), jnp.int32)],
        scratch_shapes=[pltpu.VMEM((N_EXPERTS, LANES), F32)],
        compiler_params=_params(("arbitrary",)),
        name="norm_mod_router_top4",
    )(xs, g2.reshape(1, d), mods_l, mods_l, w_router.T, b_router.reshape(N_EXPERTS, 1))


def _pack_bf16_pairs(x):
    w = x.shape[1]
    xr = x.astype(BF16).astype(F32)
    hi = lax.bitcast_convert_type(xr[:, :w // 2], jnp.int32)
    lo = lax.bitcast_convert_type(xr[:, w // 2:], jnp.int32)
    return hi | lax.shift_right_logical(lo, 16)


def _unpack_bf16_pairs(xp):
    hi = lax.bitcast_convert_type(xp & jnp.int32(-65536), F32)
    lo = lax.bitcast_convert_type(lax.shift_left(xp, 16), F32)
    return hi, lo


def _expert_kernel(be_ref, nb_ref, first_ref, slot_ref, next_ref, x_ref, wgu_hbm, bgu_ref, wdn_hbm, bdn_ref, *rest,
                   layer):
    y_ref, wgu_buf, wdn_buf, wgu_bf, wdn_bf, sems = rest[-6:]
    def fetch(e, s):
        return (pltpu.make_async_copy(wgu_hbm.at[layer, e], wgu_buf.at[s], sems.at[s, 0]),
                pltpu.make_async_copy(wdn_hbm.at[layer, e], wdn_buf.at[s], sems.at[s, 1]))

    def block(i, rows):
        active = i < nb_ref[0]
        slot = slot_ref[i]
        e = be_ref[i]

        @pl.when(jnp.logical_and(active, first_ref[i] == 1))
        def _():
            @pl.when(i == 0)
            def _():
                for cp in fetch(be_ref[0], 0):
                    cp.start()

            for cp in fetch(e, slot):
                cp.wait()
            wgu_bf[...] = wgu_buf[slot].astype(BF16)
            wdn_bf[...] = wdn_buf[slot].astype(BF16)

            @pl.when(next_ref[i] >= 0)
            def _():
                for cp in fetch(next_ref[i], 1 - slot):
                    cp.start()

        @pl.when(active)
        def _():
            de = wdn_bf.shape[0]
            half = x_ref.shape[1]
            xa, xb = _unpack_bf16_pairs(x_ref[rows, :])
            gu = _dot(xa.astype(BF16), wgu_bf[:half, :]) + _dot(xb.astype(BF16), wgu_bf[half:, :]) + bgu_ref[e]
            g = jnp.minimum(gu[:, :de], SWIGLU_LIMIT)
            u = jnp.clip(gu[:, de:], -SWIGLU_LIMIT, SWIGLU_LIMIT)
            a = (u + 1.0) * g * _sigmoid(SWIGLU_ALPHA * g)
            y_ref[rows, :] = _pack_bf16_pairs(_dot(a.astype(BF16), wdn_bf[...]) + bdn_ref[e])

    step = pl.program_id(0)
    for sub in range(MOE_BLOCKS_PER_STEP):
        block(step * MOE_BLOCKS_PER_STEP + sub, slice(sub * MOE_BLOCK, (sub + 1) * MOE_BLOCK))


def _experts(xb, block_expert, nb_used, grp_first, grp_slot, grp_next, w_gu, b_gu, w_dn, b_dn, layer,
             total_rows, row_offset=0, y_prev=None):
    nrows, half = xb.shape
    bm = MOE_BLOCK
    depth, ne, d, two_de = w_gu.shape
    de = two_de // 2
    nsp = 5
    rows_step = bm * MOE_BLOCKS_PER_STEP
    assert nrows % rows_step == 0 and row_offset % rows_step == 0
    step0 = row_offset // rows_step
    in_specs = [pl.BlockSpec((rows_step, half), lambda i, *_: (i, 0)),
                pl.BlockSpec(memory_space=pl.ANY),
                pl.BlockSpec((None, ne, 1, two_de), lambda i, *_: (layer, 0, 0, 0)),
                pl.BlockSpec(memory_space=pl.ANY),
                pl.BlockSpec((None, ne, 1, d), lambda i, *_: (layer, 0, 0, 0))]
    args = [block_expert, nb_used, grp_first, grp_slot, grp_next, xb, w_gu, b_gu.reshape(depth, ne, 1, two_de),
            w_dn, b_dn.reshape(depth, ne, 1, d)]
    aliases = {}
    if y_prev is not None:
        in_specs.append(pl.BlockSpec(memory_space=pl.ANY))
        args.append(y_prev)
        aliases = {len(args) - 1: 0}
    grid_spec = pltpu.PrefetchScalarGridSpec(
        num_scalar_prefetch=nsp,
        grid=(nrows // rows_step,),
        in_specs=in_specs,
        out_specs=pl.BlockSpec((rows_step, d // 2), lambda i, *_: (step0 + i, 0)),
        scratch_shapes=[pltpu.VMEM((2, d, two_de), F32), pltpu.VMEM((2, de, d), F32),
                        pltpu.VMEM((d, two_de), BF16), pltpu.VMEM((de, d), BF16),
                        pltpu.SemaphoreType.DMA((2, 2))],
    )
    return pl.pallas_call(
        functools.partial(_expert_kernel, layer=layer),
        grid_spec=grid_spec,
        out_shape=jax.ShapeDtypeStruct((total_rows, d // 2), jnp.int32),
        input_output_aliases=aliases,
        compiler_params=_params(("arbitrary",)),
        name="moe_expert_ffn",
    )(*args)


def _sc_row_tokens(dest, n_rows):
    top_k, r = dest.shape
    lanes = SC_LANES
    chunk = r // 2
    assert n_rows % lanes == 0 and r % 2 == 0 and chunk % lanes == 0
    fill_mask = (1 << (r.bit_length() - 1)) - 1
    mesh = plsc.VectorSubcoreMesh(core_axis_name="c", subcore_axis_name="s",
                                  num_cores=SC_CORES, num_subcores=SC_SUBCORES)

    def body(dest_hbm, out_hbm, dest_v, table_v):
        wid = lax.axis_index("s") * SC_CORES + lax.axis_index("c")

        @pl.when(wid == 0)
        def _():
            lane_id = lax.iota(jnp.int32, lanes)

            @pl.loop(0, n_rows // lanes, unroll=8)
            def _(c):
                table_v[pl.ds(c * lanes, lanes)] = (lane_id + c * lanes) & fill_mask

            for k in range(top_k):
                for half in range(2):
                    t0 = half * chunk
                    pltpu.sync_copy(dest_hbm.at[pl.ds(k * r + t0, chunk)], dest_v)

                    @pl.loop(0, chunk // lanes, unroll=8)
                    def _(v):
                        d = dest_v[pl.ds(v * lanes, lanes)]
                        plsc.store_scatter(table_v, [d], lane_id + (t0 + v * lanes))

            pltpu.sync_copy(table_v, out_hbm)

    cp = pltpu.CompilerParams(needs_layout_passes=False)
    return pl.kernel(
        body,
        out_type=jax.ShapeDtypeStruct((n_rows,), jnp.int32),
        mesh=mesh,
        scratch_types=[pltpu.VMEM((chunk,), jnp.int32), pltpu.VMEM((n_rows,), jnp.int32)],
        compiler_params=cp,
        name="sc_row_tokens",
    )(dest.reshape(top_k * r))


def _sc_gather_rows(table, idx):
    n = idx.shape[0]
    width = table.shape[1]
    nw = SC_CORES * SC_SUBCORES
    nb = SC_GATHER_ROWS
    per_w = n // nw
    assert n % nw == 0 and per_w % nb == 0, (n, nw, nb)
    steps = per_w // nb
    mesh = plsc.VectorSubcoreMesh(core_axis_name="c", subcore_axis_name="s",
                                  num_cores=SC_CORES, num_subcores=SC_SUBCORES)

    def body(table_hbm, idx_hbm, out_hbm, idx_a, idx_b, rows_a, rows_b, sem_a, sem_b):
        wid = lax.axis_index("s") * SC_CORES + lax.axis_index("c")
        base = wid * per_w
        slots = ((idx_a, rows_a, sem_a), (idx_b, rows_b, sem_b))

        def gather(j, slot):
            idx_v, rows_v, sem = slots[slot]
            return pltpu.make_async_copy(table_hbm.at[idx_v], rows_v, sem)

        def start(j, slot):
            off = pl.multiple_of(base + j * nb, 8)
            pltpu.sync_copy(idx_hbm.at[pl.ds(off, nb)], slots[slot][0])
            gather(j, slot).start()

        def finish(j, slot):
            off = pl.multiple_of(base + j * nb, 8)
            gather(j, slot).wait()
            pltpu.sync_copy(slots[slot][1], out_hbm.at[pl.ds(off, nb)])

        start(0, 0)

        @pl.loop(0, steps // 2)
        def _(p):
            j = 2 * p
            start(j + 1, 1)
            finish(j, 0)
            if steps % 2 == 1:
                start(j + 2, 0)
            else:
                @pl.when(j + 2 < steps)
                def _():
                    start(j + 2, 0)
            finish(j + 1, 1)

        if steps % 2 == 1:
            finish(steps - 1, 0)

    return pl.kernel(
        body,
        out_type=jax.ShapeDtypeStruct((n, width), table.dtype),
        mesh=mesh,
        scratch_types=[pltpu.VMEM((nb,), jnp.int32), pltpu.VMEM((nb,), jnp.int32),
                       pltpu.VMEM((nb, width), table.dtype), pltpu.VMEM((nb, width), table.dtype),
                       pltpu.SemaphoreType.DMA, pltpu.SemaphoreType.DMA],
        name="sc_gather_rows",
    )(table, idx)


def _combine_kernel(xs_ref, yg_ref, gt_ref, gate_ref, fg_ref, o_ref, *, final):
    f_hi = f_lo = None
    for k in range(TOP_K):
        y_hi, y_lo = _unpack_bf16_pairs(yg_ref[k])
        gk = gt_ref[:, k:k + 1]
        f_hi = y_hi * gk if f_hi is None else f_hi + y_hi * gk
        f_lo = y_lo * gk if f_lo is None else f_lo + y_lo * gk
    out = xs_ref[...] + gate_ref[...] * jnp.concatenate([f_hi, f_lo], axis=1)
    if final:
        out = _rms(out) * fg_ref[...]
    o_ref[...] = out


def _combine(xs, yg, gate_t, mods_l, final_g, final, dims):
    bsz, seq, ctx = dims
    r, d = xs.shape
    tm = _wide_tile(dims)
    lt = seq // tm
    rows_out = bsz * seq if final else r
    return pl.pallas_call(
        functools.partial(_combine_kernel, final=final),
        grid=(rows_out // tm,),
        in_specs=[pl.BlockSpec((tm, d), lambda i: (i, 0)),
                  pl.BlockSpec((TOP_K, tm, d // 2), lambda i: (0, i, 0)),
                  pl.BlockSpec((tm, TOP_K), lambda i: (i, 0)),
                  _mod_spec(5, lt, bsz, d),
                  pl.BlockSpec((1, d), lambda i: (0, 0))],
        out_specs=pl.BlockSpec((tm, d), lambda i: (i, 0)),
        out_shape=jax.ShapeDtypeStruct((rows_out, d), F32),
        compiler_params=_params(("arbitrary",)),
        name="moe_combine_residual",
    )(xs, yg, gate_t, mods_l, final_g.reshape(1, d))


def _moe_layer(xs, mods_l, g2, w_router, b_router, w_gu, b_gu, w_dn, b_dn, final_g, layer, final, dims):
    r, d = xs.shape
    bm = MOE_BLOCK
    hp, top_idx, gate, rank, cnt = _router(xs, g2, mods_l, w_router, b_router, dims)
    counts = cnt[:, 0]
    padded = (counts + bm - 1) // bm * bm
    pad_ends = jnp.cumsum(padded)
    pad_starts = pad_ends - padded
    eids = jnp.arange(N_EXPERTS, dtype=jnp.int32)
    onehot = top_idx[None] == eids[:, None, None]
    dest = jnp.sum(jnp.where(onehot, pad_starts[:, None, None], 0), axis=0) + rank
    n_blocks = -(-(r * TOP_K + N_EXPERTS * (bm - 1)) // bm)
    n_blocks = -(-n_blocks // MOE_BLOCKS_PER_STEP) * MOE_BLOCKS_PER_STEP
    blk_start = jnp.arange(n_blocks, dtype=jnp.int32) * bm
    block_expert = jnp.minimum(jnp.sum((pad_ends[None, :] <= blk_start[:, None]).astype(jnp.int32), axis=1),
                               N_EXPERTS - 1)
    nb_total = pad_ends[-1] // bm
    nb_used = nb_total.astype(jnp.int32).reshape(1)
    blk = jnp.arange(n_blocks, dtype=jnp.int32)
    prev_e = jnp.concatenate([jnp.full((1,), -1, jnp.int32), block_expert[:-1]])
    grp_first = ((block_expert != prev_e) & (blk < nb_total)).astype(jnp.int32)
    grp_slot = (jnp.cumsum(grp_first) - 1) % 2
    later = (eids[None, :] > eids[:, None]) & (counts[None, :] > 0)
    next_of_e = jnp.min(jnp.where(later, eids[None, :], N_EXPERTS), axis=1)
    next_of_e = jnp.where(next_of_e == N_EXPERTS, -1, next_of_e)
    grp_next = jnp.sum(jnp.where(block_expert[:, None] == eids[None, :], next_of_e[None, :], 0), axis=1)
    row_tok = _sc_row_tokens(dest, n_blocks * bm)
    gran = SC_CORES * SC_SUBCORES * SC_GATHER_ROWS // bm
    split = -(-(n_blocks // 2) // gran) * gran
    assert 0 < split < n_blocks and (n_blocks - split) % gran == 0 and split % MOE_BLOCKS_PER_STEP == 0
    total_rows = n_blocks * bm
    grp_slot = grp_slot.astype(jnp.int32)
    grp_next = grp_next.astype(jnp.int32)
    next_start = jnp.sum(jnp.where(grp_next[:, None] == eids[None, :], (pad_starts // bm)[None, :], 0), axis=1)
    next_a = jnp.where((grp_next >= 0) & (next_start < split), grp_next, -1)[:split]
    nb_a = jnp.minimum(nb_total, split).astype(jnp.int32).reshape(1)
    nb_b = jnp.clip(nb_total - split, 0, n_blocks - split).astype(jnp.int32).reshape(1)
    first_b = grp_first[split:].at[0].set((nb_total > split).astype(jnp.int32))
    slot_b = ((jnp.cumsum(first_b) - 1) % 2).astype(jnp.int32)
    xb_a = _sc_gather_rows(hp, row_tok[:split * bm])
    xb_b = _sc_gather_rows(hp, row_tok[split * bm:])
    yb = _experts(xb_a, block_expert[:split], nb_a, grp_first[:split], grp_slot[:split], next_a,
                  w_gu, b_gu, w_dn, b_dn, layer, total_rows)
    yb = _experts(xb_b, block_expert[split:], nb_b, first_b, slot_b, grp_next[split:],
                  w_gu, b_gu, w_dn, b_dn, layer, total_rows, row_offset=split * bm, y_prev=yb)
    yg = _sc_gather_rows(yb, dest.reshape(-1)).reshape(TOP_K, r, d // 2)
    return _combine(xs, yg, gate.T, mods_l, final_g, final, dims)


def kernel(x, c, ctx, c_ctx, ada_w, ada_b, norm1_g, norm2_g, ssd_w_in, ssd_conv_w, ssd_conv_b, ssd_dt_bias,
           ssd_a_log, ssd_d, ssd_norm_w, ssd_w_out, diff_w_qkv, diff_lam, diff_subln_w, diff_w_out, ml_w_in,
           ml_b_gates, ml_norm_w, ml_w_out, moe_w_router, moe_b_router, moe_w_gu, moe_b_gu, moe_w_dn, moe_b_dn,
           final_g):
    bsz, seq, d = x.shape
    n_ctx = ctx.shape[1]
    depth = ada_w.shape[0]
    dims = (bsz, seq, n_ctx)
    n_lat = bsz * seq
    xs = jnp.concatenate([x.reshape(n_lat, d), ctx.reshape(bsz * n_ctx, d)], axis=0)
    cond_rows = jnp.zeros((8, d), F32).at[:bsz].set(c).at[bsz].set(c_ctx)
    mods = _mods(cond_rows, ada_w, ada_b)
    mods = mods[:, :, :bsz + 1].reshape(depth, N_MOD, bsz + 1, 1, d)
    for i in range(depth):
        mods_l = mods[i]
        kind, j = i % 3, i // 3
        if kind == 0:
            xs = _ssd_layer(xs, mods_l, norm1_g[i], ssd_w_in[j], ssd_conv_w[j], ssd_conv_b[j], ssd_dt_bias[j],
                            ssd_a_log[j], ssd_d[j], ssd_norm_w[j], ssd_w_out[j], dims)
        elif kind == 1:
            lambda_init = 0.8 - 0.6 * math.exp(-0.3 * i)
            xs = _diff_layer(xs, mods_l, norm1_g[i], diff_w_qkv[j], diff_lam[j], diff_subln_w[j], diff_w_out[j],
                             lambda_init, dims)
        else:
            xs = _mlstm_layer(xs, mods_l, norm1_g[i], ml_w_in[j], ml_b_gates[j], ml_norm_w[j], ml_w_out[j], dims)
        xs = _moe_layer(xs, mods_l, norm2_g[i], moe_w_router[i], moe_b_router[i], moe_w_gu, moe_b_gu,
                        moe_w_dn, moe_b_dn, final_g, i, i == depth - 1, dims)
    return xs.reshape(bsz, seq, d)
```

```python
import functools
import math

import numpy as np
import jax
import jax.numpy as jnp
from jax import lax
from jax.experimental import pallas as pl
from jax.experimental.pallas import tpu as pltpu
from jax.experimental.pallas import tpu_sc as plsc

F32 = jnp.float32
BF16 = jnp.bfloat16
HIGHEST = lax.Precision.HIGHEST

GRID_W = 64
RMS_EPS = 1e-6
N_MOD = 6
SSD_HEAD_DIM = 64
SSD_HEADS = 32
SSD_GROUPS = 4
SSD_STATE = 128
SSD_INNER = SSD_HEADS * SSD_HEAD_DIM
SSD_BC = SSD_GROUPS * SSD_STATE
SSD_CONV_DIM = SSD_INNER + 2 * SSD_BC
DIFF_HEADS = 8
DIFF_HEAD_DIM = 64
DIFF_V_DIM = 128
ROPE_BASE = 10000.0
ROPE_Q = DIFF_HEAD_DIM // 4
ML_HEADS = 4
ML_QK_DIM = 128
ML_V_DIM = 256
ML_QK = ML_HEADS * ML_QK_DIM
ML_V = ML_HEADS * ML_V_DIM
N_EXPERTS = 32
TOP_K = 4
SWIGLU_LIMIT = 7.0
SWIGLU_ALPHA = 1.702
LOG2_E = 1.4426950408889634

LANES = 128
CHUNK = 128
ROW_TILE = 256
ROW_TILE_WIDE = 512
MOE_BLOCK = 256
MOE_BLOCKS_PER_STEP = 2
ATTN_Q_TILE = 512
ATTN_K_TILE = 512
ATTN_UNROLL = 16
SC_CORES = 2
SC_SUBCORES = 16
SC_LANES = 16
SC_GATHER_ROWS = 64
VMEM_LIMIT = 56 * 1024 * 1024


def _wide_tile(dims):
    bsz, seq, ctx = dims
    return min(ROW_TILE_WIDE, seq, bsz * ctx)


def _params(sem):
    return pltpu.CompilerParams(dimension_semantics=sem, vmem_limit_bytes=VMEM_LIMIT)


def _dot(a, b):
    return jnp.dot(a, b, preferred_element_type=F32)


def _dot_nt(a, b):
    return lax.dot_general(a, b, (((1,), (1,)), ((), ())), preferred_element_type=F32)


def _dot_tn(a, b):
    return lax.dot_general(a, b, (((0,), (0,)), ((), ())), preferred_element_type=F32)


def _rms(x):
    return x * lax.rsqrt(jnp.mean(x * x, axis=-1, keepdims=True) + RMS_EPS)


def _sigmoid(x):
    return 1.0 / (1.0 + jnp.exp(-x))


def _silu(x):
    return x * _sigmoid(x)


def _softplus(x):
    return jnp.maximum(x, 0.0) + jnp.log(1.0 + jnp.exp(-jnp.abs(x)))


def _log_sigmoid(x):
    return -_softplus(-x)


def _mod_spec(which, lat_tiles_per_batch, bsz, d):
    return pl.BlockSpec((None, None, 1, d),
                        lambda i: (which, jnp.minimum(i // lat_tiles_per_batch, bsz), 0, 0))


def _mods_kernel(c_ref, w_ref, b_ref, o_ref):
    c = c_ref[...]
    cond = _silu(c)
    o_ref[...] = jnp.dot(cond, w_ref[...], precision=HIGHEST, preferred_element_type=F32) + b_ref[...]


def _mods(cond_rows, ada_w, ada_b):
    depth, d, _ = ada_w.shape
    nr = cond_rows.shape[0]
    return pl.pallas_call(
        _mods_kernel,
        grid=(depth, N_MOD),
        in_specs=[pl.BlockSpec((nr, d), lambda l, j: (0, 0)),
                  pl.BlockSpec((None, d, d), lambda l, j: (l, 0, j)),
                  pl.BlockSpec((None, 1, d), lambda l, j: (l, 0, j))],
        out_specs=pl.BlockSpec((None, None, nr, d), lambda l, j: (l, j, 0, 0)),
        out_shape=jax.ShapeDtypeStruct((depth, N_MOD, nr, d), F32),
        compiler_params=_params(("arbitrary", "arbitrary")),
        name="adaln_mods",
    )(cond_rows, ada_w, ada_b.reshape(depth, 1, N_MOD * d))


def _norm_mod(x, g, sh, sc):
    return (_rms(x) * g) * (1.0 + sc) + sh


def _inproj_kernel(x_ref, g_ref, sh_ref, sc_ref, w_ref, w2_ref, o_ref, o2_ref, *, n_chunk):
    hb = _norm_mod(x_ref[...], g_ref[...], sh_ref[...], sc_ref[...]).astype(BF16)
    n = o_ref.shape[1]
    for n0 in range(0, n, n_chunk):
        o_ref[:, n0:n0 + n_chunk] = _dot(hb, w_ref[:, n0:n0 + n_chunk])
    o2_ref[...] = _dot(hb, w2_ref[...])


def _col_chunk(n):
    for c in (512, 384, 256, 128):
        if n % c == 0:
            return c
    raise ValueError(n)


def _inproj(xs, g, mods_l, w_bf16, w2_bf16, dims):
    bsz, seq, ctx = dims
    r, d = xs.shape
    n = w_bf16.shape[1]
    n2 = w2_bf16.shape[1]
    tm = _wide_tile(dims)
    lt = seq // tm
    return pl.pallas_call(
        functools.partial(_inproj_kernel, n_chunk=_col_chunk(n)),
        grid=(r // tm,),
        in_specs=[pl.BlockSpec((tm, d), lambda i: (i, 0)),
                  pl.BlockSpec((1, d), lambda i: (0, 0)),
                  _mod_spec(0, lt, bsz, d), _mod_spec(1, lt, bsz, d),
                  pl.BlockSpec((d, n), lambda i: (0, 0)),
                  pl.BlockSpec((d, n2), lambda i: (0, 0))],
        out_specs=[pl.BlockSpec((tm, n), lambda i: (i, 0)), pl.BlockSpec((tm, n2), lambda i: (i, 0))],
        out_shape=[jax.ShapeDtypeStruct((r, n), F32), jax.ShapeDtypeStruct((r, n2), F32)],
        compiler_params=_params(("arbitrary",)),
        name="norm_mod_inproj",
    )(xs, g.reshape(1, d), mods_l, mods_l, w_bf16, w2_bf16)


def _ssd_inproj_kernel(first_ref, last_ref, x_ref, xp_ref, xn_ref, g_ref, sh_ref, sc_ref, w_ref, wdt_ref, cw_ref,
                       cb_ref, z_ref, xbc_ref, dt_ref, *, n_chunk):
    i = pl.program_id(0)
    g, sh, sc = g_ref[...], sh_ref[...], sc_ref[...]
    hb = _norm_mod(x_ref[...], g, sh, sc).astype(BF16)
    halo = jnp.concatenate([xp_ref[...], xn_ref[...]], axis=0)
    hb_ext = jnp.concatenate([hb, _norm_mod(halo, g, sh, sc).astype(BF16)], axis=0)
    tm = hb.shape[0]
    nz = z_ref.shape[1]
    for n0 in range(0, nz, n_chunk):
        z_ref[:, n0:n0 + n_chunk] = _dot(hb, w_ref[:, n0:n0 + n_chunk])
    dt_ref[...] = _dot(hb, wdt_ref[...])
    row = lax.broadcasted_iota(jnp.int32, (tm, n_chunk), 0)
    keep_prev = first_ref[i] == 0
    keep_next = last_ref[i] == 0
    for n0 in range(0, xbc_ref.shape[1], n_chunk):
        cs = slice(n0, n0 + n_chunk)
        t = _dot(hb_ext, w_ref[:, nz + n0:nz + n0 + n_chunk])
        cur = t[:tm]
        prev_row = jnp.where(keep_prev, t[tm + 7:tm + 8], 0.0)
        next_row = jnp.where(keep_next, t[tm + 8:tm + 9], 0.0)
        down = jnp.where(row == 0, prev_row, pltpu.roll(cur, 1, 0))
        up = jnp.where(row == tm - 1, next_row, pltpu.roll(cur, tm - 1, 0))
        y = cw_ref[0:1, cs] * down + cw_ref[1:2, cs] * cur + cw_ref[2:3, cs] * up + cb_ref[:, cs]
        xbc_ref[:, cs] = _silu(y).astype(xbc_ref.dtype)


def _ssd_inproj(xs, g, mods_l, w_main, w_dt, conv_w, conv_b, dims):
    bsz, seq, ctx = dims
    r, d = xs.shape
    tm = ROW_TILE
    lt = seq // tm
    n2 = w_dt.shape[1]
    starts = [b * seq for b in range(bsz)] + [bsz * seq + b * ctx for b in range(bsz)]
    ends = [s + (seq if k < bsz else ctx) for k, s in enumerate(starts)]
    first = np.array([1 if (i * tm) in starts else 0 for i in range(r // tm)], np.int32)
    last = np.array([1 if ((i + 1) * tm) in ends else 0 for i in range(r // tm)], np.int32)
    sub = tm // 8
    nblk8 = r // 8

    def mod_spec(which):
        return pl.BlockSpec((None, None, 1, d), lambda i, f, l: (which, jnp.minimum(i // lt, bsz), 0, 0))

    def const(shape):
        return pl.BlockSpec(shape, lambda i, f, l: (0, 0))

    grid_spec = pltpu.PrefetchScalarGridSpec(
        num_scalar_prefetch=2,
        grid=(r // tm,),
        in_specs=[pl.BlockSpec((tm, d), lambda i, f, l: (i, 0)),
                  pl.BlockSpec((8, d), lambda i, f, l: (jnp.maximum(i * sub - 1, 0), 0)),
                  pl.BlockSpec((8, d), lambda i, f, l: (jnp.minimum((i + 1) * sub, nblk8 - 1), 0)),
                  const((1, d)), mod_spec(0), mod_spec(1),
                  const((d, SSD_INNER + SSD_CONV_DIM)), const((d, n2)),
                  const((3, SSD_CONV_DIM)), const((1, SSD_CONV_DIM))],
        out_specs=[pl.BlockSpec((tm, SSD_INNER), lambda i, f, l: (i, 0)),
                   pl.BlockSpec((tm, SSD_CONV_DIM), lambda i, f, l: (i, 0)),
                   pl.BlockSpec((tm, n2), lambda i, f, l: (i, 0))],
    )
    return pl.pallas_call(
        functools.partial(_ssd_inproj_kernel, n_chunk=512),
        grid_spec=grid_spec,
        out_shape=[jax.ShapeDtypeStruct((r, SSD_INNER), F32),
                   jax.ShapeDtypeStruct((r, SSD_CONV_DIM), BF16),
                   jax.ShapeDtypeStruct((r, n2), F32)],
        compiler_params=_params(("arbitrary",)),
        name="ssd_inproj_conv_silu",
    )(jnp.asarray(first), jnp.asarray(last), xs, xs, xs, g.reshape(1, d), mods_l, mods_l, w_main, w_dt,
      conv_w, conv_b.reshape(1, -1))


def _chunk_block(b, d, s, bsz, seq, ctx):
    nc_ctx = ctx // CHUNK
    nc_lat = seq // CHUNK
    in_ctx = s < nc_ctx
    pc = jnp.where(d == 0, s, nc_ctx - 1 - s)
    pls = s - nc_ctx
    plat = jnp.where(d == 0, pls, nc_lat - 1 - pls)
    ctx_blk = bsz * nc_lat + b * nc_ctx + pc
    lat_blk = b * nc_lat + plat
    return jnp.where(in_ctx, ctx_blk, lat_blk)


def _dir_masks(d):
    ii = lax.broadcasted_iota(jnp.int32, (CHUNK, CHUNK), 0)
    jj = lax.broadcasted_iota(jnp.int32, (CHUNK, CHUNK), 1)
    sign = 1 - 2 * d
    return (ii - jj) * sign >= 0


def _ssd_scan_kernel(xf_ref, bf_ref, cf_ref, dtf_ref, xb_ref, bb_ref, cb_ref, dtb_ref, bias_ref, a_ref,
                     yf_ref, yb_ref, st_ref):
    s = pl.program_id(1)

    @pl.when(s == 0)
    def _():
        st_ref[...] = jnp.zeros_like(st_ref)

    _ssd_chunk(0, xf_ref, bf_ref, cf_ref, dtf_ref, bias_ref, a_ref, yf_ref, st_ref)
    _ssd_chunk(1, xb_ref, bb_ref, cb_ref, dtb_ref, bias_ref, a_ref, yb_ref, st_ref)


def _ssd_chunk(d, x_ref, b_ref, c_ref, dt_ref, bias_ref, a_ref, y_ref, st_ref):
    mask = _dir_masks(d)
    tri = jnp.where(mask, 1.0, 0.0).astype(F32)
    dt = _softplus(dt_ref[...] + bias_ref[d])
    dta = dt * a_ref[d]
    la = jnp.dot(tri, dta, precision=HIGHEST, preferred_element_type=F32)
    total = jnp.sum(dta, axis=0, keepdims=True)
    la_t = la.T
    dt_t = dt.T
    ws_t = (jnp.exp(total - la) * dt).T
    e_tot = jnp.exp(total)
    lane = lax.broadcasted_iota(jnp.int32, (CHUNK, LANES), 1)
    lo_half = lane < SSD_HEAD_DIM
    hg = SSD_HEADS // SSD_GROUPS
    for g in range(SSD_GROUPS):
        bg = b_ref[:, g * SSD_STATE:(g + 1) * SSD_STATE]
        cg = c_ref[:, g * SSD_STATE:(g + 1) * SSD_STATE]
        cb = _dot_nt(cg, bg)
        bg_t = bg.astype(F32).T
        for pr in range(hg // 2):
            h0 = g * hg + 2 * pr
            col0 = slice((h0 * SSD_HEAD_DIM), (h0 + 2) * SSD_HEAD_DIM)
            xpb = x_ref[:, col0]
            ys, sts, la_cols = [], [], []
            for h in (h0, h0 + 1):
                la_col = jnp.broadcast_to(la[:, h:h + 1], (CHUNK, LANES))
                la_cols.append(la_col)
                w = jnp.exp(jnp.where(mask, la_col - la_t[h:h + 1, :], -jnp.inf)) * cb * dt_t[h:h + 1, :]
                ys.append(_dot(w.astype(BF16), xpb))
                sts.append(_dot((bg_t * ws_t[h:h + 1, :]).astype(BF16), xpb))
            st = st_ref[d, g, :, pr * LANES:(pr + 1) * LANES]
            e_pair = jnp.exp(jnp.where(lo_half, la_cols[0], la_cols[1]))
            y_pair = jnp.where(lo_half, ys[0], ys[1]) + e_pair * _dot(cg, st.astype(BF16))
            y_ref[:, col0] = y_pair.astype(y_ref.dtype)
            et_pair = jnp.where(lo_half[0:1], e_tot[:, h0:h0 + 1], e_tot[:, h0 + 1:h0 + 2])
            st_ref[d, g, :, pr * LANES:(pr + 1) * LANES] = et_pair * st + jnp.where(lo_half, sts[0], sts[1])


def _ssd_scan(xbc, dt_raw, dt_bias_pad, a_pad, dims):
    bsz, seq, ctx = dims
    r = xbc.shape[0]
    nsteps = (seq + ctx) // CHUNK
    blk = functools.partial(_chunk_block, bsz=bsz, seq=seq, ctx=ctx)
    xw = SSD_INNER
    def chunk_specs(d):
        return [pl.BlockSpec((CHUNK, xw), lambda b, s: (blk(b, d, s), 0)),
                pl.BlockSpec((CHUNK, SSD_BC), lambda b, s: (blk(b, d, s), xw // SSD_BC)),
                pl.BlockSpec((CHUNK, SSD_BC), lambda b, s: (blk(b, d, s), xw // SSD_BC + 1)),
                pl.BlockSpec((CHUNK, LANES), lambda b, s: (blk(b, d, s), d))]

    const = pl.BlockSpec((2, 1, LANES), lambda b, s: (0, 0, 0))
    y_shape = jax.ShapeDtypeStruct((r, xw), BF16)
    return pl.pallas_call(
        _ssd_scan_kernel,
        grid=(bsz, nsteps),
        in_specs=chunk_specs(0) + chunk_specs(1) + [const, const],
        out_specs=[pl.BlockSpec((CHUNK, xw), lambda b, s: (blk(b, 0, s), 0)),
                   pl.BlockSpec((CHUNK, xw), lambda b, s: (blk(b, 1, s), 0))],
        out_shape=[y_shape, y_shape],
        scratch_shapes=[pltpu.VMEM((2, SSD_GROUPS, SSD_STATE, xw // SSD_GROUPS), F32)],
        compiler_params=_params(("arbitrary", "arbitrary")),
        name="ssd_scan",
    )(xbc, xbc, xbc, dt_raw, xbc, xbc, xbc, dt_raw, dt_bias_pad, a_pad)


def _ssd_out_kernel(yf_ref, yb_ref, xc_ref, z_ref, dexp_ref, nw_ref, w_ref, xs_ref, gate_ref, o_ref):
    y = yf_ref[...].astype(F32) + yb_ref[...].astype(F32) + dexp_ref[...] * xc_ref[...].astype(F32)
    y = y * _silu(z_ref[...])
    a = (_rms(y) * nw_ref[...]).astype(BF16)
    o_ref[...] = xs_ref[...] + gate_ref[...] * _dot(a, w_ref[...])


def _ssd_out(y_f, y_b, xbc, t_main, d_exp, norm_w, w_out_bf16, xs, mods_l, dims):
    bsz, seq, ctx = dims
    r, d = xs.shape
    tm = ROW_TILE
    lt = seq // tm
    xw = SSD_INNER
    return pl.pallas_call(
        _ssd_out_kernel,
        grid=(r // tm,),
        in_specs=[pl.BlockSpec((tm, xw), lambda i: (i, 0)),
                  pl.BlockSpec((tm, xw), lambda i: (i, 0)),
                  pl.BlockSpec((tm, xw), lambda i: (i, 0)),
                  pl.BlockSpec((tm, xw), lambda i: (i, 0)),
                  pl.BlockSpec((1, xw), lambda i: (0, 0)),
                  pl.BlockSpec((1, xw), lambda i: (0, 0)),
                  pl.BlockSpec((xw, d), lambda i: (0, 0)),
                  pl.BlockSpec((tm, d), lambda i: (i, 0)),
                  _mod_spec(2, lt, bsz, d)],
        out_specs=pl.BlockSpec((tm, d), lambda i: (i, 0)),
        out_shape=jax.ShapeDtypeStruct((r, d), F32),
        compiler_params=_params(("arbitrary",)),
        name="ssd_gated_norm_outproj",
    )(y_f, y_b, xbc, t_main, d_exp, norm_w.reshape(1, xw), w_out_bf16, xs, mods_l)


def _ssd_layer(xs, mods_l, g1, w_in, conv_w, conv_b, dt_bias, a_log, d_skip, norm_w, w_out, dims):
    d = xs.shape[1]
    main = SSD_INNER + SSD_CONV_DIM
    w_main = w_in[:, :main].astype(BF16)
    w_dt = jnp.zeros((d, 2 * LANES), F32)
    w_dt = w_dt.at[:, :SSD_HEADS].set(w_in[:, main:main + SSD_HEADS])
    w_dt = w_dt.at[:, LANES:LANES + SSD_HEADS].set(w_in[:, main + SSD_HEADS:]).astype(BF16)
    t_z, xbc, dt_raw = _ssd_inproj(xs, g1, mods_l, w_main, w_dt, conv_w, conv_b, dims)
    pad = jnp.zeros((2, 1, LANES - SSD_HEADS), F32)
    bias_pad = jnp.concatenate([dt_bias.astype(F32).reshape(2, 1, SSD_HEADS), pad], axis=-1)
    a_pad = jnp.concatenate([-jnp.exp(a_log.astype(F32)).reshape(2, 1, SSD_HEADS), pad], axis=-1)
    y_f, y_b = _ssd_scan(xbc, dt_raw, bias_pad, a_pad, dims)
    d_exp = jnp.repeat(d_skip.astype(F32), SSD_HEAD_DIM).reshape(1, SSD_INNER)
    return _ssd_out(y_f, y_b, xbc, t_z, d_exp, norm_w, w_out.astype(BF16), xs, mods_l, dims)


def _qkv_rope_kernel(x_ref, g_ref, sh_ref, sc_ref, w_ref, cos_ref, sin_ref, q_ref, k_ref, v_ref, *, n_lat_tiles):
    i = pl.program_id(0)
    hb = _norm_mod(x_ref[...], g_ref[...], sh_ref[...], sc_ref[...]).astype(BF16)
    d = x_ref.shape[1]
    is_ctx = i >= n_lat_tiles
    cos = jnp.where(is_ctx, 1.0, cos_ref[...])
    sin = jnp.where(is_ctx, 0.0, sin_ref[...])
    nrep = d // LANES
    cos = jnp.concatenate([cos] * nrep, axis=1)
    sin = jnp.concatenate([sin] * nrep, axis=1)

    def mm(c):
        return _dot(hb, w_ref[:, c * d:(c + 1) * d])

    q = mm(0) * cos + mm(3) * sin
    q_ref[...] = (q * (DIFF_HEAD_DIM ** -0.5 * LOG2_E)).astype(BF16)
    k_ref[...] = (mm(1) * cos + mm(4) * sin).astype(BF16)
    v_ref[...] = mm(2).astype(BF16)


def _rope_tables(seq):
    rows = seq // GRID_W
    row = jnp.repeat(jnp.arange(rows, dtype=F32), GRID_W)
    col = jnp.tile(jnp.arange(GRID_W, dtype=F32), rows)
    inv = ROPE_BASE ** (-jnp.arange(ROPE_Q, dtype=F32) / ROPE_Q)
    ang_r = row[:, None] * inv
    ang_c = col[:, None] * inv
    cos = jnp.concatenate([jnp.cos(ang_r), jnp.cos(ang_r), jnp.cos(ang_c), jnp.cos(ang_c)], axis=1)
    sin = jnp.concatenate([-jnp.sin(ang_r), jnp.sin(ang_r), -jnp.sin(ang_c), jnp.sin(ang_c)], axis=1)
    return jnp.tile(cos, (1, 2)), jnp.tile(sin, (1, 2))


def _rope_partner_cols(d):
    col = np.arange(d)
    within = col % (2 * ROPE_Q)
    return np.where(within < ROPE_Q, col + ROPE_Q, col - ROPE_Q)


def _qkv_rope(xs, g, mods_l, w_qkv, dims):
    bsz, seq, ctx = dims
    r, d = xs.shape
    tm = _wide_tile(dims)
    lt = seq // tm
    perm = _rope_partner_cols(d)
    wq, wk, wv = w_qkv[:, :d], w_qkv[:, d:2 * d], w_qkv[:, 2 * d:]
    w_all = jnp.concatenate([wq, wk, wv, wq[:, perm], wk[:, perm]], axis=1).astype(BF16)
    cos, sin = _rope_tables(seq)
    out = jax.ShapeDtypeStruct((r, d), BF16)
    row_spec = pl.BlockSpec((tm, d), lambda i: (i, 0))
    tab_spec = pl.BlockSpec((tm, LANES), lambda i: (i % lt, 0))
    return pl.pallas_call(
        functools.partial(_qkv_rope_kernel, n_lat_tiles=bsz * lt),
        grid=(r // tm,),
        in_specs=[row_spec, pl.BlockSpec((1, d), lambda i: (0, 0)),
                  _mod_spec(0, lt, bsz, d), _mod_spec(1, lt, bsz, d),
                  pl.BlockSpec((d, 5 * d), lambda i: (0, 0)), tab_spec, tab_spec],
        out_specs=[row_spec, row_spec, row_spec],
        out_shape=[out, out, out],
        compiler_params=_params(("arbitrary",)),
        name="norm_mod_qkv_rope",
    )(xs, g.reshape(1, d), mods_l, mods_l, w_all, cos, sin)


def _lane_fold(x, op):
    parts = [x[:, t * LANES:(t + 1) * LANES] for t in range(x.shape[1] // LANES)]
    return functools.reduce(op, parts)


def _attn_kernel(*refs, tk, unroll, with_latent, lambda_init):
    if with_latent:
        lam_ref, q_ref, kc_ref, vc_ref, kx_ref, vx_ref, nw_ref, o_ref, sc_ref, sx_ref, acc_ref, m_ref, l_ref = refs
        n_x = kx_ref.shape[0] // tk
    else:
        lam_ref, q_ref, kc_ref, vc_ref, nw_ref, _, o_ref, sc_ref, acc_ref, m_ref, l_ref = refs
    q = q_ref[...]
    tq = q.shape[0]
    lane = lax.broadcasted_iota(jnp.int32, q.shape, 1)
    zero = jnp.zeros_like(q)
    qq = jnp.concatenate([jnp.where(lane < DIFF_HEAD_DIM, q, zero),
                          jnp.where(lane >= DIFF_HEAD_DIM, q, zero)], axis=0)

    s = _dot_nt(qq, kc_ref[...])
    sc_ref[...] = s
    m_ref[...] = _lane_fold(s, jnp.maximum)

    always = pl.program_id(2) >= 0

    if with_latent:
        @pl.when(always)
        def _():
            def p1(j, m_):
                kj = kx_ref[pl.ds(pl.multiple_of(j * tk, tk), tk), :]
                sj = _dot_nt(qq, kj)
                sx_ref[j] = sj
                return jnp.maximum(m_, _lane_fold(sj, jnp.maximum))

            m_ref[...] = lax.fori_loop(0, n_x, p1, m_ref[...], unroll=unroll)

    mrow = jnp.max(m_ref[...], axis=1, keepdims=True)

    p = jnp.exp2(sc_ref[...] - mrow)
    acc_ref[...] = _dot(p.astype(BF16), vc_ref[...])
    l_ref[...] = _lane_fold(p, jnp.add)

    if with_latent:
        @pl.when(always)
        def _():
            def p2(j, l_):
                vj = vx_ref[pl.ds(pl.multiple_of(j * tk, tk), tk), :]
                pj = jnp.exp2(sx_ref[j] - mrow)
                acc_ref[...] += _dot(pj.astype(BF16), vj)
                return l_ + _lane_fold(pj, jnp.add)

            l_ref[...] = lax.fori_loop(0, n_x, p2, l_ref[...], unroll=unroll)

    on = acc_ref[...] / jnp.sum(l_ref[...], axis=1, keepdims=True)
    o = on[:tq] - lam_ref[0] * on[tq:]
    o = _rms(o) * nw_ref[...] * (1.0 - lambda_init)
    o_ref[...] = o.astype(o_ref.dtype)


def _diff_attention(q, k, v, lam_full, subln_w, lambda_init, dims):
    bsz, seq, ctx = dims
    r, d = q.shape
    tq = min(ATTN_Q_TILE, seq)
    tk = min(ATTN_K_TILE, seq)
    n_lat_q = seq // tq
    smem = pl.BlockSpec(memory_space=pltpu.SMEM)
    nw_spec = pl.BlockSpec((1, LANES), lambda b, h, i: (0, 0))
    ctx_spec = pl.BlockSpec((ctx, LANES), lambda b, h, i: (bsz * seq // ctx + b, h))
    lat_spec = pl.BlockSpec((seq, LANES), lambda b, h, i: (b, h))
    q_lat = pl.BlockSpec((tq, LANES), lambda b, h, i: (b * n_lat_q + i, h))

    def stats(rows):
        return pltpu.VMEM((2 * rows, LANES), F32)

    o_lat = pl.pallas_call(
        functools.partial(_attn_kernel, tk=tk, unroll=min(ATTN_UNROLL, seq // tk), with_latent=True,
                          lambda_init=lambda_init),
        grid=(bsz, DIFF_HEADS, n_lat_q),
        in_specs=[smem, q_lat, ctx_spec, ctx_spec, lat_spec, lat_spec, nw_spec],
        out_specs=q_lat,
        out_shape=jax.ShapeDtypeStruct((r, d), BF16),
        scratch_shapes=[pltpu.VMEM((2 * tq, ctx), F32), pltpu.VMEM((seq // tk, 2 * tq, tk), F32),
                        stats(tq), stats(tq), stats(tq)],
        compiler_params=_params(("arbitrary", "arbitrary", "arbitrary")),
        name="diff_attention",
    )(lam_full, q, k, v, k, v, subln_w.reshape(1, LANES))
    tc = min(tq, ctx)
    n_ctx_q = ctx // tc
    q_ctx = pl.BlockSpec((tc, LANES), lambda b, h, i: (bsz * seq // tc + b * n_ctx_q + i, h))
    return pl.pallas_call(
        functools.partial(_attn_kernel, tk=tk, unroll=1, with_latent=False, lambda_init=lambda_init),
        grid=(bsz, DIFF_HEADS, n_ctx_q),
        in_specs=[smem, q_ctx, ctx_spec, ctx_spec, nw_spec, pl.BlockSpec(memory_space=pl.ANY)],
        out_specs=q_ctx,
        out_shape=jax.ShapeDtypeStruct((r, d), BF16),
        scratch_shapes=[pltpu.VMEM((2 * tc, ctx), F32), stats(tc), stats(tc), stats(tc)],
        input_output_aliases={5: 0},
        compiler_params=_params(("arbitrary", "arbitrary", "arbitrary")),
        name="diff_attention_ctx",
    )(lam_full, q, k, v, subln_w.reshape(1, LANES), o_lat)


def _proj_res_kernel(a_ref, w_ref, xs_ref, gate_ref, o_ref):
    o_ref[...] = xs_ref[...] + gate_ref[...] * _dot(a_ref[...], w_ref[...])


def _proj_res(a_bf16, w_bf16, xs, mods_l, dims):
    bsz, seq, ctx = dims
    r, d = xs.shape
    kdim = a_bf16.shape[1]
    tm = _wide_tile(dims)
    lt = seq // tm
    return pl.pallas_call(
        _proj_res_kernel,
        grid=(r // tm,),
        in_specs=[pl.BlockSpec((tm, kdim), lambda i: (i, 0)),
                  pl.BlockSpec((kdim, d), lambda i: (0, 0)),
                  pl.BlockSpec((tm, d), lambda i: (i, 0)),
                  _mod_spec(2, lt, bsz, d)],
        out_specs=pl.BlockSpec((tm, d), lambda i: (i, 0)),
        out_shape=jax.ShapeDtypeStruct((r, d), F32),
        compiler_params=_params(("arbitrary",)),
        name="outproj_gate_residual",
    )(a_bf16, w_bf16, xs, mods_l)


def _diff_layer(xs, mods_l, g1, w_qkv, lam, subln_w, w_out, lambda_init, dims):
    q, k, v = _qkv_rope(xs, g1, mods_l, w_qkv, dims)
    lam32 = lam.astype(F32)
    lam_full = (jnp.exp(jnp.sum(lam32[0] * lam32[1])) - jnp.exp(jnp.sum(lam32[2] * lam32[3]))
                + lambda_init).reshape(1)
    o = _diff_attention(q, k, v, lam_full, subln_w, lambda_init, dims)
    return _proj_res(o, w_out.astype(BF16), xs, mods_l, dims)


def _mlstm_scan_kernel(q_ref, k_ref, v_ref, g_ref, bias_ref, h_ref, c_ref, n_ref, m_ref):
    d = pl.program_id(1)
    s = pl.program_id(2)

    @pl.when(s == 0)
    def _():
        c_ref[...] = jnp.zeros_like(c_ref)
        n_ref[...] = jnp.zeros_like(n_ref)
        m_ref[...] = jnp.zeros_like(m_ref)

    mask = _dir_masks(d)
    tri = jnp.where(mask, 1.0, 0.0).astype(F32)
    gates = g_ref[...] + bias_ref[...]
    ig = gates[:, 0:ML_HEADS]
    lf = _log_sigmoid(gates)
    bcum = jnp.dot(tri, lf, precision=HIGHEST, preferred_element_type=F32)
    btot = jnp.sum(lf, axis=0, keepdims=True)
    bcum_t = bcum.T
    gates_t = gates.T
    row_last = jnp.where(d == 0, CHUNK - 1, 0)
    rsel = lax.broadcasted_iota(jnp.int32, (CHUNK, 1), 0) == row_last
    for h in range(ML_HEADS):
        fcol = ML_HEADS + h
        m_prev = m_ref[h:h + 1, 0:1]
        bcol = bcum[:, fcol:fcol + 1]
        brow = bcum_t[fcol:fcol + 1, :]
        irow = gates_t[h:h + 1, :]
        icol = ig[:, h:h + 1]
        gcol = bcol + m_prev
        dmat = jnp.where(mask, bcol - brow + irow, -jnp.inf)
        mt = jnp.maximum(gcol, jnp.max(dmat, axis=1, keepdims=True))
        q32 = q_ref[:, h * ML_QK_DIM:(h + 1) * ML_QK_DIM] * (ML_QK_DIM ** -0.5)
        qh = q32.astype(BF16)
        kh32 = k_ref[:, h * ML_QK_DIM:(h + 1) * ML_QK_DIM]
        kh = kh32.astype(BF16)
        vh = v_ref[:, h * ML_V_DIM:(h + 1) * ML_V_DIM].astype(BF16)
        sm = _dot_nt(qh, kh) * jnp.exp(dmat - mt)
        inter = jnp.exp(gcol - mt)
        cst = c_ref[h]
        nst = n_ref[h:h + 1, :]
        num = _dot(sm.astype(BF16), vh) + inter * _dot(qh, cst.astype(BF16))
        qn = jnp.sum(q32 * nst, axis=1, keepdims=True)
        den = jnp.sum(sm, axis=1, keepdims=True) + inter * qn
        h_ref[:, h * ML_V_DIM:(h + 1) * ML_V_DIM] = num / jnp.maximum(jnp.abs(den), jnp.exp(-mt))
        m_new = jnp.sum(jnp.where(rsel, mt, 0.0), axis=0, keepdims=True)
        btot_h = btot[:, fcol:fcol + 1]
        wk = jnp.exp(btot_h - bcol + icol - m_new)
        cscale = jnp.exp(btot_h + m_prev - m_new)
        kw = kh32 * wk
        c_ref[h] = cscale * cst + _dot_tn(kw.astype(BF16), vh)
        n_ref[h:h + 1, :] = cscale * nst + jnp.sum(kw, axis=0, keepdims=True)
        m_ref[h:h + 1, :] = jnp.broadcast_to(m_new, (1, LANES))


def _mlstm_scan(t_main, gates_raw, bias_pad, dims):
    bsz, seq, ctx = dims
    r = t_main.shape[0]
    nsteps = (seq + ctx) // CHUNK
    blk = functools.partial(_chunk_block, bsz=bsz, seq=seq, ctx=ctx)
    return pl.pallas_call(
        _mlstm_scan_kernel,
        grid=(bsz, 2, nsteps),
        in_specs=[pl.BlockSpec((CHUNK, ML_QK), lambda b, d, s: (blk(b, d, s), 0)),
                  pl.BlockSpec((CHUNK, ML_QK), lambda b, d, s: (blk(b, d, s), 1)),
                  pl.BlockSpec((CHUNK, ML_V), lambda b, d, s: (blk(b, d, s), 1)),
                  pl.BlockSpec((CHUNK, LANES), lambda b, d, s: (blk(b, d, s), d)),
                  pl.BlockSpec((None, 1, LANES), lambda b, d, s: (d, 0, 0))],
        out_specs=pl.BlockSpec((None, CHUNK, ML_V), lambda b, d, s: (d, blk(b, d, s), 0)),
        out_shape=jax.ShapeDtypeStruct((2, r, ML_V), F32),
        scratch_shapes=[pltpu.VMEM((ML_HEADS, ML_QK_DIM, ML_V_DIM), F32),
                        pltpu.VMEM((8, ML_QK_DIM), F32),
                        pltpu.VMEM((8, LANES), F32)],
        compiler_params=_params(("arbitrary", "arbitrary", "arbitrary")),
        name="mlstm_scan",
    )(t_main, t_main, t_main, gates_raw, bias_pad)


def _mlstm_out_kernel(hf_ref, hb_ref, o_ref_in, nw_ref, w_ref, xs_ref, gate_ref, out_ref):
    u = None
    for h in range(ML_HEADS):
        cs = slice(h * ML_V_DIM, (h + 1) * ML_V_DIM)
        a = _sigmoid(o_ref_in[:, cs]) * (hf_ref[:, cs] + hb_ref[:, cs])
        a = (_rms(a) * nw_ref[...]).astype(BF16)
        part = _dot(a, w_ref[cs, :])
        u = part if u is None else u + part
    out_ref[...] = xs_ref[...] + gate_ref[...] * u


def _mlstm_out(h2, t_main, norm_w, w_out_bf16, xs, mods_l, dims):
    bsz, seq, ctx = dims
    r, d = xs.shape
    tm = _wide_tile(dims)
    lt = seq // tm
    o_blk = (2 * ML_QK + ML_V) // ML_V
    return pl.pallas_call(
        _mlstm_out_kernel,
        grid=(r // tm,),
        in_specs=[pl.BlockSpec((None, tm, ML_V), lambda i: (0, i, 0)),
                  pl.BlockSpec((None, tm, ML_V), lambda i: (1, i, 0)),
                  pl.BlockSpec((tm, ML_V), lambda i: (i, o_blk)),
                  pl.BlockSpec((1, ML_V_DIM), lambda i: (0, 0)),
                  pl.BlockSpec((ML_V, d), lambda i: (0, 0)),
                  pl.BlockSpec((tm, d), lambda i: (i, 0)),
                  _mod_spec(2, lt, bsz, d)],
        out_specs=pl.BlockSpec((tm, d), lambda i: (i, 0)),
        out_shape=jax.ShapeDtypeStruct((r, d), F32),
        compiler_params=_params(("arbitrary",)),
        name="mlstm_norm_outproj",
    )(h2, h2, t_main, norm_w.reshape(1, ML_V_DIM), w_out_bf16, xs, mods_l)


def _mlstm_layer(xs, mods_l, g1, w_in, b_gates, norm_w, w_out, dims):
    d = xs.shape[1]
    main = 2 * ML_QK + 2 * ML_V
    w_main = w_in[:, :main].astype(BF16)
    wg = w_in[:, main:].reshape(d, 4, ML_HEADS)
    w_g = jnp.zeros((d, 2 * LANES), F32)
    bias = jnp.zeros((2, 1, LANES), F32)
    for dr in range(2):
        w_g = w_g.at[:, dr * LANES:dr * LANES + 2 * ML_HEADS].set(
            wg[:, 2 * dr:2 * dr + 2].reshape(d, 2 * ML_HEADS))
        bias = bias.at[dr, 0, :2 * ML_HEADS].set(b_gates.astype(F32)[2 * dr:2 * dr + 2].reshape(2 * ML_HEADS))
    t_main, gates_raw = _inproj(xs, g1, mods_l, w_main, w_g.astype(BF16), dims)
    h2 = _mlstm_scan(t_main, gates_raw, bias, dims)
    return _mlstm_out(h2, t_main, norm_w, w_out.astype(BF16), xs, mods_l, dims)


def _router_kernel(x_ref, g_ref, sh_ref, sc_ref, wr_ref, br_ref, h_ref, idx_ref, gate_ref, rank_ref, cnt_ref,
                   carry_ref):
    i = pl.program_id(0)

    @pl.when(i == 0)
    def _():
        carry_ref[...] = jnp.zeros_like(carry_ref)

    h = _norm_mod(x_ref[...], g_ref[...], sh_ref[...], sc_ref[...])
    tm, d = h.shape
    h_ref[...] = _pack_bf16_pairs(h)
    logits = lax.dot_general(wr_ref[...], h, (((1,), (1,)), ((), ())), precision=HIGHEST,
                             preferred_element_type=F32) + br_ref[...]
    eidx = lax.broadcasted_iota(jnp.int32, logits.shape, 0)
    work = logits
    vals, idxs = [], []
    picked = jnp.zeros(logits.shape, F32)
    for _ in range(TOP_K):
        mx = jnp.max(work, axis=0, keepdims=True)
        ix = jnp.min(jnp.where(work == mx, eidx, N_EXPERTS), axis=0, keepdims=True)
        sel = eidx == ix
        vals.append(mx)
        idxs.append(ix)
        picked = jnp.where(sel, 1.0, picked)
        work = jnp.where(sel, -jnp.inf, work)
    es = [jnp.exp(v - vals[0]) for v in vals]
    tot = es[0] + es[1] + es[2] + es[3]
    jj = lax.broadcasted_iota(jnp.int32, (tm, tm), 0)
    ii = lax.broadcasted_iota(jnp.int32, (tm, tm), 1)
    upper = jnp.where(jj <= ii, 1.0, 0.0).astype(BF16)
    incl = _dot(picked.astype(BF16), upper)
    carry = carry_ref[:, 0:1]
    excl = incl - picked + carry
    for k in range(TOP_K):
        idx_ref[k:k + 1, :] = idxs[k]
        gate_ref[k:k + 1, :] = es[k] / tot
        rk = jnp.sum(jnp.where(eidx == idxs[k], excl, 0.0), axis=0, keepdims=True)
        rank_ref[k:k + 1, :] = rk.astype(jnp.int32)
    new_carry = carry + jnp.sum(picked, axis=1, keepdims=True)
    carry_ref[...] = jnp.broadcast_to(new_carry, carry_ref.shape)
    cnt_ref[...] = jnp.broadcast_to(new_carry, cnt_ref.shape).astype(jnp.int32)


def _router(xs, g2, mods_l, w_router, b_router, dims):
    bsz, seq, ctx = dims
    r, d = xs.shape
    tm = _wide_tile(dims)
    lt = seq // tm
    tok_spec = pl.BlockSpec((TOP_K, tm), lambda i: (0, i))
    return pl.pallas_call(
        _router_kernel,
        grid=(r // tm,),
        in_specs=[pl.BlockSpec((tm, d), lambda i: (i, 0)),
                  pl.BlockSpec((1, d), lambda i: (0, 0)),
                  _mod_spec(3, lt, bsz, d), _mod_spec(4, lt, bsz, d),
                  pl.BlockSpec((N_EXPERTS, d), lambda i: (0, 0)),
                  pl.BlockSpec((N_EXPERTS, 1), lambda i: (0, 0))],
        out_specs=[pl.BlockSpec((tm, d // 2), lambda i: (i, 0)), tok_spec, tok_spec, tok_spec,
                   pl.BlockSpec((N_EXPERTS, LANES), lambda i: (0, 0))],
        out_shape=[jax.ShapeDtypeStruct((r, d // 2), jnp.int32),
                   jax.ShapeDtypeStruct((TOP_K, r), jnp.int32),
                   jax.ShapeDtypeStruct((TOP_K, r), F32),
                   jax.ShapeDtypeStruct((TOP_K, r), jnp.int32),
                   jax.ShapeDtypeStruct((N_EXPERTS, LANES), jnp.int32)],
        scratch_shapes=[pltpu.VMEM((N_EXPERTS, LANES), F32)],
        compiler_params=_params(("arbitrary",)),
        name="norm_mod_router_top4",
    )(xs, g2.reshape(1, d), mods_l, mods_l, w_router.T, b_router.reshape(N_EXPERTS, 1))


def _pack_bf16_pairs(x):
    w = x.shape[1]
    xr = x.astype(BF16).astype(F32)
    hi = lax.bitcast_convert_type(xr[:, :w // 2], jnp.int32)
    lo = lax.bitcast_convert_type(xr[:, w // 2:], jnp.int32)
    return hi | lax.shift_right_logical(lo, 16)


def _unpack_bf16_pairs(xp):
    hi = lax.bitcast_convert_type(xp & jnp.int32(-65536), F32)
    lo = lax.bitcast_convert_type(lax.shift_left(xp, 16), F32)
    return hi, lo


def _expert_kernel(be_ref, nb_ref, first_ref, slot_ref, next_ref, x_ref, wgu_hbm, bgu_ref, wdn_hbm, bdn_ref, *rest,
                   layer):
    y_ref, wgu_buf, wdn_buf, wgu_bf, wdn_bf, sems = rest[-6:]
    def fetch(e, s):
        return (pltpu.make_async_copy(wgu_hbm.at[layer, e], wgu_buf.at[s], sems.at[s, 0]),
                pltpu.make_async_copy(wdn_hbm.at[layer, e], wdn_buf.at[s], sems.at[s, 1]))

    def block(i, rows):
        active = i < nb_ref[0]
        slot = slot_ref[i]
        e = be_ref[i]

        @pl.when(jnp.logical_and(active, first_ref[i] == 1))
        def _():
            @pl.when(i == 0)
            def _():
                for cp in fetch(be_ref[0], 0):
                    cp.start()

            for cp in fetch(e, slot):
                cp.wait()
            wgu_bf[...] = wgu_buf[slot].astype(BF16)
            wdn_bf[...] = wdn_buf[slot].astype(BF16)

            @pl.when(next_ref[i] >= 0)
            def _():
                for cp in fetch(next_ref[i], 1 - slot):
                    cp.start()

        @pl.when(active)
        def _():
            de = wdn_bf.shape[0]
            half = x_ref.shape[1]
            xa, xb = _unpack_bf16_pairs(x_ref[rows, :])
            gu = _dot(xa.astype(BF16), wgu_bf[:half, :]) + _dot(xb.astype(BF16), wgu_bf[half:, :]) + bgu_ref[e]
            g = jnp.minimum(gu[:, :de], SWIGLU_LIMIT)
            u = jnp.clip(gu[:, de:], -SWIGLU_LIMIT, SWIGLU_LIMIT)
            a = (u + 1.0) * g * _sigmoid(SWIGLU_ALPHA * g)
            y_ref[rows, :] = _pack_bf16_pairs(_dot(a.astype(BF16), wdn_bf[...]) + bdn_ref[e])

    step = pl.program_id(0)
    for sub in range(MOE_BLOCKS_PER_STEP):
        block(step * MOE_BLOCKS_PER_STEP + sub, slice(sub * MOE_BLOCK, (sub + 1) * MOE_BLOCK))


def _experts(xb, block_expert, nb_used, grp_first, grp_slot, grp_next, w_gu, b_gu, w_dn, b_dn, layer,
             total_rows, row_offset=0, y_prev=None):
    nrows, half = xb.shape
    bm = MOE_BLOCK
    depth, ne, d, two_de = w_gu.shape
    de = two_de // 2
    nsp = 5
    rows_step = bm * MOE_BLOCKS_PER_STEP
    assert nrows % rows_step == 0 and row_offset % rows_step == 0
    step0 = row_offset // rows_step
    in_specs = [pl.BlockSpec((rows_step, half), lambda i, *_: (i, 0)),
                pl.BlockSpec(memory_space=pl.ANY),
                pl.BlockSpec((None, ne, 1, two_de), lambda i, *_: (layer, 0, 0, 0)),
                pl.BlockSpec(memory_space=pl.ANY),
                pl.BlockSpec((None, ne, 1, d), lambda i, *_: (layer, 0, 0, 0))]
    args = [block_expert, nb_used, grp_first, grp_slot, grp_next, xb, w_gu, b_gu.reshape(depth, ne, 1, two_de),
            w_dn, b_dn.reshape(depth, ne, 1, d)]
    aliases = {}
    if y_prev is not None:
        in_specs.append(pl.BlockSpec(memory_space=pl.ANY))
        args.append(y_prev)
        aliases = {len(args) - 1: 0}
    grid_spec = pltpu.PrefetchScalarGridSpec(
        num_scalar_prefetch=nsp,
        grid=(nrows // rows_step,),
        in_specs=in_specs,
        out_specs=pl.BlockSpec((rows_step, d // 2), lambda i, *_: (step0 + i, 0)),
        scratch_shapes=[pltpu.VMEM((2, d, two_de), F32), pltpu.VMEM((2, de, d), F32),
                        pltpu.VMEM((d, two_de), BF16), pltpu.VMEM((de, d), BF16),
                        pltpu.SemaphoreType.DMA((2, 2))],
    )
    return pl.pallas_call(
        functools.partial(_expert_kernel, layer=layer),
        grid_spec=grid_spec,
        out_shape=jax.ShapeDtypeStruct((total_rows, d // 2), jnp.int32),
        input_output_aliases=aliases,
        compiler_params=_params(("arbitrary",)),
        name="moe_expert_ffn",
    )(*args)


def _sc_row_tokens(dest, n_rows):
    top_k, r = dest.shape
    lanes = SC_LANES
    chunk = r // 2
    assert n_rows % lanes == 0 and r % 2 == 0 and chunk % lanes == 0
    fill_mask = (1 << (r.bit_length() - 1)) - 1
    mesh = plsc.VectorSubcoreMesh(core_axis_name="c", subcore_axis_name="s",
                                  num_cores=SC_CORES, num_subcores=SC_SUBCORES)

    def body(dest_hbm, out_hbm, dest_v, table_v):
        wid = lax.axis_index("s") * SC_CORES + lax.axis_index("c")

        @pl.when(wid == 0)
        def _():
            lane_id = lax.iota(jnp.int32, lanes)

            @pl.loop(0, n_rows // lanes, unroll=8)
            def _(c):
                table_v[pl.ds(c * lanes, lanes)] = (lane_id + c * lanes) & fill_mask

            for k in range(top_k):
                for half in range(2):
                    t0 = half * chunk
                    pltpu.sync_copy(dest_hbm.at[pl.ds(k * r + t0, chunk)], dest_v)

                    @pl.loop(0, chunk // lanes, unroll=8)
                    def _(v):
                        d = dest_v[pl.ds(v * lanes, lanes)]
                        plsc.store_scatter(table_v, [d], lane_id + (t0 + v * lanes))

            pltpu.sync_copy(table_v, out_hbm)

    cp = pltpu.CompilerParams(needs_layout_passes=False)
    return pl.kernel(
        body,
        out_type=jax.ShapeDtypeStruct((n_rows,), jnp.int32),
        mesh=mesh,
        scratch_types=[pltpu.VMEM((chunk,), jnp.int32), pltpu.VMEM((n_rows,), jnp.int32)],
        compiler_params=cp,
        name="sc_row_tokens",
    )(dest.reshape(top_k * r))


def _sc_gather_rows(table, idx):
    n = idx.shape[0]
    width = table.shape[1]
    nw = SC_CORES * SC_SUBCORES
    nb = SC_GATHER_ROWS
    per_w = n // nw
    assert n % nw == 0 and per_w % nb == 0, (n, nw, nb)
    steps = per_w // nb
    mesh = plsc.VectorSubcoreMesh(core_axis_name="c", subcore_axis_name="s",
                                  num_cores=SC_CORES, num_subcores=SC_SUBCORES)

    def body(table_hbm, idx_hbm, out_hbm, idx_a, idx_b, rows_a, rows_b, sem_a, sem_b):
        wid = lax.axis_index("s") * SC_CORES + lax.axis_index("c")
        base = wid * per_w
        slots = ((idx_a, rows_a, sem_a), (idx_b, rows_b, sem_b))

        def gather(j, slot):
            idx_v, rows_v, sem = slots[slot]
            return pltpu.make_async_copy(table_hbm.at[idx_v], rows_v, sem)

        def start(j, slot):
            off = pl.multiple_of(base + j * nb, 8)
            pltpu.sync_copy(idx_hbm.at[pl.ds(off, nb)], slots[slot][0])
            gather(j, slot).start()

        def finish(j, slot):
            off = pl.multiple_of(base + j * nb, 8)
            gather(j, slot).wait()
            pltpu.sync_copy(slots[slot][1], out_hbm.at[pl.ds(off, nb)])

        start(0, 0)

        @pl.loop(0, steps // 2)
        def _(p):
            j = 2 * p
            start(j + 1, 1)
            finish(j, 0)
            if steps % 2 == 1:
                start(j + 2, 0)
            else:
                @pl.when(j + 2 < steps)
                def _():
                    start(j + 2, 0)
            finish(j + 1, 1)

        if steps % 2 == 1:
            finish(steps - 1, 0)

    return pl.kernel(
        body,
        out_type=jax.ShapeDtypeStruct((n, width), table.dtype),
        mesh=mesh,
        scratch_types=[pltpu.VMEM((nb,), jnp.int32), pltpu.VMEM((nb,), jnp.int32),
                       pltpu.VMEM((nb, width), table.dtype), pltpu.VMEM((nb, width), table.dtype),
                       pltpu.SemaphoreType.DMA, pltpu.SemaphoreType.DMA],
        name="sc_gather_rows",
    )(table, idx)


def _combine_kernel(xs_ref, yg_ref, gt_ref, gate_ref, fg_ref, *rest, final):
    o_ref = rest[-1]
    f_hi = f_lo = None
    for k in range(TOP_K):
        y_hi, y_lo = _unpack_bf16_pairs(yg_ref[k])
        gk = gt_ref[:, k:k + 1]
        f_hi = y_hi * gk if f_hi is None else f_hi + y_hi * gk
        f_lo = y_lo * gk if f_lo is None else f_lo + y_lo * gk
    out = xs_ref[...] + gate_ref[...] * jnp.concatenate([f_hi, f_lo], axis=1)
    if final:
        out = _rms(out) * fg_ref[...]
    o_ref[...] = out


def _combine(xs, yg, gate_t, mods_l, final_g, final, dims, rows_out, tile0=0, prev=None):
    bsz, seq, ctx = dims
    r, d = xs.shape
    tm = _wide_tile(dims)
    lt = seq // tm
    n_tiles = yg.shape[1] // tm
    in_specs = [pl.BlockSpec((tm, d), lambda i: (tile0 + i, 0)),
                pl.BlockSpec((TOP_K, tm, d // 2), lambda i: (0, i, 0)),
                pl.BlockSpec((tm, TOP_K), lambda i: (tile0 + i, 0)),
                pl.BlockSpec((None, None, 1, d), lambda i: (5, jnp.minimum((tile0 + i) // lt, bsz), 0, 0)),
                pl.BlockSpec((1, d), lambda i: (0, 0))]
    args = [xs, yg, gate_t, mods_l, final_g.reshape(1, d)]
    aliases = {}
    if prev is not None:
        in_specs.append(pl.BlockSpec(memory_space=pl.ANY))
        args.append(prev)
        aliases = {len(args) - 1: 0}
    return pl.pallas_call(
        functools.partial(_combine_kernel, final=final),
        grid=(n_tiles,),
        in_specs=in_specs,
        out_specs=pl.BlockSpec((tm, d), lambda i: (tile0 + i, 0)),
        out_shape=jax.ShapeDtypeStruct((rows_out, d), F32),
        input_output_aliases=aliases,
        compiler_params=_params(("arbitrary",)),
        name="moe_combine_residual",
    )(*args)


def _moe_layer(xs, mods_l, g2, w_router, b_router, w_gu, b_gu, w_dn, b_dn, final_g, layer, final, dims):
    r, d = xs.shape
    bm = MOE_BLOCK
    hp, top_idx, gate, rank, cnt = _router(xs, g2, mods_l, w_router, b_router, dims)
    counts = cnt[:, 0]
    padded = (counts + bm - 1) // bm * bm
    pad_ends = jnp.cumsum(padded)
    pad_starts = pad_ends - padded
    eids = jnp.arange(N_EXPERTS, dtype=jnp.int32)
    onehot = top_idx[None] == eids[:, None, None]
    dest = jnp.sum(jnp.where(onehot, pad_starts[:, None, None], 0), axis=0) + rank
    n_blocks = -(-(r * TOP_K + N_EXPERTS * (bm - 1)) // bm)
    n_blocks = -(-n_blocks // MOE_BLOCKS_PER_STEP) * MOE_BLOCKS_PER_STEP
    blk_start = jnp.arange(n_blocks, dtype=jnp.int32) * bm
    block_expert = jnp.minimum(jnp.sum((pad_ends[None, :] <= blk_start[:, None]).astype(jnp.int32), axis=1),
                               N_EXPERTS - 1)
    nb_total = pad_ends[-1] // bm
    nb_used = nb_total.astype(jnp.int32).reshape(1)
    blk = jnp.arange(n_blocks, dtype=jnp.int32)
    prev_e = jnp.concatenate([jnp.full((1,), -1, jnp.int32), block_expert[:-1]])
    grp_first = ((block_expert != prev_e) & (blk < nb_total)).astype(jnp.int32)
    grp_slot = (jnp.cumsum(grp_first) - 1) % 2
    later = (eids[None, :] > eids[:, None]) & (counts[None, :] > 0)
    next_of_e = jnp.min(jnp.where(later, eids[None, :], N_EXPERTS), axis=1)
    next_of_e = jnp.where(next_of_e == N_EXPERTS, -1, next_of_e)
    grp_next = jnp.sum(jnp.where(block_expert[:, None] == eids[None, :], next_of_e[None, :], 0), axis=1)
    row_tok = _sc_row_tokens(dest, n_blocks * bm)
    gran = SC_CORES * SC_SUBCORES * SC_GATHER_ROWS // bm
    split = -(-(n_blocks // 2) // gran) * gran
    assert 0 < split < n_blocks and (n_blocks - split) % gran == 0 and split % MOE_BLOCKS_PER_STEP == 0
    total_rows = n_blocks * bm
    grp_slot = grp_slot.astype(jnp.int32)
    grp_next = grp_next.astype(jnp.int32)
    next_start = jnp.sum(jnp.where(grp_next[:, None] == eids[None, :], (pad_starts // bm)[None, :], 0), axis=1)
    next_a = jnp.where((grp_next >= 0) & (next_start < split), grp_next, -1)[:split]
    nb_a = jnp.minimum(nb_total, split).astype(jnp.int32).reshape(1)
    nb_b = jnp.clip(nb_total - split, 0, n_blocks - split).astype(jnp.int32).reshape(1)
    first_b = grp_first[split:].at[0].set((nb_total > split).astype(jnp.int32))
    slot_b = ((jnp.cumsum(first_b) - 1) % 2).astype(jnp.int32)
    xb_a = _sc_gather_rows(hp, row_tok[:split * bm])
    xb_b = _sc_gather_rows(hp, row_tok[split * bm:])
    yb = _experts(xb_a, block_expert[:split], nb_a, grp_first[:split], grp_slot[:split], next_a,
                  w_gu, b_gu, w_dn, b_dn, layer, total_rows)
    yb = _experts(xb_b, block_expert[split:], nb_b, first_b, slot_b, grp_next[split:],
                  w_gu, b_gu, w_dn, b_dn, layer, total_rows, row_offset=split * bm, y_prev=yb)
    bsz, seq, _ = dims
    rows_out = bsz * seq if final else r
    tmw = _wide_tile(dims)
    rows_a = (rows_out // tmw + 1) // 2 * tmw
    gate_t = gate.T

    def gathered(lo, hi):
        return _sc_gather_rows(yb, dest[:, lo:hi].reshape(-1)).reshape(TOP_K, hi - lo, d // 2)

    yg_a = gathered(0, rows_a)
    yg_b = gathered(rows_a, rows_out)
    out = _combine(xs, yg_a, gate_t, mods_l, final_g, final, dims, rows_out)
    return _combine(xs, yg_b, gate_t, mods_l, final_g, final, dims, rows_out, tile0=rows_a // tmw, prev=out)


def kernel(x, c, ctx, c_ctx, ada_w, ada_b, norm1_g, norm2_g, ssd_w_in, ssd_conv_w, ssd_conv_b, ssd_dt_bias,
           ssd_a_log, ssd_d, ssd_norm_w, ssd_w_out, diff_w_qkv, diff_lam, diff_subln_w, diff_w_out, ml_w_in,
           ml_b_gates, ml_norm_w, ml_w_out, moe_w_router, moe_b_router, moe_w_gu, moe_b_gu, moe_w_dn, moe_b_dn,
           final_g):
    bsz, seq, d = x.shape
    n_ctx = ctx.shape[1]
    depth = ada_w.shape[0]
    dims = (bsz, seq, n_ctx)
    n_lat = bsz * seq
    xs = jnp.concatenate([x.reshape(n_lat, d), ctx.reshape(bsz * n_ctx, d)], axis=0)
    cond_rows = jnp.zeros((8, d), F32).at[:bsz].set(c).at[bsz].set(c_ctx)
    mods = _mods(cond_rows, ada_w, ada_b)
    mods = mods[:, :, :bsz + 1].reshape(depth, N_MOD, bsz + 1, 1, d)
    for i in range(depth):
        mods_l = mods[i]
        kind, j = i % 3, i // 3
        if kind == 0:
            xs = _ssd_layer(xs, mods_l, norm1_g[i], ssd_w_in[j], ssd_conv_w[j], ssd_conv_b[j], ssd_dt_bias[j],
                            ssd_a_log[j], ssd_d[j], ssd_norm_w[j], ssd_w_out[j], dims)
        elif kind == 1:
            lambda_init = 0.8 - 0.6 * math.exp(-0.3 * i)
            xs = _diff_layer(xs, mods_l, norm1_g[i], diff_w_qkv[j], diff_lam[j], diff_subln_w[j], diff_w_out[j],
                             lambda_init, dims)
        else:
            xs = _mlstm_layer(xs, mods_l, norm1_g[i], ml_w_in[j], ml_b_gates[j], ml_norm_w[j], ml_w_out[j], dims)
        xs = _moe_layer(xs, mods_l, norm2_g[i], moe_w_router[i], moe_b_router[i], moe_w_gu, moe_b_gu,
                        moe_w_dn, moe_b_dn, final_g, i, i == depth - 1, dims)
    return xs.reshape(bsz, seq, d)
```
